```python
import math
import jax, jax.numpy as jnp
from jax import lax
import numpy as np

D_MODEL = 2048
BATCH = 1
SEQ = 8192
DEPTH = 2
DEC_BATCH = 128
DEC_SEQ = 4
PAST_LEN = 2048
PAGE_SIZE = 128

MIX_WIDTH = D_MODEL
A_HEAD_DIM = 128
A_WIDTH = MIX_WIDTH // 2
A_HEADS = A_WIDTH // A_HEAD_DIM
A_PATTERNS = ((128, 1), (512, 4), (2048, 16))
A_WINDOW_MAX = 2048
B_WIDTH = MIX_WIDTH // 4
B_GROUPS = 4
B_GROUP_W = B_WIDTH // B_GROUPS
CHUNK = 128
C_WIDTH = MIX_WIDTH - A_WIDTH - B_WIDTH
C_GROUP_W = 16
C_GROUPS = C_WIDTH // C_GROUP_W
SSM_P = 64
N_IN = 3 * A_WIDTH + 2 * B_WIDTH + C_WIDTH
D_FF = 5632
CONV_W = 3
EPS = 1e-6

kernel_name = "hybrid_dilated_gmlp_s5_step"


def rmsnorm(x, g):
    xf = x.astype(jnp.float32)
    y = xf * lax.rsqrt(jnp.mean(xf * xf, axis=-1, keepdims=True) + EPS)
    return (y * g.astype(jnp.float32)).astype(x.dtype)


def split_proj(p):
    sizes = (A_WIDTH, A_WIDTH, A_WIDTH, B_WIDTH, B_WIDTH, C_WIDTH)
    outs, start = [], 0
    for size in sizes:
        outs.append(p[..., start:start + size])
        start += size
    return outs


def dilated_band_attention(q, k, v, dil, w_sub):
    bsz, s, h, hd = q.shape
    length = s // dil
    blk = w_sub
    nb = -(-length // blk)
    pad = nb * blk - length

    def to_sub(a):
        return a.reshape(bsz, length, dil, h, hd).transpose(0, 2, 1, 3, 4)

    qb = jnp.pad(to_sub(q), ((0, 0), (0, 0), (0, pad), (0, 0), (0, 0))).reshape(bsz, dil, nb, blk, h, hd)

    def band(a):
        a = jnp.pad(to_sub(a), ((0, 0), (0, 0), (blk, pad), (0, 0), (0, 0))).reshape(bsz, dil, nb + 1, blk, h, hd)
        return jnp.concatenate([a[:, :, :-1], a[:, :, 1:]], axis=3)

    kb, vb = band(k), band(v)
    scores = jnp.einsum('brnihd,brnjhd->brnhij', qb.astype(jnp.float32), kb.astype(jnp.float32)) / math.sqrt(hd)
    i = jnp.arange(blk)[:, None]
    j = jnp.arange(2 * blk)[None, :]
    delta = blk + i - j
    n = jnp.arange(nb)[:, None, None]
    valid = (delta >= 0) & (delta <= w_sub) & ((n - 1) * blk + j >= 0)
    scores = jnp.where(valid[:, None], scores, -jnp.inf)
    m = jnp.max(scores, axis=-1, keepdims=True)
    p = jnp.exp(scores - m)
    l = jnp.sum(p, axis=-1, keepdims=True)
    o = jnp.einsum('brnhij,brnjhd->brnihd', p / l, vb.astype(jnp.float32))
    lse = (m + jnp.log(l))[..., 0]
    o = o.reshape(bsz, dil, nb * blk, h, hd)[:, :, :length].transpose(0, 2, 1, 3, 4).reshape(bsz, s, h, hd)
    lse = lse.transpose(0, 1, 2, 4, 3).reshape(bsz, dil, nb * blk, h)[:, :, :length]
    lse = lse.transpose(0, 2, 1, 3).reshape(bsz, s, h)
    return o, lse


def dilated_gather_attention(q, k_all, v_all, w_buf, dil, w_sub):
    ds, hd = q.shape[1], q.shape[3]
    steps = jnp.arange(w_sub + 1)
    idx = w_buf + jnp.arange(ds)[:, None] - steps[None, :] * dil
    valid = idx >= 0
    idx = jnp.maximum(idx, 0)
    kg = k_all[:, idx].astype(jnp.float32)
    vg = v_all[:, idx].astype(jnp.float32)
    scores = jnp.einsum('bqhd,bqkhd->bhqk', q.astype(jnp.float32), kg) / math.sqrt(hd)
    scores = jnp.where(valid, scores, -jnp.inf)
    m = jnp.max(scores, axis=-1, keepdims=True)
    p = jnp.exp(scores - m)
    l = jnp.sum(p, axis=-1, keepdims=True)
    o = jnp.einsum('bhqk,bqkhd->bqhd', p / l, vg)
    lse = (m + jnp.log(l))[..., 0].transpose(0, 2, 1)
    return o, lse


def merge_patterns(results):
    outs = jnp.stack([o for o, _ in results])
    lses = jnp.stack([l for _, l in results])
    w = jax.nn.softmax(lses, axis=0)
    return jnp.sum(w[..., None] * outs, axis=0)


def spatial_mix(v, ws, bs, chunked):
    wm = jnp.tril(ws)
    if chunked:
        bsz, s = v.shape[:2]
        vc = v.reshape(bsz, s // CHUNK, CHUNK, B_GROUPS, B_GROUP_W)
        out = jnp.einsum('gij,bnjgc->bnigc', wm, vc) + bs.T[:, :, None]
        return out.reshape(v.shape)
    n = v.shape[1]
    return jnp.einsum('gij,bjgc->bigc', wm[:, :n, :n], v) + bs.T[:n, :, None]


def _lin_rec_combine(e1, e2):
    a1, b1 = e1
    a2, b2 = e2
    return a1 * a2, a2 * b1 + b2


def s5_scan(u, h0, a_re, a_im, log_dt, b_re, b_im, c_re, c_im, d):
    lam = lax.complex(a_re.astype(jnp.float32), a_im.astype(jnp.float32))
    dt = jnp.exp(log_dt.astype(jnp.float32))[:, None]
    a_bar = jnp.exp(lam * dt)
    b_bar = ((a_bar - 1.0) / lam)[:, :, None] * lax.complex(b_re.astype(jnp.float32), b_im.astype(jnp.float32))
    c = lax.complex(c_re.astype(jnp.float32), c_im.astype(jnp.float32))
    uf = u.astype(jnp.float32)
    bu = jnp.einsum('bsgn,gpn->bsgp', uf.astype(jnp.complex64), b_bar)
    bu = bu.at[:, 0].add(a_bar * h0)
    a_seq = jnp.broadcast_to(a_bar, bu.shape)
    _, h = lax.associative_scan(_lin_rec_combine, (a_seq, bu), axis=1)
    y = jnp.einsum('gnp,bsgp->bsgn', c, h).real + d.astype(jnp.float32) * uf
    return y, h[:, -1]


def conv_ffn(h, buf, w_up, conv_w, conv_b, w_down):
    up = h @ w_up
    s = up.shape[1]
    padded = jnp.concatenate([buf.astype(up.dtype), up], axis=1)
    conv = conv_b
    for tap in range(CONV_W):
        conv = conv + conv_w[tap] * padded[:, tap:tap + s]
    gate, val = jnp.split(conv, 2, axis=-1)
    return (jax.nn.silu(gate) * val) @ w_down, padded[:, s:]


def trunk_layer(x, win_kv, ssm_h0, conv_buf, g_mix, w_in, g_out_a, g_out_b, g_out_c,
                gmlp_gv, gmlp_ws, gmlp_bs, ssm_a_re, ssm_a_im, ssm_log_dt, ssm_b_re, ssm_b_im,
                ssm_c_re, ssm_c_im, ssm_d, ssm_w_glu, w_out, g_ffn, w_up, conv_w, conv_b, w_down):
    bsz, s, _ = x.shape
    h = rmsnorm(x, g_mix)
    q, k, v, bu, bv, cu = split_proj(h @ w_in)
    q = q.reshape(bsz, s, A_HEADS, A_HEAD_DIM)
    k = k.reshape(bsz, s, A_HEADS, A_HEAD_DIM)
    v = v.reshape(bsz, s, A_HEADS, A_HEAD_DIM)
    if win_kv is None:
        res = [dilated_band_attention(q, k, v, dil, win // dil) for win, dil in A_PATTERNS]
        keep = min(A_WINDOW_MAX, s)
        new_k, new_v = k[:, s - keep:], v[:, s - keep:]
    else:
        buf_k, buf_v = win_kv
        w_buf = buf_k.shape[1]
        k_all = jnp.concatenate([buf_k.astype(k.dtype), k], axis=1)
        v_all = jnp.concatenate([buf_v.astype(v.dtype), v], axis=1)
        res = [dilated_gather_attention(q, k_all, v_all, w_buf, dil, win // dil) for win, dil in A_PATTERNS]
        new_k, new_v = k, v
    a_out = merge_patterns(res).reshape(bsz, s, A_WIDTH).astype(x.dtype)
    gu = jax.nn.gelu(bu).reshape(bsz, s, B_GROUPS, B_GROUP_W)
    gv = rmsnorm(jax.nn.gelu(bv).reshape(bsz, s, B_GROUPS, B_GROUP_W), gmlp_gv.reshape(B_GROUPS, B_GROUP_W))
    b_out = (gu * spatial_mix(gv, gmlp_ws, gmlp_bs, win_kv is None)).reshape(bsz, s, B_WIDTH)
    if ssm_h0 is None:
        h0 = jnp.zeros((bsz, C_GROUPS, SSM_P), jnp.complex64)
    else:
        h0 = lax.complex(ssm_h0[0].astype(jnp.float32), ssm_h0[1].astype(jnp.float32))
    y_ssm, h_last = s5_scan(cu.reshape(bsz, s, C_GROUPS, C_GROUP_W), h0, ssm_a_re, ssm_a_im, ssm_log_dt,
                            ssm_b_re, ssm_b_im, ssm_c_re, ssm_c_im, ssm_d.reshape(C_GROUPS, C_GROUP_W))
    y_ssm = jax.nn.gelu(y_ssm.reshape(bsz, s, C_WIDTH)).astype(x.dtype)
    glu_a, glu_b = jnp.split(y_ssm @ ssm_w_glu, 2, axis=-1)
    c_out = glu_a * jax.nn.sigmoid(glu_b)
    mixed = jnp.concatenate([rmsnorm(a_out, g_out_a), rmsnorm(b_out, g_out_b), rmsnorm(c_out, g_out_c)], axis=-1)
    x = x + mixed @ w_out
    if conv_buf is None:
        conv_buf = jnp.zeros((bsz, CONV_W - 1, 2 * D_FF), x.dtype)
    f_out, new_conv = conv_ffn(rmsnorm(x, g_ffn), conv_buf, w_up, conv_w, conv_b, w_down)
    x = x + f_out
    return x, (new_k, new_v, gv.reshape(bsz, s, B_WIDTH), h_last.real, h_last.imag, new_conv)


def setup_inputs(seed: int = 0) -> dict:
    key = jax.random.key(seed)
    ks = iter(jax.random.split(key, 32))
    f32 = jnp.float32

    def nrm(shape, scale):
        return scale * jax.random.normal(next(ks), shape, f32)

    def gain(shape):
        return 1.0 + 0.01 * jax.random.normal(next(ks), shape, f32)

    w_buf = min(A_WINDOW_MAX, PAST_LEN)
    n_idx = jnp.arange(SSM_P, dtype=f32)
    return {
        "x_prompt": nrm((BATCH, SEQ, D_MODEL), 1.0),
        "x_sample": nrm((DEC_BATCH, DEC_SEQ, D_MODEL), 1.0),
        "cache_win_k": nrm((DEPTH, DEC_BATCH, w_buf, A_HEADS, A_HEAD_DIM), 1.0),
        "cache_win_v": nrm((DEPTH, DEC_BATCH, w_buf, A_HEADS, A_HEAD_DIM), 1.0),
        "state_ssm_re": nrm((DEPTH, DEC_BATCH, C_GROUPS, SSM_P), 0.5),
        "state_ssm_im": nrm((DEPTH, DEC_BATCH, C_GROUPS, SSM_P), 0.5),
        "state_ffn_conv": nrm((DEPTH, DEC_BATCH, CONV_W - 1, 2 * D_FF), 0.5),
        "g_mix": gain((DEPTH, D_MODEL)),
        "w_in": nrm((DEPTH, D_MODEL, N_IN), D_MODEL ** -0.5),
        "g_out_a": gain((DEPTH, A_WIDTH)),
        "g_out_b": gain((DEPTH, B_WIDTH)),
        "g_out_c": gain((DEPTH, C_WIDTH)),
        "gmlp_gv": gain((DEPTH, B_WIDTH)),
        "gmlp_ws": nrm((DEPTH, B_GROUPS, CHUNK, CHUNK), CHUNK ** -0.5),
        "gmlp_bs": 1.0 + nrm((DEPTH, B_GROUPS, CHUNK), 0.02),
        "ssm_a_re": -0.5 + nrm((DEPTH, C_GROUPS, SSM_P), 0.01),
        "ssm_a_im": math.pi * n_idx + nrm((DEPTH, C_GROUPS, SSM_P), 0.01),
        "ssm_log_dt": jax.random.uniform(next(ks), (DEPTH, C_GROUPS), f32, math.log(1e-3), math.log(1e-1)),
        "ssm_b_re": nrm((DEPTH, C_GROUPS, SSM_P, C_GROUP_W), (2.0 * C_GROUP_W) ** -0.5),
        "ssm_b_im": nrm((DEPTH, C_GROUPS, SSM_P, C_GROUP_W), (2.0 * C_GROUP_W) ** -0.5),
        "ssm_c_re": nrm((DEPTH, C_GROUPS, C_GROUP_W, SSM_P), (2.0 * SSM_P) ** -0.5),
        "ssm_c_im": nrm((DEPTH, C_GROUPS, C_GROUP_W, SSM_P), (2.0 * SSM_P) ** -0.5),
        "ssm_d": nrm((DEPTH, C_WIDTH), 1.0),
        "ssm_w_glu": nrm((DEPTH, C_WIDTH, 2 * C_WIDTH), C_WIDTH ** -0.5),
        "w_out": nrm((DEPTH, MIX_WIDTH, D_MODEL), MIX_WIDTH ** -0.5),
        "g_ffn": gain((DEPTH, D_MODEL)),
        "w_up": nrm((DEPTH, D_MODEL, 2 * D_FF), D_MODEL ** -0.5),
        "conv_w": nrm((DEPTH, CONV_W, 2 * D_FF), CONV_W ** -0.5),
        "conv_b": nrm((DEPTH, 2 * D_FF), 0.01),
        "w_down": nrm((DEPTH, D_FF, D_MODEL), D_FF ** -0.5),
        "g_final": gain((D_MODEL,)),
    }


def reference(x_prompt, x_sample, cache_win_k, cache_win_v, state_ssm_re, state_ssm_im, state_ffn_conv,
              g_mix, w_in, g_out_a, g_out_b, g_out_c, gmlp_gv, gmlp_ws, gmlp_bs, ssm_a_re, ssm_a_im,
              ssm_log_dt, ssm_b_re, ssm_b_im, ssm_c_re, ssm_c_im, ssm_d, ssm_w_glu, w_out, g_ffn, w_up,
              conv_w, conv_b, w_down, g_final):
    y_p, y_s = x_prompt, x_sample
    p_states, s_states = [], []
    for l in range(DEPTH):
        lw = (g_mix[l], w_in[l], g_out_a[l], g_out_b[l], g_out_c[l], gmlp_gv[l], gmlp_ws[l], gmlp_bs[l],
              ssm_a_re[l], ssm_a_im[l], ssm_log_dt[l], ssm_b_re[l], ssm_b_im[l], ssm_c_re[l], ssm_c_im[l],
              ssm_d[l], ssm_w_glu[l], w_out[l], g_ffn[l], w_up[l], conv_w[l], conv_b[l], w_down[l])
        y_p, st_p = trunk_layer(y_p, None, None, None, *lw)
        y_s, st_s = trunk_layer(y_s, (cache_win_k[l], cache_win_v[l]), (state_ssm_re[l], state_ssm_im[l]),
                                state_ffn_conv[l], *lw)
        p_states.append(st_p)
        s_states.append(st_s)

    def stack(states, i):
        return jnp.stack([st[i] for st in states])

    y_p = rmsnorm(y_p, g_final)
    y_s = rmsnorm(y_s, g_final)
    return (y_p, y_s,
            stack(p_states, 0), stack(p_states, 1), stack(s_states, 0), stack(s_states, 1),
            stack(s_states, 2),
            stack(p_states, 3), stack(p_states, 4), stack(s_states, 3), stack(s_states, 4),
            stack(p_states, 5), stack(s_states, 5))
```

```python
import functools
import math

import numpy as np
import jax
import jax.numpy as jnp
from jax import lax
from jax.experimental import pallas as pl
from jax.experimental.pallas import tpu as pltpu

F32 = jnp.float32
BF16 = jnp.bfloat16

D_MODEL = 2048
A_WIDTH = 1024
A_HEADS = 8
HEAD_DIM = 128
A_DILATIONS = (1, 4, 16)
A_SPAN = 128
A_WINDOW_MAX = 2048
B_WIDTH = 512
B_GROUPS = 4
CHUNK = 128
C_WIDTH = 512
C_GROUPS = 32
C_GROUP_W = 16
SSM_P = 64
N_IN = 3 * A_WIDTH + 2 * B_WIDTH + C_WIDTH
D_FF = 5632
DEC_SEQ = 4
EPS = 1e-6
NEG = -1e30

LANE = 128
COL_Q, COL_K, COL_V = 0, A_WIDTH // LANE, 2 * A_WIDTH // LANE
COL_BU = 3 * A_WIDTH // LANE
COL_BV = COL_BU + B_WIDTH // LANE
COL_CU = COL_BV + B_WIDTH // LANE

SSM_LANE_GROUPS = 4
SSM_LG_STATES = C_GROUPS * SSM_P // SSM_LANE_GROUPS
SSM_R = 64
SSM_NC = 32

MIB = 1024 * 1024


def _params(sem, vmem_mib):
    return pltpu.CompilerParams(dimension_semantics=sem, vmem_limit_bytes=vmem_mib * MIB)


def _gelu(x):
    c = math.sqrt(2.0 / math.pi)
    return x * (0.5 * (1.0 + jnp.tanh(c * (x + 0.044715 * (x * x * x)))))


def _rms(x):
    return x * lax.rsqrt(jnp.mean(x * x, axis=-1, keepdims=True) + EPS)


_NT = (((1,), (1,)), ((), ()))


def _proj_body(x_ref, g_ref, w_ref, o_ref, h_ref):
    @pl.when(pl.program_id(1) == 0)
    def _():
        h_ref[...] = (_rms(x_ref[...]) * g_ref[...]).astype(BF16)

    o_ref[...] = jnp.dot(h_ref[...], w_ref[...], preferred_element_type=F32)


def _proj(x, g, w, tm, tn=512):
    t = x.shape[0]
    return pl.pallas_call(
        _proj_body,
        grid=(t // tm, N_IN // tn),
        in_specs=[
            pl.BlockSpec((tm, D_MODEL), lambda i, j: (i, 0)),
            pl.BlockSpec((1, D_MODEL), lambda i, j: (0, 0)),
            pl.BlockSpec((D_MODEL, tn), lambda i, j: (0, j)),
        ],
        out_specs=pl.BlockSpec((tm, tn), lambda i, j: (i, j)),
        out_shape=jax.ShapeDtypeStruct((t, N_IN), F32),
        scratch_shapes=[pltpu.VMEM((tm, D_MODEL), BF16)],
        compiler_params=_params(("arbitrary", "arbitrary"), 48),
        name="proj_in",
    )(x, g, w)


def _attn_prompt_body(q_ref, k_ref, v_ref, o_ref, acc_ref, m_ref, l_ref, *, seq):
    m_ref[...] = jnp.full(m_ref.shape, NEG, F32)
    l_ref[...] = jnp.zeros(l_ref.shape, F32)
    acc_ref[...] = jnp.zeros(acc_ref.shape, F32)
    scale = 1.0 / math.sqrt(HEAD_DIM)
    ii = lax.broadcasted_iota(jnp.int32, (A_SPAN, 2 * A_SPAN), 0)
    jj = lax.broadcasted_iota(jnp.int32, (A_SPAN, 2 * A_SPAN), 1)
    band = (jj >= ii) & (jj <= ii + A_SPAN)
    in_cur = jj >= A_SPAN

    for dil in A_DILATIONS:
        nb = seq // (A_SPAN * dil)

        def rows(start, dil=dil):
            if dil == 1:
                return pl.ds(pl.multiple_of(start, A_SPAN), A_SPAN)
            return pl.ds(start, A_SPAN, stride=dil)

        def block(n, r, dil=dil, rows=rows):
            base = r + n * (A_SPAN * dil)
            prev = r + jnp.maximum(n - 1, 0) * (A_SPAN * dil)
            q = (q_ref[rows(base), :] * scale).astype(BF16)
            k2 = jnp.concatenate([k_ref[rows(prev), :], k_ref[rows(base), :]], axis=0).astype(BF16)
            v2 = jnp.concatenate([v_ref[rows(prev), :], v_ref[rows(base), :]], axis=0).astype(BF16)
            s = lax.dot_general(q, k2, _NT, preferred_element_type=F32)
            s = jnp.where(band & (in_cur | (n > 0)), s, NEG)
            m_old = m_ref[rows(base), :]
            m_new = jnp.maximum(m_old, jnp.max(s, axis=1, keepdims=True))
            alpha = jnp.exp(m_old - m_new)
            p = jnp.exp(s - jnp.concatenate([m_new, m_new], axis=1))
            l_ref[rows(base), :] = alpha * l_ref[rows(base), :] + jnp.sum(p, axis=1, keepdims=True)
            acc_ref[rows(base), :] = alpha * acc_ref[rows(base), :] + jnp.dot(
                p.astype(BF16), v2, preferred_element_type=F32)
            m_ref[rows(base), :] = m_new

        def residue(r, carry, nb=nb, block=block):
            def inner(n, c):
                block(n, r)
                return c
            return lax.fori_loop(0, nb, inner, carry)

        lax.fori_loop(0, dil, residue, 0)

    o_ref[...] = acc_ref[...] / l_ref[...]


def _attn_prompt(p):
    seq = p.shape[0]
    blk = (seq, HEAD_DIM)
    return pl.pallas_call(
        functools.partial(_attn_prompt_body, seq=seq),
        grid=(A_HEADS,),
        in_specs=[
            pl.BlockSpec(blk, lambda h: (0, COL_Q + h)),
            pl.BlockSpec(blk, lambda h: (0, COL_K + h)),
            pl.BlockSpec(blk, lambda h: (0, COL_V + h)),
        ],
        out_specs=pl.BlockSpec(blk, lambda h: (0, h)),
        out_shape=jax.ShapeDtypeStruct((seq, A_WIDTH), F32),
        scratch_shapes=[pltpu.VMEM(blk, F32)] * 3,
        compiler_params=_params(("arbitrary",), 56),
        name="attn_prompt",
    )(p, p, p)


def _sample_attn_masks():
    j = np.arange(DEC_SEQ)[:, None, None, None]
    h = np.arange(A_HEADS)[None, :, None, None]
    hk = np.arange(A_HEADS)[None, None, None, :]
    same = (h == hk)
    i = np.arange(512)[None, None, :, None]
    cnt_l = same * ((i >= 384 + j).astype(np.int32) + ((i - j) % 4 == 0).astype(np.int32))
    jn = np.arange(DEC_SEQ)[None, None, :, None]
    cnt_n = same * ((jn <= j).astype(np.int32) + 2 * (jn == j).astype(np.int32))
    g = np.arange(128)[None, None, :, None]
    head_match = np.broadcast_to(same[0:1], (1, A_HEADS, 128, A_HEADS))
    return (cnt_l.reshape(DEC_SEQ * A_HEADS, 512 * A_HEADS).astype(np.float32),
            cnt_n.reshape(DEC_SEQ * A_HEADS, DEC_SEQ * A_HEADS).astype(np.float32),
            (head_match + 0 * g).reshape(A_HEADS, 128 * A_HEADS).astype(np.float32))


def _attn_sample_body(q_ref, k_ref, v_ref, ka0, ka1, ka2, ka3, kl_ref, va0, va1, va2, va3, vl_ref,
                      cl_ref, cn_ref, hm_ref, o_ref):
    nh = A_HEADS
    q = q_ref[...] * (1.0 / math.sqrt(HEAD_DIM))
    qs = q.astype(BF16)
    cl, cn, hm = cl_ref[...], cn_ref[...], hm_ref[...]

    kl = kl_ref[...].reshape(512 * nh, HEAD_DIM).astype(BF16)
    s_l = lax.dot_general(qs, kl, _NT, preferred_element_type=F32)
    s_n = lax.dot_general(qs, k_ref[...].astype(BF16), _NT, preferred_element_type=F32)
    s_a = []
    for res, ka in enumerate((ka0, ka1, ka2, ka3)):
        kk = ka[...].reshape(128 * nh, HEAD_DIM).astype(BF16)
        s_a.append(lax.dot_general(q[res * nh:(res + 1) * nh].astype(BF16), kk, _NT,
                                   preferred_element_type=F32))
    m = jnp.max(jnp.where(cl > 0, s_l, NEG), axis=1, keepdims=True)
    m = jnp.maximum(m, jnp.max(jnp.where(cn > 0, s_n, NEG), axis=1, keepdims=True))
    m_a = jnp.concatenate([jnp.max(jnp.where(hm > 0, s, NEG), axis=1, keepdims=True) for s in s_a], axis=0)
    m = jnp.maximum(m, m_a)

    p_l = cl * jnp.exp(jnp.minimum(s_l - m, 0.0))
    p_n = cn * jnp.exp(jnp.minimum(s_n - m, 0.0))
    den = jnp.sum(p_l, axis=1, keepdims=True) + jnp.sum(p_n, axis=1, keepdims=True)
    vl = vl_ref[...].reshape(512 * nh, HEAD_DIM).astype(BF16)
    out = jnp.dot(p_l.astype(BF16), vl, preferred_element_type=F32)
    out = out + jnp.dot(p_n.astype(BF16), v_ref[...].astype(BF16), preferred_element_type=F32)
    o_a, d_a = [], []
    for res, va in enumerate((va0, va1, va2, va3)):
        p_a = hm * jnp.exp(jnp.minimum(s_a[res] - m[res * nh:(res + 1) * nh], 0.0))
        d_a.append(jnp.sum(p_a, axis=1, keepdims=True))
        vv = va[...].reshape(128 * nh, HEAD_DIM).astype(BF16)
        o_a.append(jnp.dot(p_a.astype(BF16), vv, preferred_element_type=F32))
    out = out + jnp.concatenate(o_a, axis=0)
    den = den + jnp.concatenate(d_a, axis=0)
    o_ref[...] = out / den


def _attn_sample(q3, k3, v3, cache_k, cache_v, layer):
    db = q3.shape[0]
    depth, _, w_buf, nh, hd = cache_k.shape
    assert (w_buf, nh, hd) == (A_WINDOW_MAX, A_HEADS, HEAD_DIM) and q3.shape[1] == DEC_SEQ * A_HEADS
    cnt_l, cnt_n, head_match = _sample_attn_masks()
    rows = DEC_SEQ * A_HEADS

    def strided_specs():
        return [pl.BlockSpec((None, None, 128, None, nh, hd), functools.partial(
            lambda b, res: (layer, b, 0, res, 0, 0), res=res)) for res in range(DEC_SEQ)]

    last_spec = pl.BlockSpec((None, None, None, 512, nh, hd), lambda b: (layer, b, 3, 0, 0, 0))
    tok_spec = pl.BlockSpec((None, rows, hd), lambda b: (b, 0, 0))

    def whole(a):
        return pl.BlockSpec(a.shape, lambda b: (0, 0))

    k16 = cache_k.reshape(depth, db, 128, 16, nh, hd)
    v16 = cache_v.reshape(depth, db, 128, 16, nh, hd)
    k512 = cache_k.reshape(depth, db, 4, 512, nh, hd)
    v512 = cache_v.reshape(depth, db, 4, 512, nh, hd)
    return pl.pallas_call(
        _attn_sample_body,
        grid=(db,),
        in_specs=[tok_spec, tok_spec, tok_spec] + strided_specs() + [last_spec] + strided_specs() + [last_spec]
        + [whole(cnt_l), whole(cnt_n), whole(head_match)],
        out_specs=tok_spec,
        out_shape=jax.ShapeDtypeStruct((db, rows, hd), F32),
        compiler_params=_params(("arbitrary",), 48),
        name="attn_sample",
    )(q3, k3, v3, k16, k16, k16, k16, k512, v16, v16, v16, v16, v512,
      jnp.asarray(cnt_l), jnp.asarray(cnt_n), jnp.asarray(head_match))


def _gmlp_prompt_body(bu_ref, bv_ref, gg_ref, ws_ref, bs_ref, o_ref, *, chunks):
    ri = lax.broadcasted_iota(jnp.int32, (CHUNK, CHUNK), 0)
    ci = lax.broadcasted_iota(jnp.int32, (CHUNK, CHUNK), 1)
    wm = jnp.where(ci <= ri, ws_ref[0], 0.0).astype(BF16)
    bias = bs_ref[0]
    gain = gg_ref[...]
    for c in range(chunks):
        sl = slice(c * CHUNK, (c + 1) * CHUNK)
        gv = _rms(_gelu(bv_ref[sl, :])) * gain
        mix = jnp.dot(wm, gv.astype(BF16), preferred_element_type=F32) + bias
        o_ref[sl, :] = _gelu(bu_ref[sl, :]) * mix


def _gmlp_prompt(p, gain, ws, bs, chunks=8):
    seq = p.shape[0]
    tm = chunks * CHUNK
    return pl.pallas_call(
        functools.partial(_gmlp_prompt_body, chunks=chunks),
        grid=(seq // tm, B_GROUPS),
        in_specs=[
            pl.BlockSpec((tm, LANE), lambda i, g: (i, COL_BU + g)),
            pl.BlockSpec((tm, LANE), lambda i, g: (i, COL_BV + g)),
            pl.BlockSpec((1, LANE), lambda i, g: (0, g)),
            pl.BlockSpec((1, CHUNK, CHUNK), lambda i, g: (g, 0, 0)),
            pl.BlockSpec((1, CHUNK, 1), lambda i, g: (g, 0, 0)),
        ],
        out_specs=pl.BlockSpec((tm, LANE), lambda i, g: (i, g)),
        out_shape=jax.ShapeDtypeStruct((seq, B_WIDTH), F32),
        compiler_params=_params(("arbitrary", "arbitrary"), 32),
        name="gmlp_prompt",
    )(p, p, gain, ws, bs.reshape(B_GROUPS, CHUNK, 1))


def _gmlp_sample_body(ws_ref, bs_ref, bu_ref, bv_ref, gg_ref, o_ref, gv_ref, *, db):
    g = pl.program_id(0)
    gv = _rms(_gelu(bv_ref[...])) * gg_ref[...]
    gv_ref[...] = gv
    gu = _gelu(bu_ref[...])
    for i in range(DEC_SEQ):
        mix = jnp.full((db, LANE), bs_ref[g, i], F32)
        for j in range(i + 1):
            mix = mix + ws_ref[g, i * DEC_SEQ + j] * gv[j * db:(j + 1) * db]
        o_ref[i * db:(i + 1) * db, :] = gu[i * db:(i + 1) * db] * mix


def _gmlp_sample(p, gain, ws, bs):
    t = p.shape[0]
    db = t // DEC_SEQ
    ws4 = ws[:, :DEC_SEQ, :DEC_SEQ].reshape(B_GROUPS, DEC_SEQ * DEC_SEQ)
    bs4 = bs[:, :DEC_SEQ]
    smem = pl.BlockSpec(memory_space=pltpu.SMEM)
    return pl.pallas_call(
        functools.partial(_gmlp_sample_body, db=db),
        grid=(B_GROUPS,),
        in_specs=[
            smem, smem,
            pl.BlockSpec((t, LANE), lambda g: (0, COL_BU + g)),
            pl.BlockSpec((t, LANE), lambda g: (0, COL_BV + g)),
            pl.BlockSpec((1, LANE), lambda g: (0, g)),
        ],
        out_specs=[pl.BlockSpec((t, LANE), lambda g: (0, g))] * 2,
        out_shape=[jax.ShapeDtypeStruct((t, B_WIDTH), F32)] * 2,
        compiler_params=_params(("arbitrary",), 32),
        name="gmlp_sample",
    )(ws4, bs4, p, p, gain)


def _ssm_prep_body(lre_ref, lim_ref, ldt_ref, bre_ref, bim_ref, pwre_ref, pwim_ref, bbre_ref, bbim_ref):
    lre, lim = lre_ref[...], lim_ref[...]
    dt = jnp.exp(ldt_ref[...])
    k = lax.broadcasted_iota(jnp.int32, pwre_ref.shape, 0).astype(F32) + 1.0
    mag = jnp.exp(k * (dt * lre))
    ang = k * (dt * lim)
    pwre = mag * jnp.cos(ang)
    pwim = mag * jnp.sin(ang)
    pwre_ref[...] = pwre
    pwim_ref[...] = pwim
    xr, xi = pwre[0:1] - 1.0, pwim[0:1]
    den = lre * lre + lim * lim
    cr = (xr * lre + xi * lim) / den
    ci = (xi * lre - xr * lim) / den
    bre, bim = bre_ref[...], bim_ref[...]
    bbre_ref[...] = cr * bre - ci * bim
    bbim_ref[...] = cr * bim + ci * bre


def _ssm_prep(a_re, a_im, log_dt, b_re, b_im):
    n = C_GROUPS * SSM_P
    row = lambda a: a.reshape(1, n)
    bt = lambda b: jnp.transpose(b, (2, 0, 1)).reshape(C_GROUP_W, n)
    shapes = [jax.ShapeDtypeStruct((SSM_R, n), F32)] * 2 + [jax.ShapeDtypeStruct((C_GROUP_W, n), F32)] * 2
    pwre, pwim, bbre, bbim = pl.pallas_call(
        _ssm_prep_body, out_shape=shapes, name="ssm_prep",
    )(row(a_re), row(a_im), row(jnp.repeat(log_dt, SSM_P)), bt(b_re), bt(b_im))
    lg, st = SSM_LANE_GROUPS, SSM_LG_STATES
    split = lambda t: jnp.transpose(t.reshape(t.shape[0], lg, st), (1, 0, 2))
    pw = jnp.concatenate([split(pwre), split(pwim)], axis=-1)
    eye = jnp.eye(C_GROUPS // lg, dtype=F32)

    def blockdiag(bb):
        r = bb.reshape(C_GROUP_W, lg, C_GROUPS // lg, SSM_P)
        return jnp.einsum('ab,mlbp->lambp', eye, r).reshape(lg, LANE, st)

    bdb = jnp.concatenate([blockdiag(bbre), blockdiag(bbim)], axis=-1).astype(BF16)
    return pw, bdb


def _ssm_out_matrix(c):
    lg = SSM_LANE_GROUPS
    eye = jnp.eye(C_GROUPS // lg, dtype=F32)
    r = c.reshape(lg, C_GROUPS // lg, C_GROUP_W, SSM_P)
    return jnp.einsum('ab,lanp->lapbn', eye, r).reshape(lg, SSM_LG_STATES, LANE).astype(BF16)


def _cmul_add(are, aim, hre, him, xre, xim):
    return are * hre - aim * him + xre, are * him + aim * hre + xim


def _s5_prompt_body(u_ref, bdb_ref, cre_ref, cim_ref, d_ref, pw_ref, yg_ref, hl_ref,
                    u3_ref, h_ref, car_ref, hc_ref):
    r_steps, nc, st = SSM_R, SSM_NC, SSM_LG_STATES
    re, im = slice(0, st), slice(st, 2 * st)

    @pl.when(pl.program_id(1) == 0)
    def _():
        hc_ref[...] = jnp.zeros(hc_ref.shape, F32)

    def rows(r):
        return pl.ds(pl.multiple_of(r * nc, nc), nc)

    def regroup(r, c):
        u3_ref[rows(r), :] = u_ref[pl.ds(r, nc, stride=r_steps), :]
        return c

    lax.fori_loop(0, r_steps, regroup, 0)
    h_ref[...] = jnp.dot(u3_ref[...].astype(BF16), bdb_ref[0], preferred_element_type=F32)

    half = st // 2
    for hf in range(2):
        cre_ = slice(hf * half, (hf + 1) * half)
        cim_ = slice(st + hf * half, st + (hf + 1) * half)
        are = jnp.broadcast_to(pw_ref[0, 0:1, cre_], (nc, half))
        aim = jnp.broadcast_to(pw_ref[0, 0:1, cim_], (nc, half))

        def step(r, carry, cre_=cre_, cim_=cim_, are=are, aim=aim):
            hre, him = _cmul_add(are, aim, carry[0], carry[1], h_ref[rows(r), cre_], h_ref[rows(r), cim_])
            h_ref[rows(r), cre_] = hre
            h_ref[rows(r), cim_] = him
            return hre, him

        lax.fori_loop(1, r_steps, step, (h_ref[0:nc, cre_], h_ref[0:nc, cim_]))

    ends = h_ref[(r_steps - 1) * nc:r_steps * nc, :]
    are, aim = pw_ref[0, r_steps - 1:r_steps, re], pw_ref[0, r_steps - 1:r_steps, im]
    cre, cim = hc_ref[0:1, re], hc_ref[0:1, im]
    for c in range(nc):
        car_ref[c:c + 1, re] = cre
        car_ref[c:c + 1, im] = cim
        cre, cim = _cmul_add(are, aim, cre, cim, ends[c:c + 1, re], ends[c:c + 1, im])
    hc_ref[:, re] = jnp.broadcast_to(cre, (8, st))
    hc_ref[:, im] = jnp.broadcast_to(cim, (8, st))
    hl_ref[0] = hc_ref[...]

    def fix(r, c):
        pre, pim = pw_ref[0, pl.ds(r, 1), re], pw_ref[0, pl.ds(r, 1), im]
        hre, him = _cmul_add(pre, pim, car_ref[:, re], car_ref[:, im], h_ref[rows(r), re], h_ref[rows(r), im])
        h_ref[rows(r), re] = hre
        h_ref[rows(r), im] = him
        return c

    lax.fori_loop(0, r_steps, fix, 0)

    y = (jnp.dot(h_ref[:, re].astype(BF16), cre_ref[0], preferred_element_type=F32)
         - jnp.dot(h_ref[:, im].astype(BF16), cim_ref[0], preferred_element_type=F32)
         + d_ref[...] * u3_ref[...])
    u3_ref[...] = _gelu(y)

    def ungroup(r, c):
        yg_ref[pl.ds(r, nc, stride=r_steps), :] = u3_ref[rows(r), :]
        return c

    lax.fori_loop(0, r_steps, ungroup, 0)


def _s5_prompt(p, bdb, cre, cim, d, pw):
    seq = p.shape[0]
    tseg = SSM_R * SSM_NC
    st2 = 2 * SSM_LG_STATES
    lgs = SSM_LANE_GROUPS
    return pl.pallas_call(
        _s5_prompt_body,
        grid=(lgs, seq // tseg),
        in_specs=[
            pl.BlockSpec((tseg, LANE), lambda g, t: (t, COL_CU + g)),
            pl.BlockSpec((1, LANE, st2), lambda g, t: (g, 0, 0)),
            pl.BlockSpec((1, SSM_LG_STATES, LANE), lambda g, t: (g, 0, 0)),
            pl.BlockSpec((1, SSM_LG_STATES, LANE), lambda g, t: (g, 0, 0)),
            pl.BlockSpec((1, LANE), lambda g, t: (0, g)),
            pl.BlockSpec((1, SSM_R, st2), lambda g, t: (g, 0, 0)),
        ],
        out_specs=[
            pl.BlockSpec((tseg, LANE), lambda g, t: (t, g)),
            pl.BlockSpec((1, 8, st2), lambda g, t: (g, 0, 0)),
        ],
        out_shape=[jax.ShapeDtypeStruct((seq, C_WIDTH), F32), jax.ShapeDtypeStruct((lgs, 8, st2), F32)],
        scratch_shapes=[
            pltpu.VMEM((tseg, LANE), F32),
            pltpu.VMEM((tseg, st2), F32),
            pltpu.VMEM((SSM_NC, st2), F32),
            pltpu.VMEM((8, st2), F32),
        ],
        compiler_params=_params(("arbitrary", "arbitrary"), 40),
        name="s5_prompt",
    )(p, bdb, cre, cim, d, pw)


def _s5_sample_body(u_ref, hre_ref, him_ref, bdb_ref, cre_ref, cim_ref, d_ref, pw_ref,
                    yg_ref, ore_ref, oim_ref, h_ref, *, db):
    st = SSM_LG_STATES
    re, im = slice(0, st), slice(st, 2 * st)
    u = u_ref[...]
    x = jnp.dot(u.astype(BF16), bdb_ref[0], preferred_element_type=F32)
    are, aim = pw_ref[0, 0:1, re], pw_ref[0, 0:1, im]
    hre, him = hre_ref[...], him_ref[...]
    for j in range(DEC_SEQ):
        rj = slice(j * db, (j + 1) * db)
        hre, him = _cmul_add(are, aim, hre, him, x[rj, re], x[rj, im])
        h_ref[rj, re] = hre
        h_ref[rj, im] = him
    ore_ref[...] = hre
    oim_ref[...] = him
    y = (jnp.dot(h_ref[:, re].astype(BF16), cre_ref[0], preferred_element_type=F32)
         - jnp.dot(h_ref[:, im].astype(BF16), cim_ref[0], preferred_element_type=F32)
         + d_ref[...] * u)
    yg_ref[...] = _gelu(y)


def _s5_sample(p, h0re, h0im, bdb, cre, cim, d, pw):
    t = p.shape[0]
    db = t // DEC_SEQ
    st = SSM_LG_STATES
    lgs = SSM_LANE_GROUPS
    return pl.pallas_call(
        functools.partial(_s5_sample_body, db=db),
        grid=(lgs,),
        in_specs=[
            pl.BlockSpec((t, LANE), lambda g: (0, COL_CU + g)),
            pl.BlockSpec((db, st), lambda g: (0, g)),
            pl.BlockSpec((db, st), lambda g: (0, g)),
            pl.BlockSpec((1, LANE, 2 * st), lambda g: (g, 0, 0)),
            pl.BlockSpec((1, st, LANE), lambda g: (g, 0, 0)),
            pl.BlockSpec((1, st, LANE), lambda g: (g, 0, 0)),
            pl.BlockSpec((1, LANE), lambda g: (0, g)),
            pl.BlockSpec((1, SSM_R, 2 * st), lambda g: (g, 0, 0)),
        ],
        out_specs=[
            pl.BlockSpec((t, LANE), lambda g: (0, g)),
            pl.BlockSpec((db, st), lambda g: (0, g)),
            pl.BlockSpec((db, st), lambda g: (0, g)),
        ],
        out_shape=[jax.ShapeDtypeStruct((t, C_WIDTH), F32),
                   jax.ShapeDtypeStruct((db, lgs * st), F32), jax.ShapeDtypeStruct((db, lgs * st), F32)],
        scratch_shapes=[pltpu.VMEM((t, 2 * st), F32)],
        compiler_params=_params(("arbitrary",), 32),
        name="s5_sample",
    )(p, h0re, h0im, bdb, cre, cim, d, pw)


def _mix_out_body(a_ref, b_ref, y_ref, x_ref, wg_ref, ga_ref, gb_ref, gc_ref, wo_ref, o_ref):
    glu = jnp.dot(y_ref[...].astype(BF16), wg_ref[...], preferred_element_type=F32)
    c = glu[:, :C_WIDTH] * jax.nn.sigmoid(glu[:, C_WIDTH:])
    mixed = jnp.concatenate([
        (_rms(a_ref[...]) * ga_ref[...]).astype(BF16),
        (_rms(b_ref[...]) * gb_ref[...]).astype(BF16),
        (_rms(c) * gc_ref[...]).astype(BF16)], axis=1)
    o_ref[...] = x_ref[...] + jnp.dot(mixed, wo_ref[...], preferred_element_type=F32)


def _mix_out(a, b, y, x, w_glu, g_a, g_b, g_c, w_out, tm=512):
    t = x.shape[0]
    row = lambda w: pl.BlockSpec((tm, w), lambda i: (i, 0))
    whole = lambda arr: pl.BlockSpec(arr.shape, lambda i: (0, 0))
    return pl.pallas_call(
        _mix_out_body,
        grid=(t // tm,),
        in_specs=[row(A_WIDTH), row(B_WIDTH), row(C_WIDTH), row(D_MODEL),
                  whole(w_glu), whole(g_a), whole(g_b), whole(g_c), whole(w_out)],
        out_specs=row(D_MODEL),
        out_shape=jax.ShapeDtypeStruct((t, D_MODEL), F32),
        compiler_params=_params(("arbitrary",), 56),
        name="mix_out",
    )(a, b, y, x, w_glu, g_a, g_b, g_c, w_out)


FFN_HALO = 16


def _conv_gate(ug, uv, cwg_ref, cwv_ref, cbg_ref, cbv_ref, taps):
    def conv(u, cw_ref, cb_ref):
        return ((cb_ref[...] + cw_ref[0:1, :] * taps[0](u)) + cw_ref[1:2, :] * taps[1](u)) + cw_ref[2:3, :] * taps[2](u)
    gate = conv(ug, cwg_ref, cbg_ref)
    return (gate * jax.nn.sigmoid(gate)) * conv(uv, cwv_ref, cbv_ref)


def _ffn_prompt_body(x_ref, halo_ref, g_ref, wg_ref, wv_ref, cwg_ref, cwv_ref, cbg_ref, cbv_ref, wd_ref,
                     o_ref, tail_ref, h_ref, *, tm):
    i, f = pl.program_id(0), pl.program_id(1)
    hl = FFN_HALO

    @pl.when(f == 0)
    def _():
        h_ref[hl:, :] = (_rms(x_ref[...]) * g_ref[...]).astype(BF16)
        prev = _rms(halo_ref[...]) * g_ref[...]
        h_ref[0:hl, :] = jnp.where(i > 0, prev, 0.0).astype(BF16)

    h = h_ref[...]
    ug = jnp.dot(h, wg_ref[...], preferred_element_type=F32)
    uv = jnp.dot(h, wv_ref[...], preferred_element_type=F32)
    taps = (lambda u: u[hl - 2:hl - 2 + tm], lambda u: u[hl - 1:hl - 1 + tm], lambda u: u[hl:])
    act = _conv_gate(ug, uv, cwg_ref, cwv_ref, cbg_ref, cbv_ref, taps)
    down = jnp.dot(act.astype(BF16), wd_ref[...], preferred_element_type=F32)

    @pl.when(f == 0)
    def _():
        o_ref[...] = x_ref[...] + down

    @pl.when(f > 0)
    def _():
        o_ref[...] += down

    tail_ref[0] = ug[hl + tm - 8:]
    tail_ref[1] = uv[hl + tm - 8:]


def _ffn_prompt(x, g, w_up, conv_w, conv_b, w_down, tm=512, tf=512):
    t = x.shape[0]
    nf = D_FF // tf
    hl = FFN_HALO
    cb = conv_b.reshape(1, 2 * D_FF)
    return pl.pallas_call(
        functools.partial(_ffn_prompt_body, tm=tm),
        grid=(t // tm, nf),
        in_specs=[
            pl.BlockSpec((tm, D_MODEL), lambda i, f: (i, 0)),
            pl.BlockSpec((hl, D_MODEL), lambda i, f: (jnp.maximum(i * (tm // hl) - 1, 0), 0)),
            pl.BlockSpec((1, D_MODEL), lambda i, f: (0, 0)),
            pl.BlockSpec((D_MODEL, tf), lambda i, f: (0, f)),
            pl.BlockSpec((D_MODEL, tf), lambda i, f: (0, nf + f)),
            pl.BlockSpec((3, tf), lambda i, f: (0, f)),
            pl.BlockSpec((3, tf), lambda i, f: (0, nf + f)),
            pl.BlockSpec((1, tf), lambda i, f: (0, f)),
            pl.BlockSpec((1, tf), lambda i, f: (0, nf + f)),
            pl.BlockSpec((tf, D_MODEL), lambda i, f: (f, 0)),
        ],
        out_specs=[
            pl.BlockSpec((tm, D_MODEL), lambda i, f: (i, 0)),
            pl.BlockSpec((None, 2, 8, tf), lambda i, f: (i, 0, 0, f)),
        ],
        out_shape=[jax.ShapeDtypeStruct((t, D_MODEL), F32), jax.ShapeDtypeStruct((t // tm, 2, 8, D_FF), F32)],
        scratch_shapes=[pltpu.VMEM((hl + tm, D_MODEL), BF16)],
        compiler_params=_params(("arbitrary", "arbitrary"), 56),
        name="ffn_prompt",
    )(x, x, g, w_up, w_up, conv_w, conv_w, cb, cb, w_down)


def _ffn_sample_body(x_ref, g_ref, sg_ref, sv_ref, wg_ref, wv_ref, cwg_ref, cwv_ref, cbg_ref, cbv_ref, wd_ref,
                     o_ref, tail_ref, h_ref, *, db):
    f = pl.program_id(0)
    t = DEC_SEQ * db

    @pl.when(f == 0)
    def _():
        h_ref[...] = (_rms(x_ref[...]) * g_ref[...]).astype(BF16)

    h = h_ref[...]
    ug = jnp.dot(h, wg_ref[...], preferred_element_type=F32)
    uv = jnp.dot(h, wv_ref[...], preferred_element_type=F32)
    pg = jnp.concatenate([sg_ref[0], sg_ref[1], ug], axis=0)
    pv = jnp.concatenate([sv_ref[0], sv_ref[1], uv], axis=0)
    taps = (lambda u: u[0:t], lambda u: u[db:db + t], lambda u: u[2 * db:2 * db + t])
    act = _conv_gate(pg, pv, cwg_ref, cwv_ref, cbg_ref, cbv_ref, taps)
    down = jnp.dot(act.astype(BF16), wd_ref[...], preferred_element_type=F32)

    @pl.when(f == 0)
    def _():
        o_ref[...] = x_ref[...] + down

    @pl.when(f > 0)
    def _():
        o_ref[...] += down

    tail_ref[0, 0] = ug[(DEC_SEQ - 2) * db:(DEC_SEQ - 1) * db]
    tail_ref[0, 1] = ug[(DEC_SEQ - 1) * db:]
    tail_ref[1, 0] = uv[(DEC_SEQ - 2) * db:(DEC_SEQ - 1) * db]
    tail_ref[1, 1] = uv[(DEC_SEQ - 1) * db:]


def _ffn_sample(x, g, state, w_up, conv_w, conv_b, w_down, tf=512):
    t = x.shape[0]
    db = t // DEC_SEQ
    nf = D_FF // tf
    cb = conv_b.reshape(1, 2 * D_FF)
    return pl.pallas_call(
        functools.partial(_ffn_sample_body, db=db),
        grid=(nf,),
        in_specs=[
            pl.BlockSpec((t, D_MODEL), lambda f: (0, 0)),
            pl.BlockSpec((1, D_MODEL), lambda f: (0, 0)),
            pl.BlockSpec((2, db, tf), lambda f: (0, 0, f)),
            pl.BlockSpec((2, db, tf), lambda f: (0, 0, nf + f)),
            pl.BlockSpec((D_MODEL, tf), lambda f: (0, f)),
            pl.BlockSpec((D_MODEL, tf), lambda f: (0, nf + f)),
            pl.BlockSpec((3, tf), lambda f: (0, f)),
            pl.BlockSpec((3, tf), lambda f: (0, nf + f)),
            pl.BlockSpec((1, tf), lambda f: (0, f)),
            pl.BlockSpec((1, tf), lambda f: (0, nf + f)),
            pl.BlockSpec((tf, D_MODEL), lambda f: (f, 0)),
        ],
        out_specs=[
            pl.BlockSpec((t, D_MODEL), lambda f: (0, 0)),
            pl.BlockSpec((2, 2, db, tf), lambda f: (0, 0, 0, f)),
        ],
        out_shape=[jax.ShapeDtypeStruct((t, D_MODEL), F32), jax.ShapeDtypeStruct((2, 2, db, D_FF), F32)],
        scratch_shapes=[pltpu.VMEM((t, D_MODEL), BF16)],
        compiler_params=_params(("arbitrary",), 48),
        name="ffn_sample",
    )(x, g, state, state, w_up, w_up, conv_w, conv_w, cb, cb, w_down)


def _norm_body(x_ref, g_ref, o_ref):
    o_ref[...] = _rms(x_ref[...]) * g_ref[...]


def _final_norm(x, g, tm=512):
    t = x.shape[0]
    tm = min(tm, t)
    return pl.pallas_call(
        _norm_body,
        grid=(t // tm,),
        in_specs=[pl.BlockSpec((tm, D_MODEL), lambda i: (i, 0)), pl.BlockSpec((1, D_MODEL), lambda i: (0, 0))],
        out_specs=pl.BlockSpec((tm, D_MODEL), lambda i: (i, 0)),
        out_shape=jax.ShapeDtypeStruct((t, D_MODEL), F32),
        compiler_params=_params(("arbitrary",), 32),
        name="final_norm",
    )(x, g)


def kernel(x_prompt, x_sample, cache_win_k, cache_win_v, state_ssm_re, state_ssm_im, state_ffn_conv,
           g_mix, w_in, g_out_a, g_out_b, g_out_c, gmlp_gv, gmlp_ws, gmlp_bs, ssm_a_re, ssm_a_im,
           ssm_log_dt, ssm_b_re, ssm_b_im, ssm_c_re, ssm_c_im, ssm_d, ssm_w_glu, w_out, g_ffn, w_up,
           conv_w, conv_b, w_down, g_final):
    depth = w_in.shape[0]
    bsz, seq, _ = x_prompt.shape
    db, ds, _ = x_sample.shape
    assert bsz == 1 and ds == DEC_SEQ and seq % (SSM_R * SSM_NC) == 0 and seq >= A_WINDOW_MAX
    n_states = C_GROUPS * SSM_P
    row = lambda v: v.reshape(1, -1)

    w_in_b, w_glu_b, w_out_b = w_in.astype(BF16), ssm_w_glu.astype(BF16), w_out.astype(BF16)
    w_up_b, w_down_b = w_up.astype(BF16), w_down.astype(BF16)

    xp = x_prompt.reshape(seq, D_MODEL)
    xs = jnp.transpose(x_sample, (1, 0, 2)).reshape(ds * db, D_MODEL)
    outs = {k: [] for k in ("kp", "vp", "ks", "vs", "gv", "rp", "ip", "rs", "is", "cp", "cs")}

    for l in range(depth):
        pw, bdb = _ssm_prep(ssm_a_re[l], ssm_a_im[l], ssm_log_dt[l], ssm_b_re[l], ssm_b_im[l])
        c_re, c_im = _ssm_out_matrix(ssm_c_re[l]), _ssm_out_matrix(ssm_c_im[l])
        d_row = row(ssm_d[l])
        gains = (row(g_out_a[l]), row(g_out_b[l]), row(g_out_c[l]))

        p = _proj(xp, row(g_mix[l]), w_in_b[l], tm=min(1024, seq))
        a = _attn_prompt(p)
        b = _gmlp_prompt(p, row(gmlp_gv[l]), gmlp_ws[l], gmlp_bs[l])
        yg, h_last = _s5_prompt(p, bdb, c_re, c_im, d_row, pw)
        xm = _mix_out(a, b, yg, xp, w_glu_b[l], *gains, w_out_b[l])
        xp, tail = _ffn_prompt(xm, row(g_ffn[l]), w_up_b[l], conv_w[l], conv_b[l], w_down_b[l])
        keep = min(A_WINDOW_MAX, seq)
        outs["kp"].append(p[seq - keep:, A_WIDTH:2 * A_WIDTH].reshape(1, keep, A_HEADS, HEAD_DIM))
        outs["vp"].append(p[seq - keep:, 2 * A_WIDTH:3 * A_WIDTH].reshape(1, keep, A_HEADS, HEAD_DIM))
        st = SSM_LG_STATES
        outs["rp"].append(h_last[:, 0, :st].reshape(1, C_GROUPS, SSM_P))
        outs["ip"].append(h_last[:, 0, st:].reshape(1, C_GROUPS, SSM_P))
        outs["cp"].append(jnp.transpose(tail[-1, :, 6:8, :], (1, 0, 2)).reshape(1, 2, 2 * D_FF))

        ps = _proj(xs, row(g_mix[l]), w_in_b[l], tm=ds * db)

        def heads(cols):
            return jnp.transpose(cols.reshape(ds, db, A_HEADS, HEAD_DIM), (1, 0, 2, 3)).reshape(
                db, ds * A_HEADS, HEAD_DIM)

        q3, k3, v3 = (heads(ps[:, i * A_WIDTH:(i + 1) * A_WIDTH]) for i in range(3))
        a3 = _attn_sample(q3, k3, v3, cache_win_k, cache_win_v, l)
        a_s = jnp.transpose(a3.reshape(db, ds, A_WIDTH), (1, 0, 2)).reshape(ds * db, A_WIDTH)
        b_s, gv_s = _gmlp_sample(ps, row(gmlp_gv[l]), gmlp_ws[l], gmlp_bs[l])
        yg_s, hre, him = _s5_sample(ps, state_ssm_re[l].reshape(db, n_states), state_ssm_im[l].reshape(db, n_states),
                                    bdb, c_re, c_im, d_row, pw)
        xm_s = _mix_out(a_s, b_s, yg_s, xs, w_glu_b[l], *gains, w_out_b[l], tm=ds * db)
        xs, tail_s = _ffn_sample(xm_s, row(g_ffn[l]), jnp.transpose(state_ffn_conv[l], (1, 0, 2)),
                                 w_up_b[l], conv_w[l], conv_b[l], w_down_b[l])
        outs["ks"].append(k3.reshape(db, ds, A_HEADS, HEAD_DIM))
        outs["vs"].append(v3.reshape(db, ds, A_HEADS, HEAD_DIM))
        outs["gv"].append(jnp.transpose(gv_s.reshape(ds, db, B_WIDTH), (1, 0, 2)))
        outs["rs"].append(hre.reshape(db, C_GROUPS, SSM_P))
        outs["is"].append(him.reshape(db, C_GROUPS, SSM_P))
        outs["cs"].append(jnp.transpose(tail_s, (2, 1, 0, 3)).reshape(db, 2, 2 * D_FF))

    y_p = _final_norm(xp, row(g_final)).reshape(1, seq, D_MODEL)
    y_s = jnp.transpose(_final_norm(xs, row(g_final)).reshape(ds, db, D_MODEL), (1, 0, 2))
    st = lambda k: jnp.stack(outs[k])
    return (y_p, y_s, st("kp"), st("vp"), st("ks"), st("vs"), st("gv"),
            st("rp"), st("ip"), st("rs"), st("is"), st("cp"), st("cs"))
```

```python
import functools
import math

import numpy as np
import jax
import jax.numpy as jnp
from jax import lax
from jax.experimental import pallas as pl
from jax.experimental.pallas import tpu as pltpu

F32 = jnp.float32
BF16 = jnp.bfloat16

D_MODEL = 2048
A_WIDTH = 1024
A_HEADS = 8
HEAD_DIM = 128
A_DILATIONS = (1, 4, 16)
A_SPAN = 128
A_WINDOW_MAX = 2048
ATTN_UNROLL = 8
B_WIDTH = 512
B_GROUPS = 4
CHUNK = 128
C_WIDTH = 512
C_GROUPS = 32
C_GROUP_W = 16
SSM_P = 64
N_IN = 3 * A_WIDTH + 2 * B_WIDTH + C_WIDTH
D_FF = 5632
DEC_SEQ = 4
EPS = 1e-6
NEG = -1e30

LANE = 128
COL_Q, COL_K, COL_V = 0, A_WIDTH // LANE, 2 * A_WIDTH // LANE
COL_BU = 3 * A_WIDTH // LANE
COL_BV = COL_BU + B_WIDTH // LANE
COL_CU = COL_BV + B_WIDTH // LANE

SSM_LANE_GROUPS = 4
SSM_LG_STATES = C_GROUPS * SSM_P // SSM_LANE_GROUPS
SSM_R = 64
SSM_NC = 32

MIB = 1024 * 1024


def _params(sem, vmem_mib):
    return pltpu.CompilerParams(dimension_semantics=sem, vmem_limit_bytes=vmem_mib * MIB)


def _gelu(x):
    c = math.sqrt(2.0 / math.pi)
    return x * (0.5 * (1.0 + jnp.tanh(c * (x + 0.044715 * (x * x * x)))))


def _rms(x):
    return x * lax.rsqrt(jnp.mean(x * x, axis=-1, keepdims=True) + EPS)


_NT = (((1,), (1,)), ((), ()))


def _proj_body(x_ref, g_ref, w_ref, o_ref, h_ref):
    @pl.when(pl.program_id(1) == 0)
    def _():
        h_ref[...] = (_rms(x_ref[...]) * g_ref[...]).astype(BF16)

    o_ref[...] = jnp.dot(h_ref[...], w_ref[...], preferred_element_type=F32)


def _proj(x, g, w, tm, tn=512):
    t = x.shape[0]
    return pl.pallas_call(
        _proj_body,
        grid=(t // tm, N_IN // tn),
        in_specs=[
            pl.BlockSpec((tm, D_MODEL), lambda i, j: (i, 0)),
            pl.BlockSpec((1, D_MODEL), lambda i, j: (0, 0)),
            pl.BlockSpec((D_MODEL, tn), lambda i, j: (0, j)),
        ],
        out_specs=pl.BlockSpec((tm, tn), lambda i, j: (i, j)),
        out_shape=jax.ShapeDtypeStruct((t, N_IN), F32),
        scratch_shapes=[pltpu.VMEM((tm, D_MODEL), BF16)],
        compiler_params=_params(("arbitrary", "arbitrary"), 48),
        name="proj_in",
    )(x, g, w)


def _attn_prompt_body(q_ref, k_ref, v_ref, o_ref, acc_ref, m_ref, l_ref, *, seq):
    m_ref[...] = jnp.full(m_ref.shape, NEG, F32)
    l_ref[...] = jnp.zeros(l_ref.shape, F32)
    acc_ref[...] = jnp.zeros(acc_ref.shape, F32)
    scale = math.log2(math.e) / math.sqrt(HEAD_DIM)
    ii = lax.broadcasted_iota(jnp.int32, (A_SPAN, 2 * A_SPAN), 0)
    jj = lax.broadcasted_iota(jnp.int32, (A_SPAN, 2 * A_SPAN), 1)
    band = (jj >= ii) & (jj <= ii + A_SPAN)
    in_cur = jj >= A_SPAN

    for dil in A_DILATIONS:
        nb = seq // (A_SPAN * dil)

        def rows(start, dil=dil):
            if dil == 1:
                return pl.ds(pl.multiple_of(start, A_SPAN), A_SPAN)
            return pl.ds(start, A_SPAN, stride=dil)

        def blocks(pairs, dil=dil, rows=rows):
            work = []
            for n, r in pairs:
                base = r + n * (A_SPAN * dil)
                prev = r + jnp.maximum(n - 1, 0) * (A_SPAN * dil)
                q = (q_ref[rows(base), :] * scale).astype(BF16)
                k2 = jnp.concatenate([k_ref[rows(prev), :], k_ref[rows(base), :]], axis=0).astype(BF16)
                v2 = jnp.concatenate([v_ref[rows(prev), :], v_ref[rows(base), :]], axis=0).astype(BF16)
                s = lax.dot_general(q, k2, _NT, preferred_element_type=F32)
                s = jnp.where(band & (in_cur | (n > 0)), s, NEG)
                m_old = m_ref[rows(base), :]
                m_new = jnp.maximum(m_old, jnp.max(s, axis=1, keepdims=True))
                alpha = jnp.exp2(m_old - m_new)
                p = jnp.exp2(s - jnp.concatenate([m_new, m_new], axis=1))
                l_new = alpha * l_ref[rows(base), :] + jnp.sum(p, axis=1, keepdims=True)
                acc_new = alpha * acc_ref[rows(base), :] + jnp.dot(p.astype(BF16), v2, preferred_element_type=F32)
                work.append((base, m_new, l_new, acc_new))
            for base, m_new, l_new, acc_new in work:
                m_ref[rows(base), :] = m_new
                l_ref[rows(base), :] = l_new
                acc_ref[rows(base), :] = acc_new

        unroll = min(ATTN_UNROLL, nb * dil)
        if unroll <= nb:
            per_res = nb // unroll

            def trip(i, c, blocks=blocks, unroll=unroll, per_res=per_res):
                r, first = i // per_res, (i % per_res) * unroll
                blocks([(first + u, r) for u in range(unroll)])
                return c
        else:
            res_per_trip = unroll // nb

            def trip(i, c, blocks=blocks, unroll=unroll, nb=nb, res_per_trip=res_per_trip):
                blocks([(u % nb, i * res_per_trip + u // nb) for u in range(unroll)])
                return c

        lax.fori_loop(0, nb * dil // unroll, trip, 0)

    o_ref[...] = acc_ref[...] / l_ref[...]


def _attn_prompt(p):
    seq = p.shape[0]
    blk = (seq, HEAD_DIM)
    return pl.pallas_call(
        functools.partial(_attn_prompt_body, seq=seq),
        grid=(A_HEADS,),
        in_specs=[
            pl.BlockSpec(blk, lambda h: (0, COL_Q + h)),
            pl.BlockSpec(blk, lambda h: (0, COL_K + h)),
            pl.BlockSpec(blk, lambda h: (0, COL_V + h)),
        ],
        out_specs=pl.BlockSpec(blk, lambda h: (0, h)),
        out_shape=jax.ShapeDtypeStruct((seq, A_WIDTH), F32),
        scratch_shapes=[pltpu.VMEM(blk, F32)] * 3,
        compiler_params=_params(("arbitrary",), 56),
        name="attn_prompt",
    )(p, p, p)


def _sample_attn_masks():
    j = np.arange(DEC_SEQ)[:, None, None, None]
    h = np.arange(A_HEADS)[None, :, None, None]
    hk = np.arange(A_HEADS)[None, None, None, :]
    same = (h == hk)
    i = np.arange(512)[None, None, :, None]
    cnt_l = same * ((i >= 384 + j).astype(np.int32) + ((i - j) % 4 == 0).astype(np.int32))
    jn = np.arange(DEC_SEQ)[None, None, :, None]
    cnt_n = same * ((jn <= j).astype(np.int32) + 2 * (jn == j).astype(np.int32))
    g = np.arange(128)[None, None, :, None]
    head_match = np.broadcast_to(same[0:1], (1, A_HEADS, 128, A_HEADS))
    return (cnt_l.reshape(DEC_SEQ * A_HEADS, 512 * A_HEADS).astype(np.float32),
            cnt_n.reshape(DEC_SEQ * A_HEADS, DEC_SEQ * A_HEADS).astype(np.float32),
            (head_match + 0 * g).reshape(A_HEADS, 128 * A_HEADS).astype(np.float32))


def _attn_sample_body(q_ref, k_ref, v_ref, ka0, ka1, ka2, ka3, kl_ref, va0, va1, va2, va3, vl_ref,
                      cl_ref, cn_ref, hm_ref, o_ref):
    nh = A_HEADS
    q = q_ref[...] * (1.0 / math.sqrt(HEAD_DIM))
    qs = q.astype(BF16)
    cl, cn, hm = cl_ref[...], cn_ref[...], hm_ref[...]

    kl = kl_ref[...].reshape(512 * nh, HEAD_DIM).astype(BF16)
    s_l = lax.dot_general(qs, kl, _NT, preferred_element_type=F32)
    s_n = lax.dot_general(qs, k_ref[...].astype(BF16), _NT, preferred_element_type=F32)
    s_a = []
    for res, ka in enumerate((ka0, ka1, ka2, ka3)):
        kk = ka[...].reshape(128 * nh, HEAD_DIM).astype(BF16)
        s_a.append(lax.dot_general(q[res * nh:(res + 1) * nh].astype(BF16), kk, _NT,
                                   preferred_element_type=F32))
    m = jnp.max(jnp.where(cl > 0, s_l, NEG), axis=1, keepdims=True)
    m = jnp.maximum(m, jnp.max(jnp.where(cn > 0, s_n, NEG), axis=1, keepdims=True))
    m_a = jnp.concatenate([jnp.max(jnp.where(hm > 0, s, NEG), axis=1, keepdims=True) for s in s_a], axis=0)
    m = jnp.maximum(m, m_a)

    p_l = cl * jnp.exp(jnp.minimum(s_l - m, 0.0))
    p_n = cn * jnp.exp(jnp.minimum(s_n - m, 0.0))
    den = jnp.sum(p_l, axis=1, keepdims=True) + jnp.sum(p_n, axis=1, keepdims=True)
    vl = vl_ref[...].reshape(512 * nh, HEAD_DIM).astype(BF16)
    out = jnp.dot(p_l.astype(BF16), vl, preferred_element_type=F32)
    out = out + jnp.dot(p_n.astype(BF16), v_ref[...].astype(BF16), preferred_element_type=F32)
    o_a, d_a = [], []
    for res, va in enumerate((va0, va1, va2, va3)):
        p_a = hm * jnp.exp(jnp.minimum(s_a[res] - m[res * nh:(res + 1) * nh], 0.0))
        d_a.append(jnp.sum(p_a, axis=1, keepdims=True))
        vv = va[...].reshape(128 * nh, HEAD_DIM).astype(BF16)
        o_a.append(jnp.dot(p_a.astype(BF16), vv, preferred_element_type=F32))
    out = out + jnp.concatenate(o_a, axis=0)
    den = den + jnp.concatenate(d_a, axis=0)
    o_ref[...] = out / den


def _attn_sample(q3, k3, v3, cache_k, cache_v, layer):
    db = q3.shape[0]
    depth, _, w_buf, nh, hd = cache_k.shape
    assert (w_buf, nh, hd) == (A_WINDOW_MAX, A_HEADS, HEAD_DIM) and q3.shape[1] == DEC_SEQ * A_HEADS
    cnt_l, cnt_n, head_match = _sample_attn_masks()
    rows = DEC_SEQ * A_HEADS

    def strided_specs():
        return [pl.BlockSpec((None, None, 128, None, nh, hd), functools.partial(
            lambda b, res: (layer, b, 0, res, 0, 0), res=res)) for res in range(DEC_SEQ)]

    last_spec = pl.BlockSpec((None, None, None, 512, nh, hd), lambda b: (layer, b, 3, 0, 0, 0))
    tok_spec = pl.BlockSpec((None, rows, hd), lambda b: (b, 0, 0))

    def whole(a):
        return pl.BlockSpec(a.shape, lambda b: (0, 0))

    k16 = cache_k.reshape(depth, db, 128, 16, nh, hd)
    v16 = cache_v.reshape(depth, db, 128, 16, nh, hd)
    k512 = cache_k.reshape(depth, db, 4, 512, nh, hd)
    v512 = cache_v.reshape(depth, db, 4, 512, nh, hd)
    return pl.pallas_call(
        _attn_sample_body,
        grid=(db,),
        in_specs=[tok_spec, tok_spec, tok_spec] + strided_specs() + [last_spec] + strided_specs() + [last_spec]
        + [whole(cnt_l), whole(cnt_n), whole(head_match)],
        out_specs=tok_spec,
        out_shape=jax.ShapeDtypeStruct((db, rows, hd), F32),
        compiler_params=_params(("arbitrary",), 48),
        name="attn_sample",
    )(q3, k3, v3, k16, k16, k16, k16, k512, v16, v16, v16, v16, v512,
      jnp.asarray(cnt_l), jnp.asarray(cnt_n), jnp.asarray(head_match))


def _gmlp_prompt_body(bu_ref, bv_ref, gg_ref, ws_ref, bs_ref, o_ref, *, chunks):
    ri = lax.broadcasted_iota(jnp.int32, (CHUNK, CHUNK), 0)
    ci = lax.broadcasted_iota(jnp.int32, (CHUNK, CHUNK), 1)
    wm = jnp.where(ci <= ri, ws_ref[0], 0.0).astype(BF16)
    bias = bs_ref[0]
    gain = gg_ref[...]
    for c in range(chunks):
        sl = slice(c * CHUNK, (c + 1) * CHUNK)
        gv = _rms(_gelu(bv_ref[sl, :])) * gain
        mix = jnp.dot(wm, gv.astype(BF16), preferred_element_type=F32) + bias
        o_ref[sl, :] = _gelu(bu_ref[sl, :]) * mix


def _gmlp_prompt(p, gain, ws, bs, chunks=8):
    seq = p.shape[0]
    tm = chunks * CHUNK
    return pl.pallas_call(
        functools.partial(_gmlp_prompt_body, chunks=chunks),
        grid=(seq // tm, B_GROUPS),
        in_specs=[
            pl.BlockSpec((tm, LANE), lambda i, g: (i, COL_BU + g)),
            pl.BlockSpec((tm, LANE), lambda i, g: (i, COL_BV + g)),
            pl.BlockSpec((1, LANE), lambda i, g: (0, g)),
            pl.BlockSpec((1, CHUNK, CHUNK), lambda i, g: (g, 0, 0)),
            pl.BlockSpec((1, CHUNK, 1), lambda i, g: (g, 0, 0)),
        ],
        out_specs=pl.BlockSpec((tm, LANE), lambda i, g: (i, g)),
        out_shape=jax.ShapeDtypeStruct((seq, B_WIDTH), F32),
        compiler_params=_params(("arbitrary", "arbitrary"), 32),
        name="gmlp_prompt",
    )(p, p, gain, ws, bs.reshape(B_GROUPS, CHUNK, 1))


def _gmlp_sample_body(ws_ref, bs_ref, bu_ref, bv_ref, gg_ref, o_ref, gv_ref, *, db):
    g = pl.program_id(0)
    gv = _rms(_gelu(bv_ref[...])) * gg_ref[...]
    gv_ref[...] = gv
    gu = _gelu(bu_ref[...])
    for i in range(DEC_SEQ):
        mix = jnp.full((db, LANE), bs_ref[g, i], F32)
        for j in range(i + 1):
            mix = mix + ws_ref[g, i * DEC_SEQ + j] * gv[j * db:(j + 1) * db]
        o_ref[i * db:(i + 1) * db, :] = gu[i * db:(i + 1) * db] * mix


def _gmlp_sample(p, gain, ws, bs):
    t = p.shape[0]
    db = t // DEC_SEQ
    ws4 = ws[:, :DEC_SEQ, :DEC_SEQ].reshape(B_GROUPS, DEC_SEQ * DEC_SEQ)
    bs4 = bs[:, :DEC_SEQ]
    smem = pl.BlockSpec(memory_space=pltpu.SMEM)
    return pl.pallas_call(
        functools.partial(_gmlp_sample_body, db=db),
        grid=(B_GROUPS,),
        in_specs=[
            smem, smem,
            pl.BlockSpec((t, LANE), lambda g: (0, COL_BU + g)),
            pl.BlockSpec((t, LANE), lambda g: (0, COL_BV + g)),
            pl.BlockSpec((1, LANE), lambda g: (0, g)),
        ],
        out_specs=[pl.BlockSpec((t, LANE), lambda g: (0, g))] * 2,
        out_shape=[jax.ShapeDtypeStruct((t, B_WIDTH), F32)] * 2,
        compiler_params=_params(("arbitrary",), 32),
        name="gmlp_sample",
    )(ws4, bs4, p, p, gain)


def _ssm_prep_body(lre_ref, lim_ref, ldt_ref, bre_ref, bim_ref, pwre_ref, pwim_ref, bbre_ref, bbim_ref):
    lre, lim = lre_ref[...], lim_ref[...]
    dt = jnp.exp(ldt_ref[...])
    k = lax.broadcasted_iota(jnp.int32, pwre_ref.shape, 0).astype(F32) + 1.0
    mag = jnp.exp(k * (dt * lre))
    ang = k * (dt * lim)
    pwre = mag * jnp.cos(ang)
    pwim = mag * jnp.sin(ang)
    pwre_ref[...] = pwre
    pwim_ref[...] = pwim
    xr, xi = pwre[0:1] - 1.0, pwim[0:1]
    den = lre * lre + lim * lim
    cr = (xr * lre + xi * lim) / den
    ci = (xi * lre - xr * lim) / den
    bre, bim = bre_ref[...], bim_ref[...]
    bbre_ref[...] = cr * bre - ci * bim
    bbim_ref[...] = cr * bim + ci * bre


def _ssm_prep(a_re, a_im, log_dt, b_re, b_im):
    n = C_GROUPS * SSM_P
    row = lambda a: a.reshape(1, n)
    bt = lambda b: jnp.transpose(b, (2, 0, 1)).reshape(C_GROUP_W, n)
    shapes = [jax.ShapeDtypeStruct((SSM_R, n), F32)] * 2 + [jax.ShapeDtypeStruct((C_GROUP_W, n), F32)] * 2
    pwre, pwim, bbre, bbim = pl.pallas_call(
        _ssm_prep_body, out_shape=shapes, name="ssm_prep",
    )(row(a_re), row(a_im), row(jnp.repeat(log_dt, SSM_P)), bt(b_re), bt(b_im))
    lg, st = SSM_LANE_GROUPS, SSM_LG_STATES
    split = lambda t: jnp.transpose(t.reshape(t.shape[0], lg, st), (1, 0, 2))
    pw = jnp.concatenate([split(pwre), split(pwim)], axis=-1)
    eye = jnp.eye(C_GROUPS // lg, dtype=F32)

    def blockdiag(bb):
        r = bb.reshape(C_GROUP_W, lg, C_GROUPS // lg, SSM_P)
        return jnp.einsum('ab,mlbp->lambp', eye, r).reshape(lg, LANE, st)

    bdb = jnp.concatenate([blockdiag(bbre), blockdiag(bbim)], axis=-1).astype(BF16)
    return pw, bdb


def _ssm_out_matrix(c):
    lg = SSM_LANE_GROUPS
    eye = jnp.eye(C_GROUPS // lg, dtype=F32)
    r = c.reshape(lg, C_GROUPS // lg, C_GROUP_W, SSM_P)
    return jnp.einsum('ab,lanp->lapbn', eye, r).reshape(lg, SSM_LG_STATES, LANE).astype(BF16)


def _cmul_add(are, aim, hre, him, xre, xim):
    return are * hre - aim * him + xre, are * him + aim * hre + xim


def _s5_prompt_body(u_ref, bdb_ref, cre_ref, cim_ref, d_ref, pw_ref, yg_ref, hl_ref,
                    u3_ref, h_ref, car_ref, hc_ref):
    r_steps, nc, st = SSM_R, SSM_NC, SSM_LG_STATES
    re, im = slice(0, st), slice(st, 2 * st)

    @pl.when(pl.program_id(1) == 0)
    def _():
        hc_ref[...] = jnp.zeros(hc_ref.shape, F32)

    def rows(r):
        return pl.ds(pl.multiple_of(r * nc, nc), nc)

    def regroup(r, c):
        u3_ref[rows(r), :] = u_ref[pl.ds(r, nc, stride=r_steps), :]
        return c

    lax.fori_loop(0, r_steps, regroup, 0)
    h_ref[...] = jnp.dot(u3_ref[...].astype(BF16), bdb_ref[0], preferred_element_type=F32)

    half = st // 2
    for hf in range(2):
        cre_ = slice(hf * half, (hf + 1) * half)
        cim_ = slice(st + hf * half, st + (hf + 1) * half)
        are = jnp.broadcast_to(pw_ref[0, 0:1, cre_], (nc, half))
        aim = jnp.broadcast_to(pw_ref[0, 0:1, cim_], (nc, half))

        def step(r, carry, cre_=cre_, cim_=cim_, are=are, aim=aim):
            hre, him = _cmul_add(are, aim, carry[0], carry[1], h_ref[rows(r), cre_], h_ref[rows(r), cim_])
            h_ref[rows(r), cre_] = hre
            h_ref[rows(r), cim_] = him
            return hre, him

        lax.fori_loop(1, r_steps, step, (h_ref[0:nc, cre_], h_ref[0:nc, cim_]))

    ends = h_ref[(r_steps - 1) * nc:r_steps * nc, :]
    are, aim = pw_ref[0, r_steps - 1:r_steps, re], pw_ref[0, r_steps - 1:r_steps, im]
    cre, cim = hc_ref[0:1, re], hc_ref[0:1, im]
    for c in range(nc):
        car_ref[c:c + 1, re] = cre
        car_ref[c:c + 1, im] = cim
        cre, cim = _cmul_add(are, aim, cre, cim, ends[c:c + 1, re], ends[c:c + 1, im])
    hc_ref[:, re] = jnp.broadcast_to(cre, (8, st))
    hc_ref[:, im] = jnp.broadcast_to(cim, (8, st))
    hl_ref[0] = hc_ref[...]

    def fix(r, c):
        pre, pim = pw_ref[0, pl.ds(r, 1), re], pw_ref[0, pl.ds(r, 1), im]
        hre, him = _cmul_add(pre, pim, car_ref[:, re], car_ref[:, im], h_ref[rows(r), re], h_ref[rows(r), im])
        h_ref[rows(r), re] = hre
        h_ref[rows(r), im] = him
        return c

    lax.fori_loop(0, r_steps, fix, 0)

    y = (jnp.dot(h_ref[:, re].astype(BF16), cre_ref[0], preferred_element_type=F32)
         - jnp.dot(h_ref[:, im].astype(BF16), cim_ref[0], preferred_element_type=F32)
         + d_ref[...] * u3_ref[...])
    u3_ref[...] = _gelu(y)

    def ungroup(r, c):
        yg_ref[pl.ds(r, nc, stride=r_steps), :] = u3_ref[rows(r), :]
        return c

    lax.fori_loop(0, r_steps, ungroup, 0)


def _s5_prompt(p, bdb, cre, cim, d, pw):
    seq = p.shape[0]
    tseg = SSM_R * SSM_NC
    st2 = 2 * SSM_LG_STATES
    lgs = SSM_LANE_GROUPS
    return pl.pallas_call(
        _s5_prompt_body,
        grid=(lgs, seq // tseg),
        in_specs=[
            pl.BlockSpec((tseg, LANE), lambda g, t: (t, COL_CU + g)),
            pl.BlockSpec((1, LANE, st2), lambda g, t: (g, 0, 0)),
            pl.BlockSpec((1, SSM_LG_STATES, LANE), lambda g, t: (g, 0, 0)),
            pl.BlockSpec((1, SSM_LG_STATES, LANE), lambda g, t: (g, 0, 0)),
            pl.BlockSpec((1, LANE), lambda g, t: (0, g)),
            pl.BlockSpec((1, SSM_R, st2), lambda g, t: (g, 0, 0)),
        ],
        out_specs=[
            pl.BlockSpec((tseg, LANE), lambda g, t: (t, g)),
            pl.BlockSpec((1, 8, st2), lambda g, t: (g, 0, 0)),
        ],
        out_shape=[jax.ShapeDtypeStruct((seq, C_WIDTH), F32), jax.ShapeDtypeStruct((lgs, 8, st2), F32)],
        scratch_shapes=[
            pltpu.VMEM((tseg, LANE), F32),
            pltpu.VMEM((tseg, st2), F32),
            pltpu.VMEM((SSM_NC, st2), F32),
            pltpu.VMEM((8, st2), F32),
        ],
        compiler_params=_params(("arbitrary", "arbitrary"), 40),
        name="s5_prompt",
    )(p, bdb, cre, cim, d, pw)


def _s5_sample_body(u_ref, hre_ref, him_ref, bdb_ref, cre_ref, cim_ref, d_ref, pw_ref,
                    yg_ref, ore_ref, oim_ref, h_ref, *, db):
    st = SSM_LG_STATES
    re, im = slice(0, st), slice(st, 2 * st)
    u = u_ref[...]
    x = jnp.dot(u.astype(BF16), bdb_ref[0], preferred_element_type=F32)
    are, aim = pw_ref[0, 0:1, re], pw_ref[0, 0:1, im]
    hre, him = hre_ref[...], him_ref[...]
    for j in range(DEC_SEQ):
        rj = slice(j * db, (j + 1) * db)
        hre, him = _cmul_add(are, aim, hre, him, x[rj, re], x[rj, im])
        h_ref[rj, re] = hre
        h_ref[rj, im] = him
    ore_ref[...] = hre
    oim_ref[...] = him
    y = (jnp.dot(h_ref[:, re].astype(BF16), cre_ref[0], preferred_element_type=F32)
         - jnp.dot(h_ref[:, im].astype(BF16), cim_ref[0], preferred_element_type=F32)
         + d_ref[...] * u)
    yg_ref[...] = _gelu(y)


def _s5_sample(p, h0re, h0im, bdb, cre, cim, d, pw):
    t = p.shape[0]
    db = t // DEC_SEQ
    st = SSM_LG_STATES
    lgs = SSM_LANE_GROUPS
    return pl.pallas_call(
        functools.partial(_s5_sample_body, db=db),
        grid=(lgs,),
        in_specs=[
            pl.BlockSpec((t, LANE), lambda g: (0, COL_CU + g)),
            pl.BlockSpec((db, st), lambda g: (0, g)),
            pl.BlockSpec((db, st), lambda g: (0, g)),
            pl.BlockSpec((1, LANE, 2 * st), lambda g: (g, 0, 0)),
            pl.BlockSpec((1, st, LANE), lambda g: (g, 0, 0)),
            pl.BlockSpec((1, st, LANE), lambda g: (g, 0, 0)),
            pl.BlockSpec((1, LANE), lambda g: (0, g)),
            pl.BlockSpec((1, SSM_R, 2 * st), lambda g: (g, 0, 0)),
        ],
        out_specs=[
            pl.BlockSpec((t, LANE), lambda g: (0, g)),
            pl.BlockSpec((db, st), lambda g: (0, g)),
            pl.BlockSpec((db, st), lambda g: (0, g)),
        ],
        out_shape=[jax.ShapeDtypeStruct((t, C_WIDTH), F32),
                   jax.ShapeDtypeStruct((db, lgs * st), F32), jax.ShapeDtypeStruct((db, lgs * st), F32)],
        scratch_shapes=[pltpu.VMEM((t, 2 * st), F32)],
        compiler_params=_params(("arbitrary",), 32),
        name="s5_sample",
    )(p, h0re, h0im, bdb, cre, cim, d, pw)


def _mix_out_body(a_ref, b_ref, y_ref, x_ref, wg_ref, ga_ref, gb_ref, gc_ref, wo_ref, o_ref):
    glu = jnp.dot(y_ref[...].astype(BF16), wg_ref[...], preferred_element_type=F32)
    c = glu[:, :C_WIDTH] * jax.nn.sigmoid(glu[:, C_WIDTH:])
    mixed = jnp.concatenate([
        (_rms(a_ref[...]) * ga_ref[...]).astype(BF16),
        (_rms(b_ref[...]) * gb_ref[...]).astype(BF16),
        (_rms(c) * gc_ref[...]).astype(BF16)], axis=1)
    o_ref[...] = x_ref[...] + jnp.dot(mixed, wo_ref[...], preferred_element_type=F32)


def _mix_out(a, b, y, x, w_glu, g_a, g_b, g_c, w_out, tm=512):
    t = x.shape[0]
    row = lambda w: pl.BlockSpec((tm, w), lambda i: (i, 0))
    whole = lambda arr: pl.BlockSpec(arr.shape, lambda i: (0, 0))
    return pl.pallas_call(
        _mix_out_body,
        grid=(t // tm,),
        in_specs=[row(A_WIDTH), row(B_WIDTH), row(C_WIDTH), row(D_MODEL),
                  whole(w_glu), whole(g_a), whole(g_b), whole(g_c), whole(w_out)],
        out_specs=row(D_MODEL),
        out_shape=jax.ShapeDtypeStruct((t, D_MODEL), F32),
        compiler_params=_params(("arbitrary",), 56),
        name="mix_out",
    )(a, b, y, x, w_glu, g_a, g_b, g_c, w_out)


FFN_HALO = 16


def _conv_gate(ug, uv, cwg_ref, cwv_ref, cbg_ref, cbv_ref, taps):
    def conv(u, cw_ref, cb_ref):
        return ((cb_ref[...] + cw_ref[0:1, :] * taps[0](u)) + cw_ref[1:2, :] * taps[1](u)) + cw_ref[2:3, :] * taps[2](u)
    gate = conv(ug, cwg_ref, cbg_ref)
    return (gate * jax.nn.sigmoid(gate)) * conv(uv, cwv_ref, cbv_ref)


def _ffn_prompt_body(x_ref, halo_ref, g_ref, wg_ref, wv_ref, cwg_ref, cwv_ref, cbg_ref, cbv_ref, wd_ref,
                     o_ref, tail_ref, h_ref, *, tm):
    i, f = pl.program_id(0), pl.program_id(1)
    hl = FFN_HALO

    @pl.when(f == 0)
    def _():
        h_ref[hl:, :] = (_rms(x_ref[...]) * g_ref[...]).astype(BF16)
        prev = _rms(halo_ref[...]) * g_ref[...]
        h_ref[0:hl, :] = jnp.where(i > 0, prev, 0.0).astype(BF16)
        o_ref[...] = x_ref[...]

    h = h_ref[...]
    ug = jnp.dot(h, wg_ref[...], preferred_element_type=F32)
    uv = jnp.dot(h, wv_ref[...], preferred_element_type=F32)
    taps = (lambda u: u[hl - 2:hl - 2 + tm], lambda u: u[hl - 1:hl - 1 + tm], lambda u: u[hl:])
    act = _conv_gate(ug, uv, cwg_ref, cwv_ref, cbg_ref, cbv_ref, taps)
    o_ref[...] += jnp.dot(act.astype(BF16), wd_ref[...], preferred_element_type=F32)

    tail_ref[0] = ug[hl + tm - 8:]
    tail_ref[1] = uv[hl + tm - 8:]


def _ffn_prompt(x, g, w_up, conv_w, conv_b, w_down, tm=512, tf=512):
    t = x.shape[0]
    nf = D_FF // tf
    hl = FFN_HALO
    cb = conv_b.reshape(1, 2 * D_FF)
    return pl.pallas_call(
        functools.partial(_ffn_prompt_body, tm=tm),
        grid=(t // tm, nf),
        in_specs=[
            pl.BlockSpec((tm, D_MODEL), lambda i, f: (i, 0), pipeline_mode=pl.Buffered(1)),
            pl.BlockSpec((hl, D_MODEL), lambda i, f: (jnp.maximum(i * (tm // hl) - 1, 0), 0)),
            pl.BlockSpec((1, D_MODEL), lambda i, f: (0, 0)),
            pl.BlockSpec((D_MODEL, tf), lambda i, f: (0, f)),
            pl.BlockSpec((D_MODEL, tf), lambda i, f: (0, nf + f)),
            pl.BlockSpec((3, tf), lambda i, f: (0, f)),
            pl.BlockSpec((3, tf), lambda i, f: (0, nf + f)),
            pl.BlockSpec((1, tf), lambda i, f: (0, f)),
            pl.BlockSpec((1, tf), lambda i, f: (0, nf + f)),
            pl.BlockSpec((tf, D_MODEL), lambda i, f: (f, 0)),
        ],
        out_specs=[
            pl.BlockSpec((tm, D_MODEL), lambda i, f: (i, 0), pipeline_mode=pl.Buffered(1)),
            pl.BlockSpec((None, 2, 8, tf), lambda i, f: (i, 0, 0, f)),
        ],
        out_shape=[jax.ShapeDtypeStruct((t, D_MODEL), F32), jax.ShapeDtypeStruct((t // tm, 2, 8, D_FF), F32)],
        scratch_shapes=[pltpu.VMEM((hl + tm, D_MODEL), BF16)],
        compiler_params=_params(("arbitrary", "arbitrary"), 56),
        name="ffn_prompt",
    )(x, x, g, w_up, w_up, conv_w, conv_w, cb, cb, w_down)


def _ffn_sample_body(x_ref, g_ref, sg_ref, sv_ref, wg_ref, wv_ref, cwg_ref, cwv_ref, cbg_ref, cbv_ref, wd_ref,
                     o_ref, tail_ref, h_ref, *, db):
    f = pl.program_id(0)
    t = DEC_SEQ * db

    @pl.when(f == 0)
    def _():
        h_ref[...] = (_rms(x_ref[...]) * g_ref[...]).astype(BF16)
        o_ref[...] = x_ref[...]

    h = h_ref[...]
    ug = jnp.dot(h, wg_ref[...], preferred_element_type=F32)
    uv = jnp.dot(h, wv_ref[...], preferred_element_type=F32)
    pg = jnp.concatenate([sg_ref[0], sg_ref[1], ug], axis=0)
    pv = jnp.concatenate([sv_ref[0], sv_ref[1], uv], axis=0)
    taps = (lambda u: u[0:t], lambda u: u[db:db + t], lambda u: u[2 * db:2 * db + t])
    act = _conv_gate(pg, pv, cwg_ref, cwv_ref, cbg_ref, cbv_ref, taps)
    o_ref[...] += jnp.dot(act.astype(BF16), wd_ref[...], preferred_element_type=F32)

    tail_ref[0, 0] = ug[(DEC_SEQ - 2) * db:(DEC_SEQ - 1) * db]
    tail_ref[0, 1] = ug[(DEC_SEQ - 1) * db:]
    tail_ref[1, 0] = uv[(DEC_SEQ - 2) * db:(DEC_SEQ - 1) * db]
    tail_ref[1, 1] = uv[(DEC_SEQ - 1) * db:]


def _ffn_sample(x, g, state, w_up, conv_w, conv_b, w_down, tf=512):
    t = x.shape[0]
    db = t // DEC_SEQ
    nf = D_FF // tf
    cb = conv_b.reshape(1, 2 * D_FF)
    return pl.pallas_call(
        functools.partial(_ffn_sample_body, db=db),
        grid=(nf,),
        in_specs=[
            pl.BlockSpec((t, D_MODEL), lambda f: (0, 0)),
            pl.BlockSpec((1, D_MODEL), lambda f: (0, 0)),
            pl.BlockSpec((2, db, tf), lambda f: (0, 0, f)),
            pl.BlockSpec((2, db, tf), lambda f: (0, 0, nf + f)),
            pl.BlockSpec((D_MODEL, tf), lambda f: (0, f)),
            pl.BlockSpec((D_MODEL, tf), lambda f: (0, nf + f)),
            pl.BlockSpec((3, tf), lambda f: (0, f)),
            pl.BlockSpec((3, tf), lambda f: (0, nf + f)),
            pl.BlockSpec((1, tf), lambda f: (0, f)),
            pl.BlockSpec((1, tf), lambda f: (0, nf + f)),
            pl.BlockSpec((tf, D_MODEL), lambda f: (f, 0)),
        ],
        out_specs=[
            pl.BlockSpec((t, D_MODEL), lambda f: (0, 0)),
            pl.BlockSpec((2, 2, db, tf), lambda f: (0, 0, 0, f)),
        ],
        out_shape=[jax.ShapeDtypeStruct((t, D_MODEL), F32), jax.ShapeDtypeStruct((2, 2, db, D_FF), F32)],
        scratch_shapes=[pltpu.VMEM((t, D_MODEL), BF16)],
        compiler_params=_params(("arbitrary",), 48),
        name="ffn_sample",
    )(x, g, state, state, w_up, w_up, conv_w, conv_w, cb, cb, w_down)


def _norm_body(x_ref, g_ref, o_ref):
    o_ref[...] = _rms(x_ref[...]) * g_ref[...]


def _final_norm(x, g, tm=512):
    t = x.shape[0]
    tm = min(tm, t)
    return pl.pallas_call(
        _norm_body,
        grid=(t // tm,),
        in_specs=[pl.BlockSpec((tm, D_MODEL), lambda i: (i, 0)), pl.BlockSpec((1, D_MODEL), lambda i: (0, 0))],
        out_specs=pl.BlockSpec((tm, D_MODEL), lambda i: (i, 0)),
        out_shape=jax.ShapeDtypeStruct((t, D_MODEL), F32),
        compiler_params=_params(("arbitrary",), 32),
        name="final_norm",
    )(x, g)


def kernel(x_prompt, x_sample, cache_win_k, cache_win_v, state_ssm_re, state_ssm_im, state_ffn_conv,
           g_mix, w_in, g_out_a, g_out_b, g_out_c, gmlp_gv, gmlp_ws, gmlp_bs, ssm_a_re, ssm_a_im,
           ssm_log_dt, ssm_b_re, ssm_b_im, ssm_c_re, ssm_c_im, ssm_d, ssm_w_glu, w_out, g_ffn, w_up,
           conv_w, conv_b, w_down, g_final):
    depth = w_in.shape[0]
    bsz, seq, _ = x_prompt.shape
    db, ds, _ = x_sample.shape
    assert bsz == 1 and ds == DEC_SEQ and seq % (SSM_R * SSM_NC) == 0 and seq >= A_WINDOW_MAX
    n_states = C_GROUPS * SSM_P
    row = lambda v: v.reshape(1, -1)

    w_in_b, w_glu_b, w_out_b = w_in.astype(BF16), ssm_w_glu.astype(BF16), w_out.astype(BF16)
    w_up_b, w_down_b = w_up.astype(BF16), w_down.astype(BF16)

    xp = x_prompt.reshape(seq, D_MODEL)
    xs = jnp.transpose(x_sample, (1, 0, 2)).reshape(ds * db, D_MODEL)
    outs = {k: [] for k in ("kp", "vp", "ks", "vs", "gv", "rp", "ip", "rs", "is", "cp", "cs")}

    for l in range(depth):
        pw, bdb = _ssm_prep(ssm_a_re[l], ssm_a_im[l], ssm_log_dt[l], ssm_b_re[l], ssm_b_im[l])
        c_re, c_im = _ssm_out_matrix(ssm_c_re[l]), _ssm_out_matrix(ssm_c_im[l])
        d_row = row(ssm_d[l])
        gains = (row(g_out_a[l]), row(g_out_b[l]), row(g_out_c[l]))

        p = _proj(xp, row(g_mix[l]), w_in_b[l], tm=min(1024, seq))
        a = _attn_prompt(p)
        b = _gmlp_prompt(p, row(gmlp_gv[l]), gmlp_ws[l], gmlp_bs[l])
        yg, h_last = _s5_prompt(p, bdb, c_re, c_im, d_row, pw)
        xm = _mix_out(a, b, yg, xp, w_glu_b[l], *gains, w_out_b[l])
        xp, tail = _ffn_prompt(xm, row(g_ffn[l]), w_up_b[l], conv_w[l], conv_b[l], w_down_b[l], tm=min(1024, seq))
        keep = min(A_WINDOW_MAX, seq)
        outs["kp"].append(p[seq - keep:, A_WIDTH:2 * A_WIDTH].reshape(1, keep, A_HEADS, HEAD_DIM))
        outs["vp"].append(p[seq - keep:, 2 * A_WIDTH:3 * A_WIDTH].reshape(1, keep, A_HEADS, HEAD_DIM))
        st = SSM_LG_STATES
        outs["rp"].append(h_last[:, 0, :st].reshape(1, C_GROUPS, SSM_P))
        outs["ip"].append(h_last[:, 0, st:].reshape(1, C_GROUPS, SSM_P))
        outs["cp"].append(jnp.transpose(tail[-1, :, 6:8, :], (1, 0, 2)).reshape(1, 2, 2 * D_FF))

        ps = _proj(xs, row(g_mix[l]), w_in_b[l], tm=ds * db)

        def heads(cols):
            return jnp.transpose(cols.reshape(ds, db, A_HEADS, HEAD_DIM), (1, 0, 2, 3)).reshape(
                db, ds * A_HEADS, HEAD_DIM)

        q3, k3, v3 = (heads(ps[:, i * A_WIDTH:(i + 1) * A_WIDTH]) for i in range(3))
        a3 = _attn_sample(q3, k3, v3, cache_win_k, cache_win_v, l)
        a_s = jnp.transpose(a3.reshape(db, ds, A_WIDTH), (1, 0, 2)).reshape(ds * db, A_WIDTH)
        b_s, gv_s = _gmlp_sample(ps, row(gmlp_gv[l]), gmlp_ws[l], gmlp_bs[l])
        yg_s, hre, him = _s5_sample(ps, state_ssm_re[l].reshape(db, n_states), state_ssm_im[l].reshape(db, n_states),
                                    bdb, c_re, c_im, d_row, pw)
        xm_s = _mix_out(a_s, b_s, yg_s, xs, w_glu_b[l], *gains, w_out_b[l], tm=ds * db)
        xs, tail_s = _ffn_sample(xm_s, row(g_ffn[l]), jnp.transpose(state_ffn_conv[l], (1, 0, 2)),
                                 w_up_b[l], conv_w[l], conv_b[l], w_down_b[l])
        outs["ks"].append(k3.reshape(db, ds, A_HEADS, HEAD_DIM))
        outs["vs"].append(v3.reshape(db, ds, A_HEADS, HEAD_DIM))
        outs["gv"].append(jnp.transpose(gv_s.reshape(ds, db, B_WIDTH), (1, 0, 2)))
        outs["rs"].append(hre.reshape(db, C_GROUPS, SSM_P))
        outs["is"].append(him.reshape(db, C_GROUPS, SSM_P))
        outs["cs"].append(jnp.transpose(tail_s, (2, 1, 0, 3)).reshape(db, 2, 2 * D_FF))

    y_p = _final_norm(xp, row(g_final)).reshape(1, seq, D_MODEL)
    y_s = jnp.transpose(_final_norm(xs, row(g_final)).reshape(ds, db, D_MODEL), (1, 0, 2))
    st = lambda k: jnp.stack(outs[k])
    return (y_p, y_s, st("kp"), st("vp"), st("ks"), st("vs"), st("gv"),
            st("rp"), st("ip"), st("rs"), st("is"), st("cp"), st("cs"))
```

```python
import functools
import math

import numpy as np
import jax
import jax.numpy as jnp
from jax import lax
from jax.experimental import pallas as pl
from jax.experimental.pallas import tpu as pltpu

F32 = jnp.float32
BF16 = jnp.bfloat16

D_MODEL = 2048
A_WIDTH = 1024
A_HEADS = 8
HEAD_DIM = 128
A_DILATIONS = (1, 4, 16)
A_SPAN = 128
A_WINDOW_MAX = 2048
SAMPLE_TAIL_ROWS = 512
SAMPLE_FAR_GROUPS = (A_WINDOW_MAX - SAMPLE_TAIL_ROWS) // 16
ATTN_UNROLL = 8
B_WIDTH = 512
B_GROUPS = 4
CHUNK = 128
C_WIDTH = 512
C_GROUPS = 32
C_GROUP_W = 16
SSM_P = 64
N_IN = 3 * A_WIDTH + 2 * B_WIDTH + C_WIDTH
D_FF = 5632
DEC_SEQ = 4
EPS = 1e-6
NEG = -1e30

LANE = 128
COL_Q, COL_K, COL_V = 0, A_WIDTH // LANE, 2 * A_WIDTH // LANE
COL_BU = 3 * A_WIDTH // LANE
COL_BV = COL_BU + B_WIDTH // LANE
COL_CU = COL_BV + B_WIDTH // LANE

SSM_LANE_GROUPS = 4
SSM_LG_STATES = C_GROUPS * SSM_P // SSM_LANE_GROUPS
SSM_R = 64
SSM_NC = 32

MIB = 1024 * 1024


def _params(sem, vmem_mib):
    return pltpu.CompilerParams(dimension_semantics=sem, vmem_limit_bytes=vmem_mib * MIB)


def _gelu(x):
    c = math.sqrt(2.0 / math.pi)
    return x * (0.5 * (1.0 + jnp.tanh(c * (x + 0.044715 * (x * x * x)))))


def _rms(x):
    return x * lax.rsqrt(jnp.mean(x * x, axis=-1, keepdims=True) + EPS)


_NT = (((1,), (1,)), ((), ()))


def _proj_body(x_ref, g_ref, w_ref, o_ref, h_ref):
    @pl.when(pl.program_id(1) == 0)
    def _():
        h_ref[...] = (_rms(x_ref[...]) * g_ref[...]).astype(BF16)

    o_ref[...] = jnp.dot(h_ref[...], w_ref[...], preferred_element_type=F32)


def _proj(x, g, w, tm, tn=1536):
    t = x.shape[0]
    return pl.pallas_call(
        _proj_body,
        grid=(t // tm, N_IN // tn),
        in_specs=[
            pl.BlockSpec((tm, D_MODEL), lambda i, j: (i, 0)),
            pl.BlockSpec((1, D_MODEL), lambda i, j: (0, 0)),
            pl.BlockSpec((D_MODEL, tn), lambda i, j: (0, j)),
        ],
        out_specs=pl.BlockSpec((tm, tn), lambda i, j: (i, j)),
        out_shape=jax.ShapeDtypeStruct((t, N_IN), F32),
        scratch_shapes=[pltpu.VMEM((tm, D_MODEL), BF16)],
        compiler_params=_params(("arbitrary", "arbitrary"), 56),
        name="proj_in",
    )(x, g, w)


def _attn_prompt_body(q_ref, k_ref, v_ref, o_ref, acc_ref, m_ref, l_ref, *, seq):
    m_ref[...] = jnp.full(m_ref.shape, NEG, F32)
    l_ref[...] = jnp.zeros(l_ref.shape, F32)
    acc_ref[...] = jnp.zeros(acc_ref.shape, F32)
    scale = math.log2(math.e) / math.sqrt(HEAD_DIM)
    ii = lax.broadcasted_iota(jnp.int32, (A_SPAN, 2 * A_SPAN), 0)
    jj = lax.broadcasted_iota(jnp.int32, (A_SPAN, 2 * A_SPAN), 1)
    band = (jj >= ii) & (jj <= ii + A_SPAN)
    in_cur = jj >= A_SPAN

    for dil in A_DILATIONS:
        nb = seq // (A_SPAN * dil)

        def rows(start, dil=dil):
            if dil == 1:
                return pl.ds(pl.multiple_of(start, A_SPAN), A_SPAN)
            return pl.ds(start, A_SPAN, stride=dil)

        def blocks(pairs, dil=dil, rows=rows):
            work = []
            for n, r in pairs:
                base = r + n * (A_SPAN * dil)
                prev = r + jnp.maximum(n - 1, 0) * (A_SPAN * dil)
                q = (q_ref[rows(base), :] * scale).astype(BF16)
                k2 = jnp.concatenate([k_ref[rows(prev), :], k_ref[rows(base), :]], axis=0).astype(BF16)
                v2 = jnp.concatenate([v_ref[rows(prev), :], v_ref[rows(base), :]], axis=0).astype(BF16)
                s = lax.dot_general(q, k2, _NT, preferred_element_type=F32)
                s = jnp.where(band & (in_cur | (n > 0)), s, NEG)
                m_old = m_ref[rows(base), :]
                m_new = jnp.maximum(m_old, jnp.max(s, axis=1, keepdims=True))
                alpha = jnp.exp2(m_old - m_new)
                p = jnp.exp2(s - jnp.concatenate([m_new, m_new], axis=1))
                l_new = alpha * l_ref[rows(base), :] + jnp.sum(p, axis=1, keepdims=True)
                acc_new = alpha * acc_ref[rows(base), :] + jnp.dot(p.astype(BF16), v2, preferred_element_type=F32)
                work.append((base, m_new, l_new, acc_new))
            for base, m_new, l_new, acc_new in work:
                m_ref[rows(base), :] = m_new
                l_ref[rows(base), :] = l_new
                acc_ref[rows(base), :] = acc_new

        unroll = min(ATTN_UNROLL, nb * dil)
        if unroll <= nb:
            per_res = nb // unroll

            def trip(i, c, blocks=blocks, unroll=unroll, per_res=per_res):
                r, first = i // per_res, (i % per_res) * unroll
                blocks([(first + u, r) for u in range(unroll)])
                return c
        else:
            res_per_trip = unroll // nb

            def trip(i, c, blocks=blocks, unroll=unroll, nb=nb, res_per_trip=res_per_trip):
                blocks([(u % nb, i * res_per_trip + u // nb) for u in range(unroll)])
                return c

        lax.fori_loop(0, nb * dil // unroll, trip, 0)

    o_ref[...] = acc_ref[...] / l_ref[...]


def _attn_prompt(p):
    seq = p.shape[0]
    blk = (seq, HEAD_DIM)
    return pl.pallas_call(
        functools.partial(_attn_prompt_body, seq=seq),
        grid=(A_HEADS,),
        in_specs=[
            pl.BlockSpec(blk, lambda h: (0, COL_Q + h)),
            pl.BlockSpec(blk, lambda h: (0, COL_K + h)),
            pl.BlockSpec(blk, lambda h: (0, COL_V + h)),
        ],
        out_specs=pl.BlockSpec(blk, lambda h: (0, h)),
        out_shape=jax.ShapeDtypeStruct((seq, A_WIDTH), F32),
        scratch_shapes=[pltpu.VMEM(blk, F32)] * 3,
        compiler_params=_params(("arbitrary",), 56),
        name="attn_prompt",
    )(p, p, p)


def _sample_attn_masks():
    j = np.arange(DEC_SEQ)[:, None, None, None]
    h = np.arange(A_HEADS)[None, :, None, None]
    hk = np.arange(A_HEADS)[None, None, None, :]
    same = (h == hk)
    i = np.arange(512)[None, None, :, None]
    cnt_l = same * ((i >= 384 + j).astype(np.int32) + ((i - j) % 4 == 0).astype(np.int32))
    jn = np.arange(DEC_SEQ)[None, None, :, None]
    cnt_n = same * ((jn <= j).astype(np.int32) + 2 * (jn == j).astype(np.int32))
    g = np.arange(128)[None, None, :, None]
    head_match = np.broadcast_to(same[0:1], (1, A_HEADS, 128, A_HEADS))
    return (cnt_l.reshape(DEC_SEQ * A_HEADS, 512 * A_HEADS).astype(np.float32),
            cnt_n.reshape(DEC_SEQ * A_HEADS, DEC_SEQ * A_HEADS).astype(np.float32),
            (head_match + 0 * g).reshape(A_HEADS, 128 * A_HEADS).astype(np.float32))


def _attn_sample_body(q_ref, k_ref, v_ref, ka0, ka1, ka2, ka3, kl_ref, va0, va1, va2, va3, vl_ref,
                      cl_ref, cn_ref, hm_ref, o_ref):
    nh = A_HEADS
    q = q_ref[...] * (1.0 / math.sqrt(HEAD_DIM))
    qs = q.astype(BF16)
    cl, cn, hm = cl_ref[...], cn_ref[...], hm_ref[...]

    def stride16(far_ref, last, res):
        near = last.reshape(32, 16, nh, HEAD_DIM)[:, res].reshape(32 * nh, HEAD_DIM)
        return jnp.concatenate([far_ref[...].reshape(SAMPLE_FAR_GROUPS * nh, HEAD_DIM).astype(BF16),
                                near.astype(BF16)], axis=0)

    k_last = kl_ref[...]
    kl = k_last.reshape(512 * nh, HEAD_DIM).astype(BF16)
    s_l = lax.dot_general(qs, kl, _NT, preferred_element_type=F32)
    s_n = lax.dot_general(qs, k_ref[...].astype(BF16), _NT, preferred_element_type=F32)
    s_a = []
    for res, ka in enumerate((ka0, ka1, ka2, ka3)):
        kk = stride16(ka, k_last, res)
        s_a.append(lax.dot_general(q[res * nh:(res + 1) * nh].astype(BF16), kk, _NT,
                                   preferred_element_type=F32))
    m = jnp.max(jnp.where(cl > 0, s_l, NEG), axis=1, keepdims=True)
    m = jnp.maximum(m, jnp.max(jnp.where(cn > 0, s_n, NEG), axis=1, keepdims=True))
    m_a = jnp.concatenate([jnp.max(jnp.where(hm > 0, s, NEG), axis=1, keepdims=True) for s in s_a], axis=0)
    m = jnp.maximum(m, m_a)

    p_l = cl * jnp.exp(jnp.minimum(s_l - m, 0.0))
    p_n = cn * jnp.exp(jnp.minimum(s_n - m, 0.0))
    den = jnp.sum(p_l, axis=1, keepdims=True) + jnp.sum(p_n, axis=1, keepdims=True)
    v_last = vl_ref[...]
    vl = v_last.reshape(512 * nh, HEAD_DIM).astype(BF16)
    out = jnp.dot(p_l.astype(BF16), vl, preferred_element_type=F32)
    out = out + jnp.dot(p_n.astype(BF16), v_ref[...].astype(BF16), preferred_element_type=F32)
    o_a, d_a = [], []
    for res, va in enumerate((va0, va1, va2, va3)):
        p_a = hm * jnp.exp(jnp.minimum(s_a[res] - m[res * nh:(res + 1) * nh], 0.0))
        d_a.append(jnp.sum(p_a, axis=1, keepdims=True))
        vv = stride16(va, v_last, res)
        o_a.append(jnp.dot(p_a.astype(BF16), vv, preferred_element_type=F32))
    out = out + jnp.concatenate(o_a, axis=0)
    den = den + jnp.concatenate(d_a, axis=0)
    o_ref[...] = out / den


def _attn_sample(q3, k3, v3, cache_k, cache_v, layer):
    db = q3.shape[0]
    depth, _, w_buf, nh, hd = cache_k.shape
    assert (w_buf, nh, hd) == (A_WINDOW_MAX, A_HEADS, HEAD_DIM) and q3.shape[1] == DEC_SEQ * A_HEADS
    cnt_l, cnt_n, head_match = _sample_attn_masks()
    rows = DEC_SEQ * A_HEADS

    def strided_specs():
        return [pl.BlockSpec((None, None, SAMPLE_FAR_GROUPS, None, nh, hd), functools.partial(
            lambda b, res: (layer, b, 0, res, 0, 0), res=res)) for res in range(DEC_SEQ)]

    last_spec = pl.BlockSpec((None, None, None, 512, nh, hd), lambda b: (layer, b, 3, 0, 0, 0))
    tok_spec = pl.BlockSpec((None, rows, hd), lambda b: (b, 0, 0))

    def whole(a):
        return pl.BlockSpec(a.shape, lambda b: (0, 0))

    k16 = cache_k.reshape(depth, db, 128, 16, nh, hd)
    v16 = cache_v.reshape(depth, db, 128, 16, nh, hd)
    k512 = cache_k.reshape(depth, db, 4, 512, nh, hd)
    v512 = cache_v.reshape(depth, db, 4, 512, nh, hd)
    return pl.pallas_call(
        _attn_sample_body,
        grid=(db,),
        in_specs=[tok_spec, tok_spec, tok_spec] + strided_specs() + [last_spec] + strided_specs() + [last_spec]
        + [whole(cnt_l), whole(cnt_n), whole(head_match)],
        out_specs=tok_spec,
        out_shape=jax.ShapeDtypeStruct((db, rows, hd), F32),
        compiler_params=_params(("arbitrary",), 48),
        name="attn_sample",
    )(q3, k3, v3, k16, k16, k16, k16, k512, v16, v16, v16, v16, v512,
      jnp.asarray(cnt_l), jnp.asarray(cnt_n), jnp.asarray(head_match))


def _gmlp_prompt_body(bu_ref, bv_ref, gg_ref, ws_ref, bs_ref, o_ref, *, chunks):
    ri = lax.broadcasted_iota(jnp.int32, (CHUNK, CHUNK), 0)
    ci = lax.broadcasted_iota(jnp.int32, (CHUNK, CHUNK), 1)
    wm = jnp.where(ci <= ri, ws_ref[0], 0.0).astype(BF16)
    bias = bs_ref[0]
    gain = gg_ref[...]
    for c in range(chunks):
        sl = slice(c * CHUNK, (c + 1) * CHUNK)
        gv = _rms(_gelu(bv_ref[sl, :])) * gain
        mix = jnp.dot(wm, gv.astype(BF16), preferred_element_type=F32) + bias
        o_ref[sl, :] = _gelu(bu_ref[sl, :]) * mix


def _gmlp_prompt(p, gain, ws, bs, chunks=8):
    seq = p.shape[0]
    tm = chunks * CHUNK
    return pl.pallas_call(
        functools.partial(_gmlp_prompt_body, chunks=chunks),
        grid=(seq // tm, B_GROUPS),
        in_specs=[
            pl.BlockSpec((tm, LANE), lambda i, g: (i, COL_BU + g)),
            pl.BlockSpec((tm, LANE), lambda i, g: (i, COL_BV + g)),
            pl.BlockSpec((1, LANE), lambda i, g: (0, g)),
            pl.BlockSpec((1, CHUNK, CHUNK), lambda i, g: (g, 0, 0)),
            pl.BlockSpec((1, CHUNK, 1), lambda i, g: (g, 0, 0)),
        ],
        out_specs=pl.BlockSpec((tm, LANE), lambda i, g: (i, g)),
        out_shape=jax.ShapeDtypeStruct((seq, B_WIDTH), F32),
        compiler_params=_params(("arbitrary", "arbitrary"), 32),
        name="gmlp_prompt",
    )(p, p, gain, ws, bs.reshape(B_GROUPS, CHUNK, 1))


def _gmlp_sample_body(ws_ref, bs_ref, bu_ref, bv_ref, gg_ref, o_ref, gv_ref, *, db):
    g = pl.program_id(0)
    gv = _rms(_gelu(bv_ref[...])) * gg_ref[...]
    gv_ref[...] = gv
    gu = _gelu(bu_ref[...])
    for i in range(DEC_SEQ):
        mix = jnp.full((db, LANE), bs_ref[g, i], F32)
        for j in range(i + 1):
            mix = mix + ws_ref[g, i * DEC_SEQ + j] * gv[j * db:(j + 1) * db]
        o_ref[i * db:(i + 1) * db, :] = gu[i * db:(i + 1) * db] * mix


def _gmlp_sample(p, gain, ws, bs):
    t = p.shape[0]
    db = t // DEC_SEQ
    ws4 = ws[:, :DEC_SEQ, :DEC_SEQ].reshape(B_GROUPS, DEC_SEQ * DEC_SEQ)
    bs4 = bs[:, :DEC_SEQ]
    smem = pl.BlockSpec(memory_space=pltpu.SMEM)
    return pl.pallas_call(
        functools.partial(_gmlp_sample_body, db=db),
        grid=(B_GROUPS,),
        in_specs=[
            smem, smem,
            pl.BlockSpec((t, LANE), lambda g: (0, COL_BU + g)),
            pl.BlockSpec((t, LANE), lambda g: (0, COL_BV + g)),
            pl.BlockSpec((1, LANE), lambda g: (0, g)),
        ],
        out_specs=[pl.BlockSpec((t, LANE), lambda g: (0, g))] * 2,
        out_shape=[jax.ShapeDtypeStruct((t, B_WIDTH), F32)] * 2,
        compiler_params=_params(("arbitrary",), 32),
        name="gmlp_sample",
    )(ws4, bs4, p, p, gain)


def _ssm_prep_body(lre_ref, lim_ref, ldt_ref, bre_ref, bim_ref, pwre_ref, pwim_ref, bbre_ref, bbim_ref):
    lre, lim = lre_ref[...], lim_ref[...]
    dt = jnp.exp(ldt_ref[...])
    k = lax.broadcasted_iota(jnp.int32, pwre_ref.shape, 0).astype(F32) + 1.0
    mag = jnp.exp(k * (dt * lre))
    ang = k * (dt * lim)
    pwre = mag * jnp.cos(ang)
    pwim = mag * jnp.sin(ang)
    pwre_ref[...] = pwre
    pwim_ref[...] = pwim
    xr, xi = pwre[0:1] - 1.0, pwim[0:1]
    den = lre * lre + lim * lim
    cr = (xr * lre + xi * lim) / den
    ci = (xi * lre - xr * lim) / den
    bre, bim = bre_ref[...], bim_ref[...]
    bbre_ref[...] = cr * bre - ci * bim
    bbim_ref[...] = cr * bim + ci * bre


def _ssm_prep(a_re, a_im, log_dt, b_re, b_im):
    n = C_GROUPS * SSM_P
    row = lambda a: a.reshape(1, n)
    bt = lambda b: jnp.transpose(b, (2, 0, 1)).reshape(C_GROUP_W, n)
    shapes = [jax.ShapeDtypeStruct((SSM_R, n), F32)] * 2 + [jax.ShapeDtypeStruct((C_GROUP_W, n), F32)] * 2
    pwre, pwim, bbre, bbim = pl.pallas_call(
        _ssm_prep_body, out_shape=shapes, name="ssm_prep",
    )(row(a_re), row(a_im), row(jnp.repeat(log_dt, SSM_P)), bt(b_re), bt(b_im))
    lg, st = SSM_LANE_GROUPS, SSM_LG_STATES
    split = lambda t: jnp.transpose(t.reshape(t.shape[0], lg, st), (1, 0, 2))
    pw = jnp.concatenate([split(pwre), split(pwim)], axis=-1)
    eye = jnp.eye(C_GROUPS // lg, dtype=F32)

    def blockdiag(bb):
        r = bb.reshape(C_GROUP_W, lg, C_GROUPS // lg, SSM_P)
        return jnp.einsum('ab,mlbp->lambp', eye, r).reshape(lg, LANE, st)

    bdb = jnp.concatenate([blockdiag(bbre), blockdiag(bbim)], axis=-1).astype(BF16)
    return pw, bdb


def _ssm_out_matrix(c):
    lg = SSM_LANE_GROUPS
    eye = jnp.eye(C_GROUPS // lg, dtype=F32)
    r = c.reshape(lg, C_GROUPS // lg, C_GROUP_W, SSM_P)
    return jnp.einsum('ab,lanp->lapbn', eye, r).reshape(lg, SSM_LG_STATES, LANE).astype(BF16)


def _cmul_add(are, aim, hre, him, xre, xim):
    return are * hre - aim * him + xre, are * him + aim * hre + xim


def _s5_prompt_body(u_ref, bdb_ref, cre_ref, cim_ref, d_ref, pw_ref, yg_ref, hl_ref,
                    u3_ref, h_ref, car_ref, hc_ref):
    r_steps, nc, st = SSM_R, SSM_NC, SSM_LG_STATES
    re, im = slice(0, st), slice(st, 2 * st)

    @pl.when(pl.program_id(1) == 0)
    def _():
        hc_ref[...] = jnp.zeros(hc_ref.shape, F32)

    def rows(r):
        return pl.ds(pl.multiple_of(r * nc, nc), nc)

    def regroup(r, c):
        u3_ref[rows(r), :] = u_ref[pl.ds(r, nc, stride=r_steps), :]
        return c

    lax.fori_loop(0, r_steps, regroup, 0)
    h_ref[...] = jnp.dot(u3_ref[...].astype(BF16), bdb_ref[0], preferred_element_type=F32)

    half = st // 2
    for hf in range(2):
        cre_ = slice(hf * half, (hf + 1) * half)
        cim_ = slice(st + hf * half, st + (hf + 1) * half)
        are = jnp.broadcast_to(pw_ref[0, 0:1, cre_], (nc, half))
        aim = jnp.broadcast_to(pw_ref[0, 0:1, cim_], (nc, half))

        def step(r, carry, cre_=cre_, cim_=cim_, are=are, aim=aim):
            hre, him = _cmul_add(are, aim, carry[0], carry[1], h_ref[rows(r), cre_], h_ref[rows(r), cim_])
            h_ref[rows(r), cre_] = hre
            h_ref[rows(r), cim_] = him
            return hre, him

        lax.fori_loop(1, r_steps, step, (h_ref[0:nc, cre_], h_ref[0:nc, cim_]))

    ends = h_ref[(r_steps - 1) * nc:r_steps * nc, :]
    are, aim = pw_ref[0, r_steps - 1:r_steps, re], pw_ref[0, r_steps - 1:r_steps, im]
    cre, cim = hc_ref[0:1, re], hc_ref[0:1, im]
    for c in range(nc):
        car_ref[c:c + 1, re] = cre
        car_ref[c:c + 1, im] = cim
        cre, cim = _cmul_add(are, aim, cre, cim, ends[c:c + 1, re], ends[c:c + 1, im])
    hc_ref[:, re] = jnp.broadcast_to(cre, (8, st))
    hc_ref[:, im] = jnp.broadcast_to(cim, (8, st))
    hl_ref[0] = hc_ref[...]

    def fix(r, c):
        pre, pim = pw_ref[0, pl.ds(r, 1), re], pw_ref[0, pl.ds(r, 1), im]
        hre, him = _cmul_add(pre, pim, car_ref[:, re], car_ref[:, im], h_ref[rows(r), re], h_ref[rows(r), im])
        h_ref[rows(r), re] = hre
        h_ref[rows(r), im] = him
        return c

    lax.fori_loop(0, r_steps, fix, 0)

    y = (jnp.dot(h_ref[:, re].astype(BF16), cre_ref[0], preferred_element_type=F32)
         - jnp.dot(h_ref[:, im].astype(BF16), cim_ref[0], preferred_element_type=F32)
         + d_ref[...] * u3_ref[...])
    u3_ref[...] = _gelu(y)

    def ungroup(r, c):
        yg_ref[pl.ds(r, nc, stride=r_steps), :] = u3_ref[rows(r), :]
        return c

    lax.fori_loop(0, r_steps, ungroup, 0)


def _s5_prompt(p, bdb, cre, cim, d, pw):
    seq = p.shape[0]
    tseg = SSM_R * SSM_NC
    st2 = 2 * SSM_LG_STATES
    lgs = SSM_LANE_GROUPS
    return pl.pallas_call(
        _s5_prompt_body,
        grid=(lgs, seq // tseg),
        in_specs=[
            pl.BlockSpec((tseg, LANE), lambda g, t: (t, COL_CU + g)),
            pl.BlockSpec((1, LANE, st2), lambda g, t: (g, 0, 0)),
            pl.BlockSpec((1, SSM_LG_STATES, LANE), lambda g, t: (g, 0, 0)),
            pl.BlockSpec((1, SSM_LG_STATES, LANE), lambda g, t: (g, 0, 0)),
            pl.BlockSpec((1, LANE), lambda g, t: (0, g)),
            pl.BlockSpec((1, SSM_R, st2), lambda g, t: (g, 0, 0)),
        ],
        out_specs=[
            pl.BlockSpec((tseg, LANE), lambda g, t: (t, g)),
            pl.BlockSpec((1, 8, st2), lambda g, t: (g, 0, 0)),
        ],
        out_shape=[jax.ShapeDtypeStruct((seq, C_WIDTH), F32), jax.ShapeDtypeStruct((lgs, 8, st2), F32)],
        scratch_shapes=[
            pltpu.VMEM((tseg, LANE), F32),
            pltpu.VMEM((tseg, st2), F32),
            pltpu.VMEM((SSM_NC, st2), F32),
            pltpu.VMEM((8, st2), F32),
        ],
        compiler_params=_params(("arbitrary", "arbitrary"), 40),
        name="s5_prompt",
    )(p, bdb, cre, cim, d, pw)


def _s5_sample_body(u_ref, hre_ref, him_ref, bdb_ref, cre_ref, cim_ref, d_ref, pw_ref,
                    yg_ref, ore_ref, oim_ref, h_ref, *, db):
    st = SSM_LG_STATES
    re, im = slice(0, st), slice(st, 2 * st)
    u = u_ref[...]
    x = jnp.dot(u.astype(BF16), bdb_ref[0], preferred_element_type=F32)
    are, aim = pw_ref[0, 0:1, re], pw_ref[0, 0:1, im]
    hre, him = hre_ref[...], him_ref[...]
    for j in range(DEC_SEQ):
        rj = slice(j * db, (j + 1) * db)
        hre, him = _cmul_add(are, aim, hre, him, x[rj, re], x[rj, im])
        h_ref[rj, re] = hre
        h_ref[rj, im] = him
    ore_ref[...] = hre
    oim_ref[...] = him
    y = (jnp.dot(h_ref[:, re].astype(BF16), cre_ref[0], preferred_element_type=F32)
         - jnp.dot(h_ref[:, im].astype(BF16), cim_ref[0], preferred_element_type=F32)
         + d_ref[...] * u)
    yg_ref[...] = _gelu(y)


def _s5_sample(p, h0re, h0im, bdb, cre, cim, d, pw):
    t = p.shape[0]
    db = t // DEC_SEQ
    st = SSM_LG_STATES
    lgs = SSM_LANE_GROUPS
    return pl.pallas_call(
        functools.partial(_s5_sample_body, db=db),
        grid=(lgs,),
        in_specs=[
            pl.BlockSpec((t, LANE), lambda g: (0, COL_CU + g)),
            pl.BlockSpec((db, st), lambda g: (0, g)),
            pl.BlockSpec((db, st), lambda g: (0, g)),
            pl.BlockSpec((1, LANE, 2 * st), lambda g: (g, 0, 0)),
            pl.BlockSpec((1, st, LANE), lambda g: (g, 0, 0)),
            pl.BlockSpec((1, st, LANE), lambda g: (g, 0, 0)),
            pl.BlockSpec((1, LANE), lambda g: (0, g)),
            pl.BlockSpec((1, SSM_R, 2 * st), lambda g: (g, 0, 0)),
        ],
        out_specs=[
            pl.BlockSpec((t, LANE), lambda g: (0, g)),
            pl.BlockSpec((db, st), lambda g: (0, g)),
            pl.BlockSpec((db, st), lambda g: (0, g)),
        ],
        out_shape=[jax.ShapeDtypeStruct((t, C_WIDTH), F32),
                   jax.ShapeDtypeStruct((db, lgs * st), F32), jax.ShapeDtypeStruct((db, lgs * st), F32)],
        scratch_shapes=[pltpu.VMEM((t, 2 * st), F32)],
        compiler_params=_params(("arbitrary",), 32),
        name="s5_sample",
    )(p, h0re, h0im, bdb, cre, cim, d, pw)


def _mix_out_body(a_ref, b_ref, y_ref, x_ref, wg_ref, ga_ref, gb_ref, gc_ref, wo_ref, o_ref):
    glu = jnp.dot(y_ref[...].astype(BF16), wg_ref[...], preferred_element_type=F32)
    c = glu[:, :C_WIDTH] * jax.nn.sigmoid(glu[:, C_WIDTH:])
    mixed = jnp.concatenate([
        (_rms(a_ref[...]) * ga_ref[...]).astype(BF16),
        (_rms(b_ref[...]) * gb_ref[...]).astype(BF16),
        (_rms(c) * gc_ref[...]).astype(BF16)], axis=1)
    o_ref[...] = x_ref[...] + jnp.dot(mixed, wo_ref[...], preferred_element_type=F32)


def _mix_out(a, b, y, x, w_glu, g_a, g_b, g_c, w_out, tm=512):
    t = x.shape[0]
    row = lambda w: pl.BlockSpec((tm, w), lambda i: (i, 0))
    whole = lambda arr: pl.BlockSpec(arr.shape, lambda i: (0, 0))
    return pl.pallas_call(
        _mix_out_body,
        grid=(t // tm,),
        in_specs=[row(A_WIDTH), row(B_WIDTH), row(C_WIDTH), row(D_MODEL),
                  whole(w_glu), whole(g_a), whole(g_b), whole(g_c), whole(w_out)],
        out_specs=row(D_MODEL),
        out_shape=jax.ShapeDtypeStruct((t, D_MODEL), F32),
        compiler_params=_params(("arbitrary",), 56),
        name="mix_out",
    )(a, b, y, x, w_glu, g_a, g_b, g_c, w_out)


FFN_HALO = 16


def _conv_gate(ug, uv, cwg_ref, cwv_ref, cbg_ref, cbv_ref, taps):
    def conv(u, cw_ref, cb_ref):
        return ((cb_ref[...] + cw_ref[0:1, :] * taps[0](u)) + cw_ref[1:2, :] * taps[1](u)) + cw_ref[2:3, :] * taps[2](u)
    gate = conv(ug, cwg_ref, cbg_ref)
    return (gate * jax.nn.sigmoid(gate)) * conv(uv, cwv_ref, cbv_ref)


def _finish(o_ref, gf_ref, is_last_tile, final_norm):
    if final_norm:
        @pl.when(is_last_tile)
        def _():
            o_ref[...] = _rms(o_ref[...]) * gf_ref[...]


def _ffn_prompt_body(x_ref, halo_ref, g_ref, wg_ref, wv_ref, cwg_ref, cwv_ref, cbg_ref, cbv_ref, wd_ref, gf_ref,
                     o_ref, tail_ref, h_ref, *, tm, final_norm):
    i, f = pl.program_id(0), pl.program_id(1)
    hl = FFN_HALO

    @pl.when(f == 0)
    def _():
        h_ref[hl:, :] = (_rms(x_ref[...]) * g_ref[...]).astype(BF16)
        prev = _rms(halo_ref[...]) * g_ref[...]
        h_ref[0:hl, :] = jnp.where(i > 0, prev, 0.0).astype(BF16)
        o_ref[...] = x_ref[...]

    h = h_ref[...]
    ug = jnp.dot(h, wg_ref[...], preferred_element_type=F32)
    uv = jnp.dot(h, wv_ref[...], preferred_element_type=F32)
    taps = (lambda u: u[hl - 2:hl - 2 + tm], lambda u: u[hl - 1:hl - 1 + tm], lambda u: u[hl:])
    act = _conv_gate(ug, uv, cwg_ref, cwv_ref, cbg_ref, cbv_ref, taps)
    o_ref[...] += jnp.dot(act.astype(BF16), wd_ref[...], preferred_element_type=F32)

    tail_ref[0] = ug[hl + tm - 8:]
    tail_ref[1] = uv[hl + tm - 8:]
    _finish(o_ref, gf_ref, f == pl.num_programs(1) - 1, final_norm)


def _ffn_prompt(x, g, w_up, conv_w, conv_b, w_down, g_final, final_norm, tm=512, tf=512):
    t = x.shape[0]
    nf = D_FF // tf
    hl = FFN_HALO
    cb = conv_b.reshape(1, 2 * D_FF)
    return pl.pallas_call(
        functools.partial(_ffn_prompt_body, tm=tm, final_norm=final_norm),
        grid=(t // tm, nf),
        in_specs=[
            pl.BlockSpec((tm, D_MODEL), lambda i, f: (i, 0)),
            pl.BlockSpec((hl, D_MODEL), lambda i, f: (jnp.maximum(i * (tm // hl) - 1, 0), 0)),
            pl.BlockSpec((1, D_MODEL), lambda i, f: (0, 0)),
            pl.BlockSpec((D_MODEL, tf), lambda i, f: (0, f)),
            pl.BlockSpec((D_MODEL, tf), lambda i, f: (0, nf + f)),
            pl.BlockSpec((3, tf), lambda i, f: (0, f)),
            pl.BlockSpec((3, tf), lambda i, f: (0, nf + f)),
            pl.BlockSpec((1, tf), lambda i, f: (0, f)),
            pl.BlockSpec((1, tf), lambda i, f: (0, nf + f)),
            pl.BlockSpec((tf, D_MODEL), lambda i, f: (f, 0)),
            pl.BlockSpec((1, D_MODEL), lambda i, f: (0, 0)),
        ],
        out_specs=[
            pl.BlockSpec((tm, D_MODEL), lambda i, f: (i, 0), pipeline_mode=pl.Buffered(1)),
            pl.BlockSpec((None, 2, 8, tf), lambda i, f: (i, 0, 0, f)),
        ],
        out_shape=[jax.ShapeDtypeStruct((t, D_MODEL), F32), jax.ShapeDtypeStruct((t // tm, 2, 8, D_FF), F32)],
        scratch_shapes=[pltpu.VMEM((hl + tm, D_MODEL), BF16)],
        compiler_params=_params(("arbitrary", "arbitrary"), 56),
        name="ffn_prompt",
    )(x, x, g, w_up, w_up, conv_w, conv_w, cb, cb, w_down, g_final)


def _ffn_sample_body(x_ref, g_ref, sg_ref, sv_ref, wg_ref, wv_ref, cwg_ref, cwv_ref, cbg_ref, cbv_ref, wd_ref, gf_ref,
                     o_ref, tail_ref, h_ref, *, db, final_norm):
    f = pl.program_id(0)
    t = DEC_SEQ * db

    @pl.when(f == 0)
    def _():
        h_ref[...] = (_rms(x_ref[...]) * g_ref[...]).astype(BF16)
        o_ref[...] = x_ref[...]

    h = h_ref[...]
    ug = jnp.dot(h, wg_ref[...], preferred_element_type=F32)
    uv = jnp.dot(h, wv_ref[...], preferred_element_type=F32)
    pg = jnp.concatenate([sg_ref[0], sg_ref[1], ug], axis=0)
    pv = jnp.concatenate([sv_ref[0], sv_ref[1], uv], axis=0)
    taps = (lambda u: u[0:t], lambda u: u[db:db + t], lambda u: u[2 * db:2 * db + t])
    act = _conv_gate(pg, pv, cwg_ref, cwv_ref, cbg_ref, cbv_ref, taps)
    o_ref[...] += jnp.dot(act.astype(BF16), wd_ref[...], preferred_element_type=F32)

    tail_ref[0, 0] = ug[(DEC_SEQ - 2) * db:(DEC_SEQ - 1) * db]
    tail_ref[0, 1] = ug[(DEC_SEQ - 1) * db:]
    tail_ref[1, 0] = uv[(DEC_SEQ - 2) * db:(DEC_SEQ - 1) * db]
    tail_ref[1, 1] = uv[(DEC_SEQ - 1) * db:]
    _finish(o_ref, gf_ref, f == pl.num_programs(0) - 1, final_norm)


def _ffn_sample(x, g, state, w_up, conv_w, conv_b, w_down, g_final, final_norm, tf=512):
    t = x.shape[0]
    db = t // DEC_SEQ
    nf = D_FF // tf
    cb = conv_b.reshape(1, 2 * D_FF)
    return pl.pallas_call(
        functools.partial(_ffn_sample_body, db=db, final_norm=final_norm),
        grid=(nf,),
        in_specs=[
            pl.BlockSpec((t, D_MODEL), lambda f: (0, 0)),
            pl.BlockSpec((1, D_MODEL), lambda f: (0, 0)),
            pl.BlockSpec((2, db, tf), lambda f: (0, 0, f)),
            pl.BlockSpec((2, db, tf), lambda f: (0, 0, nf + f)),
            pl.BlockSpec((D_MODEL, tf), lambda f: (0, f)),
            pl.BlockSpec((D_MODEL, tf), lambda f: (0, nf + f)),
            pl.BlockSpec((3, tf), lambda f: (0, f)),
            pl.BlockSpec((3, tf), lambda f: (0, nf + f)),
            pl.BlockSpec((1, tf), lambda f: (0, f)),
            pl.BlockSpec((1, tf), lambda f: (0, nf + f)),
            pl.BlockSpec((tf, D_MODEL), lambda f: (f, 0)),
            pl.BlockSpec((1, D_MODEL), lambda f: (0, 0)),
        ],
        out_specs=[
            pl.BlockSpec((t, D_MODEL), lambda f: (0, 0)),
            pl.BlockSpec((2, 2, db, tf), lambda f: (0, 0, 0, f)),
        ],
        out_shape=[jax.ShapeDtypeStruct((t, D_MODEL), F32), jax.ShapeDtypeStruct((2, 2, db, D_FF), F32)],
        scratch_shapes=[pltpu.VMEM((t, D_MODEL), BF16)],
        compiler_params=_params(("arbitrary",), 48),
        name="ffn_sample",
    )(x, g, state, state, w_up, w_up, conv_w, conv_w, cb, cb, w_down, g_final)


def kernel(x_prompt, x_sample, cache_win_k, cache_win_v, state_ssm_re, state_ssm_im, state_ffn_conv,
           g_mix, w_in, g_out_a, g_out_b, g_out_c, gmlp_gv, gmlp_ws, gmlp_bs, ssm_a_re, ssm_a_im,
           ssm_log_dt, ssm_b_re, ssm_b_im, ssm_c_re, ssm_c_im, ssm_d, ssm_w_glu, w_out, g_ffn, w_up,
           conv_w, conv_b, w_down, g_final):
    depth = w_in.shape[0]
    bsz, seq, _ = x_prompt.shape
    db, ds, _ = x_sample.shape
    assert bsz == 1 and ds == DEC_SEQ and seq % (SSM_R * SSM_NC) == 0 and seq >= A_WINDOW_MAX
    n_states = C_GROUPS * SSM_P
    row = lambda v: v.reshape(1, -1)

    g_fin = row(g_final)
    xp = x_prompt.reshape(seq, D_MODEL)
    xs = jnp.transpose(x_sample, (1, 0, 2)).reshape(ds * db, D_MODEL)
    outs = {k: [] for k in ("kp", "vp", "ks", "vs", "gv", "rp", "ip", "rs", "is", "cp", "cs")}

    for l in range(depth):
        pw, bdb = _ssm_prep(ssm_a_re[l], ssm_a_im[l], ssm_log_dt[l], ssm_b_re[l], ssm_b_im[l])
        c_re, c_im = _ssm_out_matrix(ssm_c_re[l]), _ssm_out_matrix(ssm_c_im[l])
        d_row = row(ssm_d[l])
        gains = (row(g_out_a[l]), row(g_out_b[l]), row(g_out_c[l]))
        w_in_b, w_glu_b, w_out_b = w_in[l].astype(BF16), ssm_w_glu[l].astype(BF16), w_out[l].astype(BF16)
        w_up_b, w_down_b = w_up[l].astype(BF16), w_down[l].astype(BF16)
        last = l == depth - 1

        p = _proj(xp, row(g_mix[l]), w_in_b, tm=min(1024, seq))
        a = _attn_prompt(p)
        b = _gmlp_prompt(p, row(gmlp_gv[l]), gmlp_ws[l], gmlp_bs[l])
        yg, h_last = _s5_prompt(p, bdb, c_re, c_im, d_row, pw)
        xm = _mix_out(a, b, yg, xp, w_glu_b, *gains, w_out_b)
        xp, tail = _ffn_prompt(xm, row(g_ffn[l]), w_up_b, conv_w[l], conv_b[l], w_down_b, g_fin,
                               final_norm=last, tm=min(1024, seq))
        keep = min(A_WINDOW_MAX, seq)
        outs["kp"].append(p[seq - keep:, A_WIDTH:2 * A_WIDTH].reshape(1, keep, A_HEADS, HEAD_DIM))
        outs["vp"].append(p[seq - keep:, 2 * A_WIDTH:3 * A_WIDTH].reshape(1, keep, A_HEADS, HEAD_DIM))
        st = SSM_LG_STATES
        outs["rp"].append(h_last[:, 0, :st].reshape(1, C_GROUPS, SSM_P))
        outs["ip"].append(h_last[:, 0, st:].reshape(1, C_GROUPS, SSM_P))
        outs["cp"].append(jnp.transpose(tail[-1, :, 6:8, :], (1, 0, 2)).reshape(1, 2, 2 * D_FF))

        ps = _proj(xs, row(g_mix[l]), w_in_b, tm=ds * db)

        def heads(cols):
            return jnp.transpose(cols.reshape(ds, db, A_HEADS, HEAD_DIM), (1, 0, 2, 3)).reshape(
                db, ds * A_HEADS, HEAD_DIM)

        q3, k3, v3 = (heads(ps[:, i * A_WIDTH:(i + 1) * A_WIDTH]) for i in range(3))
        a3 = _attn_sample(q3, k3, v3, cache_win_k, cache_win_v, l)
        a_s = jnp.transpose(a3.reshape(db, ds, A_WIDTH), (1, 0, 2)).reshape(ds * db, A_WIDTH)
        b_s, gv_s = _gmlp_sample(ps, row(gmlp_gv[l]), gmlp_ws[l], gmlp_bs[l])
        yg_s, hre, him = _s5_sample(ps, state_ssm_re[l].reshape(db, n_states), state_ssm_im[l].reshape(db, n_states),
                                    bdb, c_re, c_im, d_row, pw)
        xm_s = _mix_out(a_s, b_s, yg_s, xs, w_glu_b, *gains, w_out_b, tm=ds * db)
        xs, tail_s = _ffn_sample(xm_s, row(g_ffn[l]), jnp.transpose(state_ffn_conv[l], (1, 0, 2)),
                                 w_up_b, conv_w[l], conv_b[l], w_down_b, g_fin, final_norm=last)
        outs["ks"].append(k3.reshape(db, ds, A_HEADS, HEAD_DIM))
        outs["vs"].append(v3.reshape(db, ds, A_HEADS, HEAD_DIM))
        outs["gv"].append(jnp.transpose(gv_s.reshape(ds, db, B_WIDTH), (1, 0, 2)))
        outs["rs"].append(hre.reshape(db, C_GROUPS, SSM_P))
        outs["is"].append(him.reshape(db, C_GROUPS, SSM_P))
        outs["cs"].append(jnp.transpose(tail_s, (2, 1, 0, 3)).reshape(db, 2, 2 * D_FF))

    y_p = xp.reshape(1, seq, D_MODEL)
    y_s = jnp.transpose(xs.reshape(ds, db, D_MODEL), (1, 0, 2))
    st = lambda k: jnp.stack(outs[k])
    return (y_p, y_s, st("kp"), st("vp"), st("ks"), st("vs"), st("gv"),
            st("rp"), st("ip"), st("rs"), st("is"), st("cp"), st("cs"))
```

```python
import functools
import math

import numpy as np
import jax
import jax.numpy as jnp
from jax import lax
from jax.experimental import pallas as pl
from jax.experimental.pallas import tpu as pltpu

F32 = jnp.float32
BF16 = jnp.bfloat16

D_MODEL = 2048
A_WIDTH = 1024
A_HEADS = 8
HEAD_DIM = 128
A_DILATIONS = (1, 4, 16)
A_SPAN = 128
A_WINDOW_MAX = 2048
SAMPLE_TAIL_ROWS = 512
SAMPLE_FAR_QUARTERS = A_WINDOW_MAX // SAMPLE_TAIL_ROWS - 1
ATTN_UNROLL = 8
B_WIDTH = 512
B_GROUPS = 4
CHUNK = 128
C_WIDTH = 512
C_GROUPS = 32
C_GROUP_W = 16
SSM_P = 64
N_IN = 3 * A_WIDTH + 2 * B_WIDTH + C_WIDTH
D_FF = 5632
DEC_SEQ = 4
EPS = 1e-6
NEG = -1e30

LANE = 128
COL_Q, COL_K, COL_V = 0, A_WIDTH // LANE, 2 * A_WIDTH // LANE
COL_BU = 3 * A_WIDTH // LANE
COL_BV = COL_BU + B_WIDTH // LANE
COL_CU = COL_BV + B_WIDTH // LANE

SSM_LANE_GROUPS = 4
SSM_LG_STATES = C_GROUPS * SSM_P // SSM_LANE_GROUPS
SSM_R = 64
SSM_NC = 32

MIB = 1024 * 1024


def _params(sem, vmem_mib):
    return pltpu.CompilerParams(dimension_semantics=sem, vmem_limit_bytes=vmem_mib * MIB)


def _gelu(x):
    c = math.sqrt(2.0 / math.pi)
    return x * (0.5 * (1.0 + jnp.tanh(c * (x + 0.044715 * (x * x * x)))))


def _rms(x):
    return x * lax.rsqrt(jnp.mean(x * x, axis=-1, keepdims=True) + EPS)


_NT = (((1,), (1,)), ((), ()))


def _proj_body(x_ref, g_ref, w_ref, o_ref, h_ref):
    @pl.when(pl.program_id(1) == 0)
    def _():
        h_ref[...] = (_rms(x_ref[...]) * g_ref[...]).astype(BF16)

    o_ref[...] = jnp.dot(h_ref[...], w_ref[...], preferred_element_type=F32)


def _proj(x, g, w, layer, tm, tn=1536):
    t = x.shape[0]
    return pl.pallas_call(
        _proj_body,
        grid=(t // tm, N_IN // tn),
        in_specs=[
            pl.BlockSpec((tm, D_MODEL), lambda i, j: (i, 0)),
            pl.BlockSpec((1, D_MODEL), lambda i, j: (0, 0)),
            pl.BlockSpec((None, D_MODEL, tn), lambda i, j: (layer, 0, j)),
        ],
        out_specs=pl.BlockSpec((tm, tn), lambda i, j: (i, j)),
        out_shape=jax.ShapeDtypeStruct((t, N_IN), F32),
        scratch_shapes=[pltpu.VMEM((tm, D_MODEL), BF16)],
        compiler_params=_params(("arbitrary", "arbitrary"), 56),
        name="proj_in",
    )(x, g, w)


def _attn_prompt_body(q_ref, k_ref, v_ref, o_ref, acc_ref, m_ref, l_ref, *, seq):
    m_ref[...] = jnp.full(m_ref.shape, NEG, F32)
    l_ref[...] = jnp.zeros(l_ref.shape, F32)
    acc_ref[...] = jnp.zeros(acc_ref.shape, F32)
    scale = math.log2(math.e) / math.sqrt(HEAD_DIM)
    ii = lax.broadcasted_iota(jnp.int32, (A_SPAN, 2 * A_SPAN), 0)
    jj = lax.broadcasted_iota(jnp.int32, (A_SPAN, 2 * A_SPAN), 1)
    band = (jj >= ii) & (jj <= ii + A_SPAN)
    in_cur = jj >= A_SPAN

    for dil in A_DILATIONS:
        nb = seq // (A_SPAN * dil)

        def rows(start, dil=dil):
            if dil == 1:
                return pl.ds(pl.multiple_of(start, A_SPAN), A_SPAN)
            return pl.ds(start, A_SPAN, stride=dil)

        def blocks(pairs, dil=dil, rows=rows):
            work = []
            for n, r in pairs:
                base = r + n * (A_SPAN * dil)
                prev = r + jnp.maximum(n - 1, 0) * (A_SPAN * dil)
                q = (q_ref[rows(base), :] * scale).astype(BF16)
                k2 = jnp.concatenate([k_ref[rows(prev), :], k_ref[rows(base), :]], axis=0).astype(BF16)
                v2 = jnp.concatenate([v_ref[rows(prev), :], v_ref[rows(base), :]], axis=0).astype(BF16)
                s = lax.dot_general(q, k2, _NT, preferred_element_type=F32)
                s = jnp.where(band & (in_cur | (n > 0)), s, NEG)
                m_old = m_ref[rows(base), :]
                m_new = jnp.maximum(m_old, jnp.max(s, axis=1, keepdims=True))
                alpha = jnp.exp2(m_old - m_new)
                p = jnp.exp2(s - jnp.concatenate([m_new, m_new], axis=1))
                l_new = alpha * l_ref[rows(base), :] + jnp.sum(p, axis=1, keepdims=True)
                acc_new = alpha * acc_ref[rows(base), :] + jnp.dot(p.astype(BF16), v2, preferred_element_type=F32)
                work.append((base, m_new, l_new, acc_new))
            for base, m_new, l_new, acc_new in work:
                m_ref[rows(base), :] = m_new
                l_ref[rows(base), :] = l_new
                acc_ref[rows(base), :] = acc_new

        unroll = min(ATTN_UNROLL, nb * dil)
        if unroll <= nb:
            per_res = nb // unroll

            def trip(i, c, blocks=blocks, unroll=unroll, per_res=per_res):
                r, first = i // per_res, (i % per_res) * unroll
                blocks([(first + u, r) for u in range(unroll)])
                return c
        else:
            res_per_trip = unroll // nb

            def trip(i, c, blocks=blocks, unroll=unroll, nb=nb, res_per_trip=res_per_trip):
                blocks([(u % nb, i * res_per_trip + u // nb) for u in range(unroll)])
                return c

        lax.fori_loop(0, nb * dil // unroll, trip, 0)

    o_ref[...] = acc_ref[...] / l_ref[...]


def _attn_prompt(p):
    seq = p.shape[0]
    blk = (seq, HEAD_DIM)
    return pl.pallas_call(
        functools.partial(_attn_prompt_body, seq=seq),
        grid=(A_HEADS,),
        in_specs=[
            pl.BlockSpec(blk, lambda h: (0, COL_Q + h)),
            pl.BlockSpec(blk, lambda h: (0, COL_K + h)),
            pl.BlockSpec(blk, lambda h: (0, COL_V + h)),
        ],
        out_specs=pl.BlockSpec(blk, lambda h: (0, h)),
        out_shape=jax.ShapeDtypeStruct((seq, A_WIDTH), F32),
        scratch_shapes=[pltpu.VMEM(blk, F32)] * 3,
        compiler_params=_params(("arbitrary",), 56),
        name="attn_prompt",
    )(p, p, p)


def _sample_attn_masks():
    j = np.arange(DEC_SEQ)[:, None, None, None]
    h = np.arange(A_HEADS)[None, :, None, None]
    hk = np.arange(A_HEADS)[None, None, None, :]
    same = (h == hk)
    i = np.arange(512)[None, None, :, None]
    cnt_l = same * ((i >= 384 + j).astype(np.int32) + ((i - j) % 4 == 0).astype(np.int32))
    jn = np.arange(DEC_SEQ)[None, None, :, None]
    cnt_n = same * ((jn <= j).astype(np.int32) + 2 * (jn == j).astype(np.int32))
    g = np.arange(128)[None, None, :, None]
    head_match = np.broadcast_to(same[0:1], (1, A_HEADS, 128, A_HEADS))
    return (cnt_l.reshape(DEC_SEQ * A_HEADS, 512 * A_HEADS).astype(np.float32),
            cnt_n.reshape(DEC_SEQ * A_HEADS, DEC_SEQ * A_HEADS).astype(np.float32),
            (head_match + 0 * g).reshape(A_HEADS, 128 * A_HEADS).astype(np.float32))


def _attn_sample_body(q_ref, k_ref, v_ref, *refs):
    nh = A_HEADS
    nfar = SAMPLE_FAR_QUARTERS * DEC_SEQ
    k_far, kl_ref = refs[:nfar], refs[nfar]
    v_far, vl_ref = refs[nfar + 1:2 * nfar + 1], refs[2 * nfar + 1]
    cl_ref, cn_ref, hm_ref, o_ref = refs[2 * nfar + 2:]
    q = q_ref[...] * (1.0 / math.sqrt(HEAD_DIM))
    qs = q.astype(BF16)
    cl, cn, hm = cl_ref[...], cn_ref[...], hm_ref[...]

    def stride16(far, last, res):
        near = last.reshape(32, 16, nh, HEAD_DIM)[:, res].reshape(32 * nh, HEAD_DIM)
        parts = [far[qt * DEC_SEQ + res][...].reshape(32 * nh, HEAD_DIM).astype(BF16)
                 for qt in range(SAMPLE_FAR_QUARTERS)]
        return jnp.concatenate(parts + [near.astype(BF16)], axis=0)

    k_last = kl_ref[...]
    kl = k_last.reshape(512 * nh, HEAD_DIM).astype(BF16)
    s_l = lax.dot_general(qs, kl, _NT, preferred_element_type=F32)
    s_n = lax.dot_general(qs, k_ref[...].astype(BF16), _NT, preferred_element_type=F32)
    s_a = []
    for res in range(DEC_SEQ):
        kk = stride16(k_far, k_last, res)
        s_a.append(lax.dot_general(q[res * nh:(res + 1) * nh].astype(BF16), kk, _NT,
                                   preferred_element_type=F32))
    m = jnp.max(jnp.where(cl > 0, s_l, NEG), axis=1, keepdims=True)
    m = jnp.maximum(m, jnp.max(jnp.where(cn > 0, s_n, NEG), axis=1, keepdims=True))
    m_a = jnp.concatenate([jnp.max(jnp.where(hm > 0, s, NEG), axis=1, keepdims=True) for s in s_a], axis=0)
    m = jnp.maximum(m, m_a)

    p_l = cl * jnp.exp(jnp.minimum(s_l - m, 0.0))
    p_n = cn * jnp.exp(jnp.minimum(s_n - m, 0.0))
    den = jnp.sum(p_l, axis=1, keepdims=True) + jnp.sum(p_n, axis=1, keepdims=True)
    v_last = vl_ref[...]
    vl = v_last.reshape(512 * nh, HEAD_DIM).astype(BF16)
    out = jnp.dot(p_l.astype(BF16), vl, preferred_element_type=F32)
    out = out + jnp.dot(p_n.astype(BF16), v_ref[...].astype(BF16), preferred_element_type=F32)
    o_a, d_a = [], []
    for res in range(DEC_SEQ):
        p_a = hm * jnp.exp(jnp.minimum(s_a[res] - m[res * nh:(res + 1) * nh], 0.0))
        d_a.append(jnp.sum(p_a, axis=1, keepdims=True))
        vv = stride16(v_far, v_last, res)
        o_a.append(jnp.dot(p_a.astype(BF16), vv, preferred_element_type=F32))
    out = out + jnp.concatenate(o_a, axis=0)
    den = den + jnp.concatenate(d_a, axis=0)
    o_ref[...] = out / den


def _attn_sample(q3, k3, v3, cache_k, cache_v, layer):
    db = q3.shape[0]
    depth, _, w_buf, nh, hd = cache_k.shape
    assert (w_buf, nh, hd) == (A_WINDOW_MAX, A_HEADS, HEAD_DIM) and q3.shape[1] == DEC_SEQ * A_HEADS
    cnt_l, cnt_n, head_match = _sample_attn_masks()
    rows = DEC_SEQ * A_HEADS

    nq = SAMPLE_FAR_QUARTERS

    def strided_specs():
        return [pl.BlockSpec((None, None, None, 32, None, nh, hd), functools.partial(
            lambda b, qt, res: (layer, b, qt, 0, res, 0, 0), qt=qt, res=res))
            for qt in range(nq) for res in range(DEC_SEQ)]

    last_spec = pl.BlockSpec((None, None, None, SAMPLE_TAIL_ROWS, nh, hd), lambda b: (layer, b, nq, 0, 0, 0))
    tok_spec = pl.BlockSpec((None, rows, hd), lambda b: (b, 0, 0))

    def whole(a):
        return pl.BlockSpec(a.shape, lambda b: (0, 0))

    k16 = cache_k.reshape(depth, db, nq + 1, 32, 16, nh, hd)
    v16 = cache_v.reshape(depth, db, nq + 1, 32, 16, nh, hd)
    k512 = cache_k.reshape(depth, db, nq + 1, SAMPLE_TAIL_ROWS, nh, hd)
    v512 = cache_v.reshape(depth, db, nq + 1, SAMPLE_TAIL_ROWS, nh, hd)
    nfar = nq * DEC_SEQ
    return pl.pallas_call(
        _attn_sample_body,
        grid=(db,),
        in_specs=[tok_spec, tok_spec, tok_spec] + strided_specs() + [last_spec] + strided_specs() + [last_spec]
        + [whole(cnt_l), whole(cnt_n), whole(head_match)],
        out_specs=tok_spec,
        out_shape=jax.ShapeDtypeStruct((db, rows, hd), F32),
        compiler_params=_params(("arbitrary",), 48),
        name="attn_sample",
    )(q3, k3, v3, *([k16] * nfar), k512, *([v16] * nfar), v512,
      jnp.asarray(cnt_l), jnp.asarray(cnt_n), jnp.asarray(head_match))


def _gmlp_prompt_body(bu_ref, bv_ref, gg_ref, ws_ref, bs_ref, o_ref, *, chunks):
    ri = lax.broadcasted_iota(jnp.int32, (CHUNK, CHUNK), 0)
    ci = lax.broadcasted_iota(jnp.int32, (CHUNK, CHUNK), 1)
    wm = jnp.where(ci <= ri, ws_ref[0], 0.0).astype(BF16)
    bias = bs_ref[0]
    gain = gg_ref[...]
    for c in range(chunks):
        sl = slice(c * CHUNK, (c + 1) * CHUNK)
        gv = _rms(_gelu(bv_ref[sl, :])) * gain
        mix = jnp.dot(wm, gv.astype(BF16), preferred_element_type=F32) + bias
        o_ref[sl, :] = _gelu(bu_ref[sl, :]) * mix


def _gmlp_prompt(p, gain, ws, bs, chunks=8):
    seq = p.shape[0]
    tm = chunks * CHUNK
    return pl.pallas_call(
        functools.partial(_gmlp_prompt_body, chunks=chunks),
        grid=(seq // tm, B_GROUPS),
        in_specs=[
            pl.BlockSpec((tm, LANE), lambda i, g: (i, COL_BU + g)),
            pl.BlockSpec((tm, LANE), lambda i, g: (i, COL_BV + g)),
            pl.BlockSpec((1, LANE), lambda i, g: (0, g)),
            pl.BlockSpec((1, CHUNK, CHUNK), lambda i, g: (g, 0, 0)),
            pl.BlockSpec((1, CHUNK, 1), lambda i, g: (g, 0, 0)),
        ],
        out_specs=pl.BlockSpec((tm, LANE), lambda i, g: (i, g)),
        out_shape=jax.ShapeDtypeStruct((seq, B_WIDTH), F32),
        compiler_params=_params(("arbitrary", "arbitrary"), 32),
        name="gmlp_prompt",
    )(p, p, gain, ws, bs.reshape(B_GROUPS, CHUNK, 1))


def _gmlp_sample_body(ws_ref, bs_ref, bu_ref, bv_ref, gg_ref, o_ref, gv_ref, *, db):
    g = pl.program_id(0)
    gv = _rms(_gelu(bv_ref[...])) * gg_ref[...]
    gv_ref[...] = gv
    gu = _gelu(bu_ref[...])
    for i in range(DEC_SEQ):
        mix = jnp.full((db, LANE), bs_ref[g, i], F32)
        for j in range(i + 1):
            mix = mix + ws_ref[g, i * DEC_SEQ + j] * gv[j * db:(j + 1) * db]
        o_ref[i * db:(i + 1) * db, :] = gu[i * db:(i + 1) * db] * mix


def _gmlp_sample(p, gain, ws, bs):
    t = p.shape[0]
    db = t // DEC_SEQ
    ws4 = ws[:, :DEC_SEQ, :DEC_SEQ].reshape(B_GROUPS, DEC_SEQ * DEC_SEQ)
    bs4 = bs[:, :DEC_SEQ]
    smem = pl.BlockSpec(memory_space=pltpu.SMEM)
    return pl.pallas_call(
        functools.partial(_gmlp_sample_body, db=db),
        grid=(B_GROUPS,),
        in_specs=[
            smem, smem,
            pl.BlockSpec((t, LANE), lambda g: (0, COL_BU + g)),
            pl.BlockSpec((t, LANE), lambda g: (0, COL_BV + g)),
            pl.BlockSpec((1, LANE), lambda g: (0, g)),
        ],
        out_specs=[pl.BlockSpec((t, LANE), lambda g: (0, g))] * 2,
        out_shape=[jax.ShapeDtypeStruct((t, B_WIDTH), F32)] * 2,
        compiler_params=_params(("arbitrary",), 32),
        name="gmlp_sample",
    )(ws4, bs4, p, p, gain)


def _ssm_prep_body(lre_ref, lim_ref, ldt_ref, bre_ref, bim_ref, pwre_ref, pwim_ref, bbre_ref, bbim_ref):
    lre, lim = lre_ref[...], lim_ref[...]
    dt = jnp.exp(ldt_ref[...])
    k = lax.broadcasted_iota(jnp.int32, pwre_ref.shape, 0).astype(F32) + 1.0
    mag = jnp.exp(k * (dt * lre))
    ang = k * (dt * lim)
    pwre = mag * jnp.cos(ang)
    pwim = mag * jnp.sin(ang)
    pwre_ref[...] = pwre
    pwim_ref[...] = pwim
    xr, xi = pwre[0:1] - 1.0, pwim[0:1]
    den = lre * lre + lim * lim
    cr = (xr * lre + xi * lim) / den
    ci = (xi * lre - xr * lim) / den
    bre, bim = bre_ref[...], bim_ref[...]
    bbre_ref[...] = cr * bre - ci * bim
    bbim_ref[...] = cr * bim + ci * bre


def _ssm_prep(a_re, a_im, log_dt, b_re, b_im):
    n = C_GROUPS * SSM_P
    row = lambda a: a.reshape(1, n)
    bt = lambda b: jnp.transpose(b, (2, 0, 1)).reshape(C_GROUP_W, n)
    shapes = [jax.ShapeDtypeStruct((SSM_R, n), F32)] * 2 + [jax.ShapeDtypeStruct((C_GROUP_W, n), F32)] * 2
    pwre, pwim, bbre, bbim = pl.pallas_call(
        _ssm_prep_body, out_shape=shapes, name="ssm_prep",
    )(row(a_re), row(a_im), row(jnp.repeat(log_dt, SSM_P)), bt(b_re), bt(b_im))
    lg, st = SSM_LANE_GROUPS, SSM_LG_STATES
    split = lambda t: jnp.transpose(t.reshape(t.shape[0], lg, st), (1, 0, 2))
    pw = jnp.concatenate([split(pwre), split(pwim)], axis=-1)
    eye = jnp.eye(C_GROUPS // lg, dtype=F32)

    def blockdiag(bb):
        r = bb.reshape(C_GROUP_W, lg, C_GROUPS // lg, SSM_P)
        return jnp.einsum('ab,mlbp->lambp', eye, r).reshape(lg, LANE, st)

    bdb = jnp.concatenate([blockdiag(bbre), blockdiag(bbim)], axis=-1).astype(BF16)
    return pw, bdb


def _ssm_out_matrix(c):
    lg = SSM_LANE_GROUPS
    eye = jnp.eye(C_GROUPS // lg, dtype=F32)
    r = c.reshape(lg, C_GROUPS // lg, C_GROUP_W, SSM_P)
    return jnp.einsum('ab,lanp->lapbn', eye, r).reshape(lg, SSM_LG_STATES, LANE).astype(BF16)


def _cmul_add(are, aim, hre, him, xre, xim):
    return are * hre - aim * him + xre, are * him + aim * hre + xim


def _s5_prompt_body(u_ref, bdb_ref, cre_ref, cim_ref, d_ref, pw_ref, yg_ref, hl_ref,
                    u3_ref, h_ref, car_ref, hc_ref):
    r_steps, nc, st = SSM_R, SSM_NC, SSM_LG_STATES
    re, im = slice(0, st), slice(st, 2 * st)

    @pl.when(pl.program_id(1) == 0)
    def _():
        hc_ref[...] = jnp.zeros(hc_ref.shape, F32)

    def rows(r):
        return pl.ds(pl.multiple_of(r * nc, nc), nc)

    def regroup(r, c):
        u3_ref[rows(r), :] = u_ref[pl.ds(r, nc, stride=r_steps), :]
        return c

    lax.fori_loop(0, r_steps, regroup, 0)
    h_ref[...] = jnp.dot(u3_ref[...].astype(BF16), bdb_ref[0], preferred_element_type=F32)

    half = st // 2
    for hf in range(2):
        cre_ = slice(hf * half, (hf + 1) * half)
        cim_ = slice(st + hf * half, st + (hf + 1) * half)
        are = jnp.broadcast_to(pw_ref[0, 0:1, cre_], (nc, half))
        aim = jnp.broadcast_to(pw_ref[0, 0:1, cim_], (nc, half))

        def step(r, carry, cre_=cre_, cim_=cim_, are=are, aim=aim):
            hre, him = _cmul_add(are, aim, carry[0], carry[1], h_ref[rows(r), cre_], h_ref[rows(r), cim_])
            h_ref[rows(r), cre_] = hre
            h_ref[rows(r), cim_] = him
            return hre, him

        lax.fori_loop(1, r_steps, step, (h_ref[0:nc, cre_], h_ref[0:nc, cim_]))

    ends = h_ref[(r_steps - 1) * nc:r_steps * nc, :]
    are, aim = pw_ref[0, r_steps - 1:r_steps, re], pw_ref[0, r_steps - 1:r_steps, im]
    cre, cim = hc_ref[0:1, re], hc_ref[0:1, im]
    for c in range(nc):
        car_ref[c:c + 1, re] = cre
        car_ref[c:c + 1, im] = cim
        cre, cim = _cmul_add(are, aim, cre, cim, ends[c:c + 1, re], ends[c:c + 1, im])
    hc_ref[:, re] = jnp.broadcast_to(cre, (8, st))
    hc_ref[:, im] = jnp.broadcast_to(cim, (8, st))
    hl_ref[0] = hc_ref[...]

    def fix(r, c):
        pre, pim = pw_ref[0, pl.ds(r, 1), re], pw_ref[0, pl.ds(r, 1), im]
        hre, him = _cmul_add(pre, pim, car_ref[:, re], car_ref[:, im], h_ref[rows(r), re], h_ref[rows(r), im])
        h_ref[rows(r), re] = hre
        h_ref[rows(r), im] = him
        return c

    lax.fori_loop(0, r_steps, fix, 0)

    y = (jnp.dot(h_ref[:, re].astype(BF16), cre_ref[0], preferred_element_type=F32)
         - jnp.dot(h_ref[:, im].astype(BF16), cim_ref[0], preferred_element_type=F32)
         + d_ref[...] * u3_ref[...])
    u3_ref[...] = _gelu(y)

    def ungroup(r, c):
        yg_ref[pl.ds(r, nc, stride=r_steps), :] = u3_ref[rows(r), :]
        return c

    lax.fori_loop(0, r_steps, ungroup, 0)


def _s5_prompt(p, bdb, cre, cim, d, pw):
    seq = p.shape[0]
    tseg = SSM_R * SSM_NC
    st2 = 2 * SSM_LG_STATES
    lgs = SSM_LANE_GROUPS
    return pl.pallas_call(
        _s5_prompt_body,
        grid=(lgs, seq // tseg),
        in_specs=[
            pl.BlockSpec((tseg, LANE), lambda g, t: (t, COL_CU + g)),
            pl.BlockSpec((1, LANE, st2), lambda g, t: (g, 0, 0)),
            pl.BlockSpec((1, SSM_LG_STATES, LANE), lambda g, t: (g, 0, 0)),
            pl.BlockSpec((1, SSM_LG_STATES, LANE), lambda g, t: (g, 0, 0)),
            pl.BlockSpec((1, LANE), lambda g, t: (0, g)),
            pl.BlockSpec((1, SSM_R, st2), lambda g, t: (g, 0, 0)),
        ],
        out_specs=[
            pl.BlockSpec((tseg, LANE), lambda g, t: (t, g)),
            pl.BlockSpec((1, 8, st2), lambda g, t: (g, 0, 0)),
        ],
        out_shape=[jax.ShapeDtypeStruct((seq, C_WIDTH), F32), jax.ShapeDtypeStruct((lgs, 8, st2), F32)],
        scratch_shapes=[
            pltpu.VMEM((tseg, LANE), F32),
            pltpu.VMEM((tseg, st2), F32),
            pltpu.VMEM((SSM_NC, st2), F32),
            pltpu.VMEM((8, st2), F32),
        ],
        compiler_params=_params(("arbitrary", "arbitrary"), 40),
        name="s5_prompt",
    )(p, bdb, cre, cim, d, pw)


def _s5_sample_body(u_ref, hre_ref, him_ref, bdb_ref, cre_ref, cim_ref, d_ref, pw_ref,
                    yg_ref, ore_ref, oim_ref, h_ref, *, db):
    st = SSM_LG_STATES
    re, im = slice(0, st), slice(st, 2 * st)
    u = u_ref[...]
    x = jnp.dot(u.astype(BF16), bdb_ref[0], preferred_element_type=F32)
    are, aim = pw_ref[0, 0:1, re], pw_ref[0, 0:1, im]
    hre, him = hre_ref[...], him_ref[...]
    for j in range(DEC_SEQ):
        rj = slice(j * db, (j + 1) * db)
        hre, him = _cmul_add(are, aim, hre, him, x[rj, re], x[rj, im])
        h_ref[rj, re] = hre
        h_ref[rj, im] = him
    ore_ref[...] = hre
    oim_ref[...] = him
    y = (jnp.dot(h_ref[:, re].astype(BF16), cre_ref[0], preferred_element_type=F32)
         - jnp.dot(h_ref[:, im].astype(BF16), cim_ref[0], preferred_element_type=F32)
         + d_ref[...] * u)
    yg_ref[...] = _gelu(y)


def _s5_sample(p, h0re, h0im, bdb, cre, cim, d, pw):
    t = p.shape[0]
    db = t // DEC_SEQ
    st = SSM_LG_STATES
    lgs = SSM_LANE_GROUPS
    return pl.pallas_call(
        functools.partial(_s5_sample_body, db=db),
        grid=(lgs,),
        in_specs=[
            pl.BlockSpec((t, LANE), lambda g: (0, COL_CU + g)),
            pl.BlockSpec((db, st), lambda g: (0, g)),
            pl.BlockSpec((db, st), lambda g: (0, g)),
            pl.BlockSpec((1, LANE, 2 * st), lambda g: (g, 0, 0)),
            pl.BlockSpec((1, st, LANE), lambda g: (g, 0, 0)),
            pl.BlockSpec((1, st, LANE), lambda g: (g, 0, 0)),
            pl.BlockSpec((1, LANE), lambda g: (0, g)),
            pl.BlockSpec((1, SSM_R, 2 * st), lambda g: (g, 0, 0)),
        ],
        out_specs=[
            pl.BlockSpec((t, LANE), lambda g: (0, g)),
            pl.BlockSpec((db, st), lambda g: (0, g)),
            pl.BlockSpec((db, st), lambda g: (0, g)),
        ],
        out_shape=[jax.ShapeDtypeStruct((t, C_WIDTH), F32),
                   jax.ShapeDtypeStruct((db, lgs * st), F32), jax.ShapeDtypeStruct((db, lgs * st), F32)],
        scratch_shapes=[pltpu.VMEM((t, 2 * st), F32)],
        compiler_params=_params(("arbitrary",), 32),
        name="s5_sample",
    )(p, h0re, h0im, bdb, cre, cim, d, pw)


def _mix_out_body(a_ref, b_ref, y_ref, x_ref, wg_ref, ga_ref, gb_ref, gc_ref, wo_ref, o_ref):
    glu = jnp.dot(y_ref[...].astype(BF16), wg_ref[...], preferred_element_type=F32)
    c = glu[:, :C_WIDTH] * jax.nn.sigmoid(glu[:, C_WIDTH:])
    mixed = jnp.concatenate([
        (_rms(a_ref[...]) * ga_ref[...]).astype(BF16),
        (_rms(b_ref[...]) * gb_ref[...]).astype(BF16),
        (_rms(c) * gc_ref[...]).astype(BF16)], axis=1)
    o_ref[...] = x_ref[...] + jnp.dot(mixed, wo_ref[...], preferred_element_type=F32)


def _mix_out(a, b, y, x, w_glu, g_a, g_b, g_c, w_out, layer, tm=512):
    t = x.shape[0]
    row = lambda w: pl.BlockSpec((tm, w), lambda i: (i, 0))
    whole = lambda arr: pl.BlockSpec(arr.shape, lambda i: (0, 0))
    of_layer = lambda arr: pl.BlockSpec((None,) + arr.shape[1:], lambda i: (layer, 0, 0))
    return pl.pallas_call(
        _mix_out_body,
        grid=(t // tm,),
        in_specs=[row(A_WIDTH), row(B_WIDTH), row(C_WIDTH), row(D_MODEL),
                  of_layer(w_glu), whole(g_a), whole(g_b), whole(g_c), of_layer(w_out)],
        out_specs=row(D_MODEL),
        out_shape=jax.ShapeDtypeStruct((t, D_MODEL), F32),
        compiler_params=_params(("arbitrary",), 56),
        name="mix_out",
    )(a, b, y, x, w_glu, g_a, g_b, g_c, w_out)


FFN_HALO = 16


def _conv_gate(ug, uv, cwg_ref, cwv_ref, cbg_ref, cbv_ref, taps):
    def conv(u, cw_ref, cb_ref):
        return ((cb_ref[...] + cw_ref[0:1, :] * taps[0](u)) + cw_ref[1:2, :] * taps[1](u)) + cw_ref[2:3, :] * taps[2](u)
    gate = conv(ug, cwg_ref, cbg_ref)
    return (gate * jax.nn.sigmoid(gate)) * conv(uv, cwv_ref, cbv_ref)


def _finish(o_ref, gf_ref, is_last_tile, final_norm):
    if final_norm:
        @pl.when(is_last_tile)
        def _():
            o_ref[...] = _rms(o_ref[...]) * gf_ref[...]


def _ffn_prompt_body(x_ref, halo_ref, g_ref, wg_ref, wv_ref, cwg_ref, cwv_ref, cbg_ref, cbv_ref, wd_ref, gf_ref,
                     o_ref, tail_ref, h_ref, *, tm, final_norm):
    i, f = pl.program_id(0), pl.program_id(1)
    hl = FFN_HALO

    @pl.when(f == 0)
    def _():
        h_ref[hl:, :] = (_rms(x_ref[...]) * g_ref[...]).astype(BF16)
        prev = _rms(halo_ref[...]) * g_ref[...]
        h_ref[0:hl, :] = jnp.where(i > 0, prev, 0.0).astype(BF16)
        o_ref[...] = x_ref[...]

    h = h_ref[...]
    ug = jnp.dot(h, wg_ref[...], preferred_element_type=F32)
    uv = jnp.dot(h, wv_ref[...], preferred_element_type=F32)
    taps = (lambda u: u[hl - 2:hl - 2 + tm], lambda u: u[hl - 1:hl - 1 + tm], lambda u: u[hl:])
    act = _conv_gate(ug, uv, cwg_ref, cwv_ref, cbg_ref, cbv_ref, taps)
    o_ref[...] += jnp.dot(act.astype(BF16), wd_ref[...], preferred_element_type=F32)

    tail_ref[0] = ug[hl + tm - 8:]
    tail_ref[1] = uv[hl + tm - 8:]
    _finish(o_ref, gf_ref, f == pl.num_programs(1) - 1, final_norm)


def _ffn_prompt(x, g, w_up, conv_w, conv_b, w_down, g_final, layer, final_norm, tm=512, tf=512):
    t = x.shape[0]
    nf = D_FF // tf
    hl = FFN_HALO
    cb = conv_b.reshape(conv_b.shape[0], 1, 2 * D_FF)
    return pl.pallas_call(
        functools.partial(_ffn_prompt_body, tm=tm, final_norm=final_norm),
        grid=(t // tm, nf),
        in_specs=[
            pl.BlockSpec((tm, D_MODEL), lambda i, f: (i, 0)),
            pl.BlockSpec((hl, D_MODEL), lambda i, f: (jnp.maximum(i * (tm // hl) - 1, 0), 0)),
            pl.BlockSpec((1, D_MODEL), lambda i, f: (0, 0)),
            pl.BlockSpec((None, D_MODEL, tf), lambda i, f: (layer, 0, f)),
            pl.BlockSpec((None, D_MODEL, tf), lambda i, f: (layer, 0, nf + f)),
            pl.BlockSpec((None, 3, tf), lambda i, f: (layer, 0, f)),
            pl.BlockSpec((None, 3, tf), lambda i, f: (layer, 0, nf + f)),
            pl.BlockSpec((None, 1, tf), lambda i, f: (layer, 0, f)),
            pl.BlockSpec((None, 1, tf), lambda i, f: (layer, 0, nf + f)),
            pl.BlockSpec((None, tf, D_MODEL), lambda i, f: (layer, f, 0)),
            pl.BlockSpec((1, D_MODEL), lambda i, f: (0, 0)),
        ],
        out_specs=[
            pl.BlockSpec((tm, D_MODEL), lambda i, f: (i, 0), pipeline_mode=pl.Buffered(1)),
            pl.BlockSpec((None, 2, 8, tf), lambda i, f: (i, 0, 0, f)),
        ],
        out_shape=[jax.ShapeDtypeStruct((t, D_MODEL), F32), jax.ShapeDtypeStruct((t // tm, 2, 8, D_FF), F32)],
        scratch_shapes=[pltpu.VMEM((hl + tm, D_MODEL), BF16)],
        compiler_params=_params(("arbitrary", "arbitrary"), 56),
        name="ffn_prompt",
    )(x, x, g, w_up, w_up, conv_w, conv_w, cb, cb, w_down, g_final)


def _ffn_sample_body(x_ref, g_ref, sg_ref, sv_ref, wg_ref, wv_ref, cwg_ref, cwv_ref, cbg_ref, cbv_ref, wd_ref, gf_ref,
                     o_ref, tg_ref, tv_ref, h_ref, *, db, final_norm):
    f = pl.program_id(0)
    t = DEC_SEQ * db

    @pl.when(f == 0)
    def _():
        h_ref[...] = (_rms(x_ref[...]) * g_ref[...]).astype(BF16)
        o_ref[...] = x_ref[...]

    h = h_ref[...]
    ug = jnp.dot(h, wg_ref[...], preferred_element_type=F32)
    uv = jnp.dot(h, wv_ref[...], preferred_element_type=F32)
    pg = jnp.concatenate([sg_ref[:, 0, :], sg_ref[:, 1, :], ug], axis=0)
    pv = jnp.concatenate([sv_ref[:, 0, :], sv_ref[:, 1, :], uv], axis=0)
    taps = (lambda u: u[0:t], lambda u: u[db:db + t], lambda u: u[2 * db:2 * db + t])
    act = _conv_gate(pg, pv, cwg_ref, cwv_ref, cbg_ref, cbv_ref, taps)
    o_ref[...] += jnp.dot(act.astype(BF16), wd_ref[...], preferred_element_type=F32)

    for j in range(2):
        rows = slice((DEC_SEQ - 2 + j) * db, (DEC_SEQ - 1 + j) * db)
        tg_ref[:, j, :] = ug[rows]
        tv_ref[:, j, :] = uv[rows]
    _finish(o_ref, gf_ref, f == pl.num_programs(0) - 1, final_norm)


def _ffn_sample(x, g, state, w_up, conv_w, conv_b, w_down, g_final, layer, final_norm, tf=512):
    t = x.shape[0]
    db = t // DEC_SEQ
    nf = D_FF // tf
    cb = conv_b.reshape(conv_b.shape[0], 1, 2 * D_FF)
    return pl.pallas_call(
        functools.partial(_ffn_sample_body, db=db, final_norm=final_norm),
        grid=(nf,),
        in_specs=[
            pl.BlockSpec((t, D_MODEL), lambda f: (0, 0)),
            pl.BlockSpec((1, D_MODEL), lambda f: (0, 0)),
            pl.BlockSpec((None, db, 2, tf), lambda f: (layer, 0, 0, f)),
            pl.BlockSpec((None, db, 2, tf), lambda f: (layer, 0, 0, nf + f)),
            pl.BlockSpec((None, D_MODEL, tf), lambda f: (layer, 0, f)),
            pl.BlockSpec((None, D_MODEL, tf), lambda f: (layer, 0, nf + f)),
            pl.BlockSpec((None, 3, tf), lambda f: (layer, 0, f)),
            pl.BlockSpec((None, 3, tf), lambda f: (layer, 0, nf + f)),
            pl.BlockSpec((None, 1, tf), lambda f: (layer, 0, f)),
            pl.BlockSpec((None, 1, tf), lambda f: (layer, 0, nf + f)),
            pl.BlockSpec((None, tf, D_MODEL), lambda f: (layer, f, 0)),
            pl.BlockSpec((1, D_MODEL), lambda f: (0, 0)),
        ],
        out_specs=[
            pl.BlockSpec((t, D_MODEL), lambda f: (0, 0)),
            pl.BlockSpec((db, 2, tf), lambda f: (0, 0, f)),
            pl.BlockSpec((db, 2, tf), lambda f: (0, 0, f)),
        ],
        out_shape=[jax.ShapeDtypeStruct((t, D_MODEL), F32)] + [jax.ShapeDtypeStruct((db, 2, D_FF), F32)] * 2,
        scratch_shapes=[pltpu.VMEM((t, D_MODEL), BF16)],
        compiler_params=_params(("arbitrary",), 48),
        name="ffn_sample",
    )(x, g, state, state, w_up, w_up, conv_w, conv_w, cb, cb, w_down, g_final)


def kernel(x_prompt, x_sample, cache_win_k, cache_win_v, state_ssm_re, state_ssm_im, state_ffn_conv,
           g_mix, w_in, g_out_a, g_out_b, g_out_c, gmlp_gv, gmlp_ws, gmlp_bs, ssm_a_re, ssm_a_im,
           ssm_log_dt, ssm_b_re, ssm_b_im, ssm_c_re, ssm_c_im, ssm_d, ssm_w_glu, w_out, g_ffn, w_up,
           conv_w, conv_b, w_down, g_final):
    depth = w_in.shape[0]
    bsz, seq, _ = x_prompt.shape
    db, ds, _ = x_sample.shape
    assert bsz == 1 and ds == DEC_SEQ and seq % (SSM_R * SSM_NC) == 0 and seq >= A_WINDOW_MAX
    n_states = C_GROUPS * SSM_P
    row = lambda v: v.reshape(1, -1)

    g_fin = row(g_final)
    w_in_b, w_glu_b, w_out_b = w_in.astype(BF16), ssm_w_glu.astype(BF16), w_out.astype(BF16)
    w_up_b, w_down_b = w_up.astype(BF16), w_down.astype(BF16)
    xp = x_prompt.reshape(seq, D_MODEL)
    xs = jnp.transpose(x_sample, (1, 0, 2)).reshape(ds * db, D_MODEL)
    outs = {k: [] for k in ("kp", "vp", "ks", "vs", "gv", "rp", "ip", "rs", "is", "cp", "cs")}

    for l in range(depth):
        pw, bdb = _ssm_prep(ssm_a_re[l], ssm_a_im[l], ssm_log_dt[l], ssm_b_re[l], ssm_b_im[l])
        c_re, c_im = _ssm_out_matrix(ssm_c_re[l]), _ssm_out_matrix(ssm_c_im[l])
        d_row = row(ssm_d[l])
        gains = (row(g_out_a[l]), row(g_out_b[l]), row(g_out_c[l]))
        last = l == depth - 1

        p = _proj(xp, row(g_mix[l]), w_in_b, l, tm=min(1024, seq))
        a = _attn_prompt(p)
        b = _gmlp_prompt(p, row(gmlp_gv[l]), gmlp_ws[l], gmlp_bs[l])
        yg, h_last = _s5_prompt(p, bdb, c_re, c_im, d_row, pw)
        xm = _mix_out(a, b, yg, xp, w_glu_b, *gains, w_out_b, l)
        xp, tail = _ffn_prompt(xm, row(g_ffn[l]), w_up_b, conv_w, conv_b, w_down_b, g_fin, l,
                               final_norm=last, tm=min(1024, seq))
        keep = min(A_WINDOW_MAX, seq)
        outs["kp"].append(p[seq - keep:, A_WIDTH:2 * A_WIDTH].reshape(1, keep, A_HEADS, HEAD_DIM))
        outs["vp"].append(p[seq - keep:, 2 * A_WIDTH:3 * A_WIDTH].reshape(1, keep, A_HEADS, HEAD_DIM))
        st = SSM_LG_STATES
        outs["rp"].append(h_last[:, 0, :st].reshape(1, C_GROUPS, SSM_P))
        outs["ip"].append(h_last[:, 0, st:].reshape(1, C_GROUPS, SSM_P))
        outs["cp"].append(jnp.transpose(tail[-1, :, 6:8, :], (1, 0, 2)).reshape(1, 2, 2 * D_FF))

        ps = _proj(xs, row(g_mix[l]), w_in_b, l, tm=ds * db)

        def heads(cols):
            return jnp.transpose(cols.reshape(ds, db, A_HEADS, HEAD_DIM), (1, 0, 2, 3)).reshape(
                db, ds * A_HEADS, HEAD_DIM)

        q3, k3, v3 = (heads(ps[:, i * A_WIDTH:(i + 1) * A_WIDTH]) for i in range(3))
        a3 = _attn_sample(q3, k3, v3, cache_win_k, cache_win_v, l)
        a_s = jnp.transpose(a3.reshape(db, ds, A_WIDTH), (1, 0, 2)).reshape(ds * db, A_WIDTH)
        b_s, gv_s = _gmlp_sample(ps, row(gmlp_gv[l]), gmlp_ws[l], gmlp_bs[l])
        yg_s, hre, him = _s5_sample(ps, state_ssm_re[l].reshape(db, n_states), state_ssm_im[l].reshape(db, n_states),
                                    bdb, c_re, c_im, d_row, pw)
        xm_s = _mix_out(a_s, b_s, yg_s, xs, w_glu_b, *gains, w_out_b, l, tm=ds * db)
        xs, tail_g, tail_v = _ffn_sample(xm_s, row(g_ffn[l]), state_ffn_conv, w_up_b, conv_w, conv_b, w_down_b,
                                         g_fin, l, final_norm=last)
        outs["ks"].append(k3.reshape(db, ds, A_HEADS, HEAD_DIM))
        outs["vs"].append(v3.reshape(db, ds, A_HEADS, HEAD_DIM))
        outs["gv"].append(jnp.transpose(gv_s.reshape(ds, db, B_WIDTH), (1, 0, 2)))
        outs["rs"].append(hre.reshape(db, C_GROUPS, SSM_P))
        outs["is"].append(him.reshape(db, C_GROUPS, SSM_P))
        outs["cs"].append(jnp.concatenate([tail_g, tail_v], axis=-1))

    y_p = xp.reshape(1, seq, D_MODEL)
    y_s = jnp.transpose(xs.reshape(ds, db, D_MODEL), (1, 0, 2))
    st = lambda k: jnp.stack(outs[k])
    return (y_p, y_s, st("kp"), st("vp"), st("ks"), st("vs"), st("gv"),
            st("rp"), st("ip"), st("rs"), st("is"), st("cp"), st("cs"))
```

```python
import functools
import math

import numpy as np
import jax
import jax.numpy as jnp
from jax import lax
from jax.experimental import pallas as pl
from jax.experimental.pallas import tpu as pltpu

F32 = jnp.float32
BF16 = jnp.bfloat16

D_MODEL = 2048
A_WIDTH = 1024
A_HEADS = 8
HEAD_DIM = 128
A_DILATIONS = (1, 4, 16)
A_SPAN = 128
A_WINDOW_MAX = 2048
SAMPLE_TAIL_ROWS = 512
SAMPLE_FAR_QUARTERS = A_WINDOW_MAX // SAMPLE_TAIL_ROWS - 1
ATTN_UNROLL = 8
B_WIDTH = 512
B_GROUPS = 4
CHUNK = 128
C_WIDTH = 512
C_GROUPS = 32
C_GROUP_W = 16
SSM_P = 64
N_IN = 3 * A_WIDTH + 2 * B_WIDTH + C_WIDTH
D_FF = 5632
DEC_SEQ = 4
EPS = 1e-6
NEG = -1e30

LANE = 128
COL_Q, COL_K, COL_V = 0, A_WIDTH // LANE, 2 * A_WIDTH // LANE
COL_BU = 3 * A_WIDTH // LANE
COL_BV = COL_BU + B_WIDTH // LANE
COL_CU = COL_BV + B_WIDTH // LANE

SSM_LANE_GROUPS = 4
SSM_LG_STATES = C_GROUPS * SSM_P // SSM_LANE_GROUPS
SSM_R = 64
SSM_NC = 32

MIB = 1024 * 1024


def _params(sem, vmem_mib, flags=None):
    return pltpu.CompilerParams(dimension_semantics=sem, vmem_limit_bytes=vmem_mib * MIB, flags=flags)


def _gelu(x):
    c = math.sqrt(2.0 / math.pi)
    return x * (0.5 * (1.0 + jnp.tanh(c * (x + 0.044715 * (x * x * x)))))


def _rms(x):
    return x * lax.rsqrt(jnp.mean(x * x, axis=-1, keepdims=True) + EPS)


_NT = (((1,), (1,)), ((), ()))


def _proj_body(x_ref, g_ref, w_ref, o_ref, h_ref):
    @pl.when(pl.program_id(1) == 0)
    def _():
        h_ref[...] = (_rms(x_ref[...]) * g_ref[...]).astype(BF16)

    o_ref[...] = jnp.dot(h_ref[...], w_ref[...], preferred_element_type=F32)


def _proj(x, g, w, layer, tm, tn=1536):
    t = x.shape[0]
    return pl.pallas_call(
        _proj_body,
        grid=(t // tm, N_IN // tn),
        in_specs=[
            pl.BlockSpec((tm, D_MODEL), lambda i, j: (i, 0)),
            pl.BlockSpec((1, D_MODEL), lambda i, j: (0, 0)),
            pl.BlockSpec((None, D_MODEL, tn), lambda i, j: (layer, 0, j)),
        ],
        out_specs=pl.BlockSpec((tm, tn), lambda i, j: (i, j)),
        out_shape=jax.ShapeDtypeStruct((t, N_IN), F32),
        scratch_shapes=[pltpu.VMEM((tm, D_MODEL), BF16)],
        compiler_params=_params(("arbitrary", "arbitrary"), 56),
        name="proj_in",
    )(x, g, w)


def _attn_prompt_body(q_ref, k_ref, v_ref, o_ref, acc_ref, m_ref, l_ref, *, seq):
    scale = math.log2(math.e) / math.sqrt(HEAD_DIM)
    ii = lax.broadcasted_iota(jnp.int32, (A_SPAN, 2 * A_SPAN), 0)
    jj = lax.broadcasted_iota(jnp.int32, (A_SPAN, 2 * A_SPAN), 1)
    band = (jj >= ii) & (jj <= ii + A_SPAN)
    band_cur = band & (jj >= A_SPAN)
    blk = (A_SPAN, A_SPAN)

    for order, dil in enumerate(sorted(A_DILATIONS, reverse=True)):
        nb = seq // (A_SPAN * dil)
        step = A_SPAN * dil

        def rows(start, dil=dil):
            if dil == 1:
                return pl.ds(pl.multiple_of(start, A_SPAN), A_SPAN)
            return pl.ds(start, A_SPAN, stride=dil)

        def runs(run_list, rows=rows, step=step, first=order == 0):
            work = []
            for r, n0, count in run_list:
                static_start = isinstance(n0, int)
                if not (static_start and n0 == 0):
                    prev = r + jnp.maximum(n0 - 1, 0) * step
                    kp, vp = k_ref[rows(prev), :].astype(BF16), v_ref[rows(prev), :].astype(BF16)
                for u in range(count):
                    base = r + (n0 + u) * step
                    q = (q_ref[rows(base), :] * scale).astype(BF16)
                    kc, vc = k_ref[rows(base), :].astype(BF16), v_ref[rows(base), :].astype(BF16)
                    if u == 0 and static_start and n0 == 0:
                        kp, vp, mask = kc, vc, band_cur
                    elif u == 0 and not static_start:
                        mask = band & ((jj >= A_SPAN) | (n0 > 0))
                    else:
                        mask = band
                    s = lax.dot_general(q, jnp.concatenate([kp, kc], axis=0), _NT,
                                        preferred_element_type=F32)
                    s = jnp.where(mask, s, NEG)
                    v2 = jnp.concatenate([vp, vc], axis=0)
                    m_blk = jnp.max(s, axis=1, keepdims=True)
                    if first:
                        m_new = jnp.broadcast_to(m_blk, blk)
                        p = jnp.exp2(s - m_blk)
                        l_new = jnp.broadcast_to(jnp.sum(p, axis=1, keepdims=True), blk)
                        acc_new = jnp.dot(p.astype(BF16), v2, preferred_element_type=F32)
                    else:
                        m_old = m_ref[rows(base), :]
                        m_new = jnp.maximum(m_old, m_blk)
                        alpha = jnp.exp2(m_old - m_new)
                        p = jnp.exp2(s - jnp.concatenate([m_new, m_new], axis=1))
                        l_new = alpha * l_ref[rows(base), :] + jnp.sum(p, axis=1, keepdims=True)
                        acc_new = alpha * acc_ref[rows(base), :] + jnp.dot(p.astype(BF16), v2,
                                                                           preferred_element_type=F32)
                    work.append((base, m_new, l_new, acc_new))
                    kp, vp = kc, vc
            for base, m_new, l_new, acc_new in work:
                m_ref[rows(base), :] = m_new
                l_ref[rows(base), :] = l_new
                acc_ref[rows(base), :] = acc_new

        unroll = min(ATTN_UNROLL, nb * dil)
        if unroll <= nb:
            per_res = nb // unroll

            def trip(i, c, runs=runs, unroll=unroll, per_res=per_res):
                runs([(i // per_res, (i % per_res) * unroll, unroll)])
                return c
        else:
            res_per_trip = unroll // nb

            def trip(i, c, runs=runs, nb=nb, res_per_trip=res_per_trip):
                runs([(i * res_per_trip + j, 0, nb) for j in range(res_per_trip)])
                return c

        lax.fori_loop(0, nb * dil // unroll, trip, 0)

    o_ref[...] = acc_ref[...] / l_ref[...]


def _attn_prompt(p):
    seq = p.shape[0]
    blk = (seq, HEAD_DIM)
    return pl.pallas_call(
        functools.partial(_attn_prompt_body, seq=seq),
        grid=(A_HEADS,),
        in_specs=[
            pl.BlockSpec(blk, lambda h: (0, COL_Q + h)),
            pl.BlockSpec(blk, lambda h: (0, COL_K + h)),
            pl.BlockSpec(blk, lambda h: (0, COL_V + h)),
        ],
        out_specs=pl.BlockSpec(blk, lambda h: (0, h)),
        out_shape=jax.ShapeDtypeStruct((seq, A_WIDTH), F32),
        scratch_shapes=[pltpu.VMEM(blk, F32)] * 3,
        compiler_params=_params(("arbitrary",), 56),
        name="attn_prompt",
    )(p, p, p)


def _sample_attn_masks():
    j = np.arange(DEC_SEQ)[:, None, None, None]
    h = np.arange(A_HEADS)[None, :, None, None]
    hk = np.arange(A_HEADS)[None, None, None, :]
    same = (h == hk)
    i = np.arange(512)[None, None, :, None]
    cnt_l = same * ((i >= 384 + j).astype(np.int32) + ((i - j) % 4 == 0).astype(np.int32))
    jn = np.arange(DEC_SEQ)[None, None, :, None]
    cnt_n = same * ((jn <= j).astype(np.int32) + 2 * (jn == j).astype(np.int32))
    g = np.arange(128)[None, None, :, None]
    head_match = np.broadcast_to(same[0:1], (1, A_HEADS, 128, A_HEADS))
    return (cnt_l.reshape(DEC_SEQ * A_HEADS, 512 * A_HEADS).astype(np.float32),
            cnt_n.reshape(DEC_SEQ * A_HEADS, DEC_SEQ * A_HEADS).astype(np.float32),
            (head_match + 0 * g).reshape(A_HEADS, 128 * A_HEADS).astype(np.float32))


def _attn_sample_body(q_ref, k_ref, v_ref, *refs):
    nh = A_HEADS
    nfar = SAMPLE_FAR_QUARTERS * DEC_SEQ
    k_far, kl_ref = refs[:nfar], refs[nfar]
    v_far, vl_ref = refs[nfar + 1:2 * nfar + 1], refs[2 * nfar + 1]
    cl_ref, cn_ref, hm_ref, o_ref = refs[2 * nfar + 2:]
    q = q_ref[...] * (1.0 / math.sqrt(HEAD_DIM))
    qs = q.astype(BF16)
    cl, cn, hm = cl_ref[...], cn_ref[...], hm_ref[...]

    def stride16(far, last, res):
        near = last.reshape(32, 16, nh, HEAD_DIM)[:, res].reshape(32 * nh, HEAD_DIM)
        parts = [far[qt * DEC_SEQ + res][...].reshape(32 * nh, HEAD_DIM).astype(BF16)
                 for qt in range(SAMPLE_FAR_QUARTERS)]
        return jnp.concatenate(parts + [near.astype(BF16)], axis=0)

    k_last = kl_ref[...]
    kl = k_last.reshape(512 * nh, HEAD_DIM).astype(BF16)
    s_l = lax.dot_general(qs, kl, _NT, preferred_element_type=F32)
    s_n = lax.dot_general(qs, k_ref[...].astype(BF16), _NT, preferred_element_type=F32)
    s_a = []
    for res in range(DEC_SEQ):
        kk = stride16(k_far, k_last, res)
        s_a.append(lax.dot_general(q[res * nh:(res + 1) * nh].astype(BF16), kk, _NT,
                                   preferred_element_type=F32))
    m = jnp.max(jnp.where(cl > 0, s_l, NEG), axis=1, keepdims=True)
    m = jnp.maximum(m, jnp.max(jnp.where(cn > 0, s_n, NEG), axis=1, keepdims=True))
    m_a = jnp.concatenate([jnp.max(jnp.where(hm > 0, s, NEG), axis=1, keepdims=True) for s in s_a], axis=0)
    m = jnp.maximum(m, m_a)

    p_l = cl * jnp.exp(jnp.minimum(s_l - m, 0.0))
    p_n = cn * jnp.exp(jnp.minimum(s_n - m, 0.0))
    den = jnp.sum(p_l, axis=1, keepdims=True) + jnp.sum(p_n, axis=1, keepdims=True)
    v_last = vl_ref[...]
    vl = v_last.reshape(512 * nh, HEAD_DIM).astype(BF16)
    out = jnp.dot(p_l.astype(BF16), vl, preferred_element_type=F32)
    out = out + jnp.dot(p_n.astype(BF16), v_ref[...].astype(BF16), preferred_element_type=F32)
    o_a, d_a = [], []
    for res in range(DEC_SEQ):
        p_a = hm * jnp.exp(jnp.minimum(s_a[res] - m[res * nh:(res + 1) * nh], 0.0))
        d_a.append(jnp.sum(p_a, axis=1, keepdims=True))
        vv = stride16(v_far, v_last, res)
        o_a.append(jnp.dot(p_a.astype(BF16), vv, preferred_element_type=F32))
    out = out + jnp.concatenate(o_a, axis=0)
    den = den + jnp.concatenate(d_a, axis=0)
    o_ref[...] = out / den


def _attn_sample(q3, k3, v3, cache_k, cache_v, layer):
    db = q3.shape[0]
    depth, _, w_buf, nh, hd = cache_k.shape
    assert (w_buf, nh, hd) == (A_WINDOW_MAX, A_HEADS, HEAD_DIM) and q3.shape[1] == DEC_SEQ * A_HEADS
    cnt_l, cnt_n, head_match = _sample_attn_masks()
    rows = DEC_SEQ * A_HEADS

    nq = SAMPLE_FAR_QUARTERS

    def strided_specs():
        return [pl.BlockSpec((None, None, None, 32, None, nh, hd), functools.partial(
            lambda b, qt, res: (layer, b, qt, 0, res, 0, 0), qt=qt, res=res))
            for qt in range(nq) for res in range(DEC_SEQ)]

    last_spec = pl.BlockSpec((None, None, None, SAMPLE_TAIL_ROWS, nh, hd), lambda b: (layer, b, nq, 0, 0, 0))
    tok_spec = pl.BlockSpec((None, rows, hd), lambda b: (b, 0, 0))

    def whole(a):
        return pl.BlockSpec(a.shape, lambda b: (0, 0))

    k16 = cache_k.reshape(depth, db, nq + 1, 32, 16, nh, hd)
    v16 = cache_v.reshape(depth, db, nq + 1, 32, 16, nh, hd)
    k512 = cache_k.reshape(depth, db, nq + 1, SAMPLE_TAIL_ROWS, nh, hd)
    v512 = cache_v.reshape(depth, db, nq + 1, SAMPLE_TAIL_ROWS, nh, hd)
    nfar = nq * DEC_SEQ
    return pl.pallas_call(
        _attn_sample_body,
        grid=(db,),
        in_specs=[tok_spec, tok_spec, tok_spec] + strided_specs() + [last_spec] + strided_specs() + [last_spec]
        + [whole(cnt_l), whole(cnt_n), whole(head_match)],
        out_specs=tok_spec,
        out_shape=jax.ShapeDtypeStruct((db, rows, hd), F32),
        compiler_params=_params(("arbitrary",), 48),
        name="attn_sample",
    )(q3, k3, v3, *([k16] * nfar), k512, *([v16] * nfar), v512,
      jnp.asarray(cnt_l), jnp.asarray(cnt_n), jnp.asarray(head_match))


def _gmlp_prompt_body(bu_ref, bv_ref, gg_ref, ws_ref, bs_ref, o_ref, *, chunks):
    ri = lax.broadcasted_iota(jnp.int32, (CHUNK, CHUNK), 0)
    ci = lax.broadcasted_iota(jnp.int32, (CHUNK, CHUNK), 1)
    wm = jnp.where(ci <= ri, ws_ref[0], 0.0).astype(BF16)
    bias = bs_ref[0]
    gain = gg_ref[...]
    for c in range(chunks):
        sl = slice(c * CHUNK, (c + 1) * CHUNK)
        gv = _rms(_gelu(bv_ref[sl, :])) * gain
        mix = jnp.dot(wm, gv.astype(BF16), preferred_element_type=F32) + bias
        o_ref[sl, :] = _gelu(bu_ref[sl, :]) * mix


def _gmlp_prompt(p, gain, ws, bs, chunks=8):
    seq = p.shape[0]
    tm = chunks * CHUNK
    return pl.pallas_call(
        functools.partial(_gmlp_prompt_body, chunks=chunks),
        grid=(seq // tm, B_GROUPS),
        in_specs=[
            pl.BlockSpec((tm, LANE), lambda i, g: (i, COL_BU + g)),
            pl.BlockSpec((tm, LANE), lambda i, g: (i, COL_BV + g)),
            pl.BlockSpec((1, LANE), lambda i, g: (0, g)),
            pl.BlockSpec((1, CHUNK, CHUNK), lambda i, g: (g, 0, 0)),
            pl.BlockSpec((1, CHUNK, 1), lambda i, g: (g, 0, 0)),
        ],
        out_specs=pl.BlockSpec((tm, LANE), lambda i, g: (i, g)),
        out_shape=jax.ShapeDtypeStruct((seq, B_WIDTH), F32),
        compiler_params=_params(("arbitrary", "arbitrary"), 32),
        name="gmlp_prompt",
    )(p, p, gain, ws, bs.reshape(B_GROUPS, CHUNK, 1))


def _gmlp_sample_body(ws_ref, bs_ref, bu_ref, bv_ref, gg_ref, o_ref, gv_ref, *, db):
    g = pl.program_id(0)
    gv = _rms(_gelu(bv_ref[...])) * gg_ref[...]
    gv_ref[...] = gv
    gu = _gelu(bu_ref[...])
    for i in range(DEC_SEQ):
        mix = jnp.full((db, LANE), bs_ref[g, i], F32)
        for j in range(i + 1):
            mix = mix + ws_ref[g, i * DEC_SEQ + j] * gv[j * db:(j + 1) * db]
        o_ref[i * db:(i + 1) * db, :] = gu[i * db:(i + 1) * db] * mix


def _gmlp_sample(p, gain, ws, bs):
    t = p.shape[0]
    db = t // DEC_SEQ
    ws4 = ws[:, :DEC_SEQ, :DEC_SEQ].reshape(B_GROUPS, DEC_SEQ * DEC_SEQ)
    bs4 = bs[:, :DEC_SEQ]
    smem = pl.BlockSpec(memory_space=pltpu.SMEM)
    return pl.pallas_call(
        functools.partial(_gmlp_sample_body, db=db),
        grid=(B_GROUPS,),
        in_specs=[
            smem, smem,
            pl.BlockSpec((t, LANE), lambda g: (0, COL_BU + g)),
            pl.BlockSpec((t, LANE), lambda g: (0, COL_BV + g)),
            pl.BlockSpec((1, LANE), lambda g: (0, g)),
        ],
        out_specs=[pl.BlockSpec((t, LANE), lambda g: (0, g))] * 2,
        out_shape=[jax.ShapeDtypeStruct((t, B_WIDTH), F32)] * 2,
        compiler_params=_params(("arbitrary",), 32),
        name="gmlp_sample",
    )(ws4, bs4, p, p, gain)


def _ssm_prep_body(lre_ref, lim_ref, ldt_ref, bre_ref, bim_ref, pwre_ref, pwim_ref, bbre_ref, bbim_ref):
    lre, lim = lre_ref[...], lim_ref[...]
    dt = jnp.exp(ldt_ref[...])
    k = lax.broadcasted_iota(jnp.int32, pwre_ref.shape, 0).astype(F32) + 1.0
    mag = jnp.exp(k * (dt * lre))
    ang = k * (dt * lim)
    pwre = mag * jnp.cos(ang)
    pwim = mag * jnp.sin(ang)
    pwre_ref[...] = pwre
    pwim_ref[...] = pwim
    xr, xi = pwre[0:1] - 1.0, pwim[0:1]
    den = lre * lre + lim * lim
    cr = (xr * lre + xi * lim) / den
    ci = (xi * lre - xr * lim) / den
    bre, bim = bre_ref[...], bim_ref[...]
    bbre_ref[...] = cr * bre - ci * bim
    bbim_ref[...] = cr * bim + ci * bre


def _ssm_prep(a_re, a_im, log_dt, b_re, b_im):
    n = C_GROUPS * SSM_P
    row = lambda a: a.reshape(1, n)
    bt = lambda b: jnp.transpose(b, (2, 0, 1)).reshape(C_GROUP_W, n)
    shapes = [jax.ShapeDtypeStruct((SSM_R, n), F32)] * 2 + [jax.ShapeDtypeStruct((C_GROUP_W, n), F32)] * 2
    pwre, pwim, bbre, bbim = pl.pallas_call(
        _ssm_prep_body, out_shape=shapes, name="ssm_prep",
    )(row(a_re), row(a_im), row(jnp.repeat(log_dt, SSM_P)), bt(b_re), bt(b_im))
    lg, st = SSM_LANE_GROUPS, SSM_LG_STATES
    split = lambda t: jnp.transpose(t.reshape(t.shape[0], lg, st), (1, 0, 2))
    pw = jnp.concatenate([split(pwre), split(pwim)], axis=-1)
    eye = jnp.eye(C_GROUPS // lg, dtype=F32)

    def blockdiag(bb):
        r = bb.reshape(C_GROUP_W, lg, C_GROUPS // lg, SSM_P)
        return jnp.einsum('ab,mlbp->lambp', eye, r).reshape(lg, LANE, st)

    bdb = jnp.concatenate([blockdiag(bbre), blockdiag(bbim)], axis=-1).astype(BF16)
    return pw, bdb


def _ssm_out_matrix(c):
    lg = SSM_LANE_GROUPS
    eye = jnp.eye(C_GROUPS // lg, dtype=F32)
    r = c.reshape(lg, C_GROUPS // lg, C_GROUP_W, SSM_P)
    return jnp.einsum('ab,lanp->lapbn', eye, r).reshape(lg, SSM_LG_STATES, LANE).astype(BF16)


def _cmul_add(are, aim, hre, him, xre, xim):
    return are * hre - aim * him + xre, are * him + aim * hre + xim


def _s5_prompt_body(u_ref, bdb_ref, cre_ref, cim_ref, d_ref, pw_ref, yg_ref, hl_ref,
                    u3_ref, h_ref, car_ref, hc_ref):
    r_steps, nc, st = SSM_R, SSM_NC, SSM_LG_STATES
    re, im = slice(0, st), slice(st, 2 * st)

    @pl.when(pl.program_id(1) == 0)
    def _():
        hc_ref[...] = jnp.zeros(hc_ref.shape, F32)

    def rows(r):
        return pl.ds(pl.multiple_of(r * nc, nc), nc)

    def regroup(r, c):
        u3_ref[rows(r), :] = u_ref[pl.ds(r, nc, stride=r_steps), :]
        return c

    lax.fori_loop(0, r_steps, regroup, 0)
    h_ref[...] = jnp.dot(u3_ref[...].astype(BF16), bdb_ref[0], preferred_element_type=F32)

    half = st // 2
    for hf in range(2):
        cre_ = slice(hf * half, (hf + 1) * half)
        cim_ = slice(st + hf * half, st + (hf + 1) * half)
        are = jnp.broadcast_to(pw_ref[0, 0:1, cre_], (nc, half))
        aim = jnp.broadcast_to(pw_ref[0, 0:1, cim_], (nc, half))

        def step(r, carry, cre_=cre_, cim_=cim_, are=are, aim=aim):
            hre, him = _cmul_add(are, aim, carry[0], carry[1], h_ref[rows(r), cre_], h_ref[rows(r), cim_])
            h_ref[rows(r), cre_] = hre
            h_ref[rows(r), cim_] = him
            return hre, him

        lax.fori_loop(1, r_steps, step, (h_ref[0:nc, cre_], h_ref[0:nc, cim_]))

    ends = h_ref[(r_steps - 1) * nc:r_steps * nc, :]
    are, aim = pw_ref[0, r_steps - 1:r_steps, re], pw_ref[0, r_steps - 1:r_steps, im]
    cre, cim = hc_ref[0:1, re], hc_ref[0:1, im]
    for c in range(nc):
        car_ref[c:c + 1, re] = cre
        car_ref[c:c + 1, im] = cim
        cre, cim = _cmul_add(are, aim, cre, cim, ends[c:c + 1, re], ends[c:c + 1, im])
    hc_ref[:, re] = jnp.broadcast_to(cre, (8, st))
    hc_ref[:, im] = jnp.broadcast_to(cim, (8, st))
    hl_ref[0] = hc_ref[...]

    def fix(r, c):
        pre, pim = pw_ref[0, pl.ds(r, 1), re], pw_ref[0, pl.ds(r, 1), im]
        hre, him = _cmul_add(pre, pim, car_ref[:, re], car_ref[:, im], h_ref[rows(r), re], h_ref[rows(r), im])
        h_ref[rows(r), re] = hre
        h_ref[rows(r), im] = him
        return c

    lax.fori_loop(0, r_steps, fix, 0)

    y = (jnp.dot(h_ref[:, re].astype(BF16), cre_ref[0], preferred_element_type=F32)
         - jnp.dot(h_ref[:, im].astype(BF16), cim_ref[0], preferred_element_type=F32)
         + d_ref[...] * u3_ref[...])
    u3_ref[...] = _gelu(y)

    def ungroup(r, c):
        yg_ref[pl.ds(r, nc, stride=r_steps), :] = u3_ref[rows(r), :]
        return c

    lax.fori_loop(0, r_steps, ungroup, 0)


def _s5_prompt(p, bdb, cre, cim, d, pw):
    seq = p.shape[0]
    tseg = SSM_R * SSM_NC
    st2 = 2 * SSM_LG_STATES
    lgs = SSM_LANE_GROUPS
    return pl.pallas_call(
        _s5_prompt_body,
        grid=(lgs, seq // tseg),
        in_specs=[
            pl.BlockSpec((tseg, LANE), lambda g, t: (t, COL_CU + g)),
            pl.BlockSpec((1, LANE, st2), lambda g, t: (g, 0, 0)),
            pl.BlockSpec((1, SSM_LG_STATES, LANE), lambda g, t: (g, 0, 0)),
            pl.BlockSpec((1, SSM_LG_STATES, LANE), lambda g, t: (g, 0, 0)),
            pl.BlockSpec((1, LANE), lambda g, t: (0, g)),
            pl.BlockSpec((1, SSM_R, st2), lambda g, t: (g, 0, 0)),
        ],
        out_specs=[
            pl.BlockSpec((tseg, LANE), lambda g, t: (t, g)),
            pl.BlockSpec((1, 8, st2), lambda g, t: (g, 0, 0)),
        ],
        out_shape=[jax.ShapeDtypeStruct((seq, C_WIDTH), F32), jax.ShapeDtypeStruct((lgs, 8, st2), F32)],
        scratch_shapes=[
            pltpu.VMEM((tseg, LANE), F32),
            pltpu.VMEM((tseg, st2), F32),
            pltpu.VMEM((SSM_NC, st2), F32),
            pltpu.VMEM((8, st2), F32),
        ],
        compiler_params=_params(("arbitrary", "arbitrary"), 40),
        name="s5_prompt",
    )(p, bdb, cre, cim, d, pw)


def _s5_sample_body(u_ref, hre_ref, him_ref, bdb_ref, cre_ref, cim_ref, d_ref, pw_ref,
                    yg_ref, ore_ref, oim_ref, h_ref, *, db):
    st = SSM_LG_STATES
    re, im = slice(0, st), slice(st, 2 * st)
    u = u_ref[...]
    x = jnp.dot(u.astype(BF16), bdb_ref[0], preferred_element_type=F32)
    are, aim = pw_ref[0, 0:1, re], pw_ref[0, 0:1, im]
    hre, him = hre_ref[...], him_ref[...]
    for j in range(DEC_SEQ):
        rj = slice(j * db, (j + 1) * db)
        hre, him = _cmul_add(are, aim, hre, him, x[rj, re], x[rj, im])
        h_ref[rj, re] = hre
        h_ref[rj, im] = him
    ore_ref[...] = hre
    oim_ref[...] = him
    y = (jnp.dot(h_ref[:, re].astype(BF16), cre_ref[0], preferred_element_type=F32)
         - jnp.dot(h_ref[:, im].astype(BF16), cim_ref[0], preferred_element_type=F32)
         + d_ref[...] * u)
    yg_ref[...] = _gelu(y)


def _s5_sample(p, h0re, h0im, bdb, cre, cim, d, pw):
    t = p.shape[0]
    db = t // DEC_SEQ
    st = SSM_LG_STATES
    lgs = SSM_LANE_GROUPS
    return pl.pallas_call(
        functools.partial(_s5_sample_body, db=db),
        grid=(lgs,),
        in_specs=[
            pl.BlockSpec((t, LANE), lambda g: (0, COL_CU + g)),
            pl.BlockSpec((db, st), lambda g: (0, g)),
            pl.BlockSpec((db, st), lambda g: (0, g)),
            pl.BlockSpec((1, LANE, 2 * st), lambda g: (g, 0, 0)),
            pl.BlockSpec((1, st, LANE), lambda g: (g, 0, 0)),
            pl.BlockSpec((1, st, LANE), lambda g: (g, 0, 0)),
            pl.BlockSpec((1, LANE), lambda g: (0, g)),
            pl.BlockSpec((1, SSM_R, 2 * st), lambda g: (g, 0, 0)),
        ],
        out_specs=[
            pl.BlockSpec((t, LANE), lambda g: (0, g)),
            pl.BlockSpec((db, st), lambda g: (0, g)),
            pl.BlockSpec((db, st), lambda g: (0, g)),
        ],
        out_shape=[jax.ShapeDtypeStruct((t, C_WIDTH), F32),
                   jax.ShapeDtypeStruct((db, lgs * st), F32), jax.ShapeDtypeStruct((db, lgs * st), F32)],
        scratch_shapes=[pltpu.VMEM((t, 2 * st), F32)],
        compiler_params=_params(("arbitrary",), 32),
        name="s5_sample",
    )(p, h0re, h0im, bdb, cre, cim, d, pw)


def _mix_out_body(a_ref, b_ref, y_ref, x_ref, wg_ref, ga_ref, gb_ref, gc_ref, wo_ref, o_ref):
    glu = jnp.dot(y_ref[...].astype(BF16), wg_ref[...], preferred_element_type=F32)
    c = glu[:, :C_WIDTH] * jax.nn.sigmoid(glu[:, C_WIDTH:])
    mixed = jnp.concatenate([
        (_rms(a_ref[...]) * ga_ref[...]).astype(BF16),
        (_rms(b_ref[...]) * gb_ref[...]).astype(BF16),
        (_rms(c) * gc_ref[...]).astype(BF16)], axis=1)
    o_ref[...] = x_ref[...] + jnp.dot(mixed, wo_ref[...], preferred_element_type=F32)


def _mix_out(a, b, y, x, w_glu, g_a, g_b, g_c, w_out, layer, tm=512):
    t = x.shape[0]
    row = lambda w: pl.BlockSpec((tm, w), lambda i: (i, 0))
    whole = lambda arr: pl.BlockSpec(arr.shape, lambda i: (0, 0))
    of_layer = lambda arr: pl.BlockSpec((None,) + arr.shape[1:], lambda i: (layer, 0, 0))
    return pl.pallas_call(
        _mix_out_body,
        grid=(t // tm,),
        in_specs=[row(A_WIDTH), row(B_WIDTH), row(C_WIDTH), row(D_MODEL),
                  of_layer(w_glu), whole(g_a), whole(g_b), whole(g_c), of_layer(w_out)],
        out_specs=row(D_MODEL),
        out_shape=jax.ShapeDtypeStruct((t, D_MODEL), F32),
        compiler_params=_params(("arbitrary",), 56),
        name="mix_out",
    )(a, b, y, x, w_glu, g_a, g_b, g_c, w_out)


FFN_HALO = 16


def _conv_gate(ug, uv, cwg, cwv, cbg, cbv, taps):
    def conv(u, cw, cb):
        return ((cb + cw[0:1, :] * taps[0](u)) + cw[1:2, :] * taps[1](u)) + cw[2:3, :] * taps[2](u)
    gate = conv(ug, cwg, cbg)
    return (gate * jax.nn.sigmoid(gate)) * conv(uv, cwv, cbv)


def _finish(o_ref, gf_ref, is_last_tile, final_norm):
    if final_norm:
        @pl.when(is_last_tile)
        def _():
            o_ref[...] = _rms(o_ref[...]) * gf_ref[...]


def _ffn_prompt_body(x_ref, halo_ref, g_ref, wg_ref, wv_ref, cwg_ref, cwv_ref, cbg_ref, cbv_ref, wd_ref, gf_ref,
                     o_ref, tail_ref, h_ref, *, tm, final_norm):
    i, f = pl.program_id(0), pl.program_id(1)
    hl = FFN_HALO

    @pl.when(f == 0)
    def _():
        h_ref[hl:, :] = (_rms(x_ref[...]) * g_ref[...]).astype(BF16)
        prev = _rms(halo_ref[...]) * g_ref[...]
        h_ref[0:hl, :] = jnp.where(i > 0, prev, 0.0).astype(BF16)
        o_ref[...] = x_ref[...]

    h = h_ref[...]
    taps = (lambda u: u[hl - 2:hl - 2 + tm], lambda u: u[hl - 1:hl - 1 + tm], lambda u: u[hl:])
    ug = jnp.dot(h, wg_ref[...], preferred_element_type=F32)
    uv = jnp.dot(h, wv_ref[...], preferred_element_type=F32)
    act = _conv_gate(ug, uv, cwg_ref[...], cwv_ref[...], cbg_ref[...], cbv_ref[...], taps)
    o_ref[...] += jnp.dot(act.astype(BF16), wd_ref[...], preferred_element_type=F32)
    tail_ref[0] = ug[hl + tm - 8:]
    tail_ref[1] = uv[hl + tm - 8:]
    _finish(o_ref, gf_ref, f == pl.num_programs(1) - 1, final_norm)


def _ffn_prompt(x, g, w_up, conv_w, conv_b, w_down, g_final, layer, final_norm, tm=512, tf=512):
    t = x.shape[0]
    nf = D_FF // tf
    hl = FFN_HALO
    cb = conv_b.reshape(conv_b.shape[0], 1, 2 * D_FF)
    return pl.pallas_call(
        functools.partial(_ffn_prompt_body, tm=tm, final_norm=final_norm),
        grid=(t // tm, nf),
        in_specs=[
            pl.BlockSpec((tm, D_MODEL), lambda i, f: (i, 0)),
            pl.BlockSpec((hl, D_MODEL), lambda i, f: (jnp.maximum(i * (tm // hl) - 1, 0), 0)),
            pl.BlockSpec((1, D_MODEL), lambda i, f: (0, 0)),
            pl.BlockSpec((None, D_MODEL, tf), lambda i, f: (layer, 0, f)),
            pl.BlockSpec((None, D_MODEL, tf), lambda i, f: (layer, 0, nf + f)),
            pl.BlockSpec((None, 3, tf), lambda i, f: (layer, 0, f)),
            pl.BlockSpec((None, 3, tf), lambda i, f: (layer, 0, nf + f)),
            pl.BlockSpec((None, 1, tf), lambda i, f: (layer, 0, f)),
            pl.BlockSpec((None, 1, tf), lambda i, f: (layer, 0, nf + f)),
            pl.BlockSpec((None, tf, D_MODEL), lambda i, f: (layer, f, 0)),
            pl.BlockSpec((1, D_MODEL), lambda i, f: (0, 0)),
        ],
        out_specs=[
            pl.BlockSpec((tm, D_MODEL), lambda i, f: (i, 0), pipeline_mode=pl.Buffered(1)),
            pl.BlockSpec((None, 2, 8, tf), lambda i, f: (i, 0, 0, f)),
        ],
        out_shape=[jax.ShapeDtypeStruct((t, D_MODEL), F32), jax.ShapeDtypeStruct((t // tm, 2, 8, D_FF), F32)],
        scratch_shapes=[pltpu.VMEM((hl + tm, D_MODEL), BF16)],
        compiler_params=_params(("arbitrary", "arbitrary"), 56),
        name="ffn_prompt",
    )(x, x, g, w_up, w_up, conv_w, conv_w, cb, cb, w_down, g_final)


def _ffn_sample_body(x_ref, g_ref, sg_ref, sv_ref, wg_ref, wv_ref, cwg_ref, cwv_ref, cbg_ref, cbv_ref, wd_ref, gf_ref,
                     o_ref, tg_ref, tv_ref, h_ref, *, db, final_norm):
    f = pl.program_id(0)
    t = DEC_SEQ * db

    @pl.when(f == 0)
    def _():
        h_ref[...] = (_rms(x_ref[...]) * g_ref[...]).astype(BF16)
        o_ref[...] = x_ref[...]

    h = h_ref[...]
    ug = jnp.dot(h, wg_ref[...], preferred_element_type=F32)
    uv = jnp.dot(h, wv_ref[...], preferred_element_type=F32)
    pg = jnp.concatenate([sg_ref[:, 0, :], sg_ref[:, 1, :], ug], axis=0)
    pv = jnp.concatenate([sv_ref[:, 0, :], sv_ref[:, 1, :], uv], axis=0)
    taps = (lambda u: u[0:t], lambda u: u[db:db + t], lambda u: u[2 * db:2 * db + t])
    act = _conv_gate(pg, pv, cwg_ref[...], cwv_ref[...], cbg_ref[...], cbv_ref[...], taps)
    o_ref[...] += jnp.dot(act.astype(BF16), wd_ref[...], preferred_element_type=F32)

    for j in range(2):
        rows = slice((DEC_SEQ - 2 + j) * db, (DEC_SEQ - 1 + j) * db)
        tg_ref[:, j, :] = ug[rows]
        tv_ref[:, j, :] = uv[rows]
    _finish(o_ref, gf_ref, f == pl.num_programs(0) - 1, final_norm)


def _ffn_sample(x, g, state, w_up, conv_w, conv_b, w_down, g_final, layer, final_norm, tf=512):
    t = x.shape[0]
    db = t // DEC_SEQ
    nf = D_FF // tf
    cb = conv_b.reshape(conv_b.shape[0], 1, 2 * D_FF)
    return pl.pallas_call(
        functools.partial(_ffn_sample_body, db=db, final_norm=final_norm),
        grid=(nf,),
        in_specs=[
            pl.BlockSpec((t, D_MODEL), lambda f: (0, 0)),
            pl.BlockSpec((1, D_MODEL), lambda f: (0, 0)),
            pl.BlockSpec((None, db, 2, tf), lambda f: (layer, 0, 0, f)),
            pl.BlockSpec((None, db, 2, tf), lambda f: (layer, 0, 0, nf + f)),
            pl.BlockSpec((None, D_MODEL, tf), lambda f: (layer, 0, f)),
            pl.BlockSpec((None, D_MODEL, tf), lambda f: (layer, 0, nf + f)),
            pl.BlockSpec((None, 3, tf), lambda f: (layer, 0, f)),
            pl.BlockSpec((None, 3, tf), lambda f: (layer, 0, nf + f)),
            pl.BlockSpec((None, 1, tf), lambda f: (layer, 0, f)),
            pl.BlockSpec((None, 1, tf), lambda f: (layer, 0, nf + f)),
            pl.BlockSpec((None, tf, D_MODEL), lambda f: (layer, f, 0)),
            pl.BlockSpec((1, D_MODEL), lambda f: (0, 0)),
        ],
        out_specs=[
            pl.BlockSpec((t, D_MODEL), lambda f: (0, 0)),
            pl.BlockSpec((db, 2, tf), lambda f: (0, 0, f)),
            pl.BlockSpec((db, 2, tf), lambda f: (0, 0, f)),
        ],
        out_shape=[jax.ShapeDtypeStruct((t, D_MODEL), F32)] + [jax.ShapeDtypeStruct((db, 2, D_FF), F32)] * 2,
        scratch_shapes=[pltpu.VMEM((t, D_MODEL), BF16)],
        compiler_params=_params(("arbitrary",), 48),
        name="ffn_sample",
    )(x, g, state, state, w_up, w_up, conv_w, conv_w, cb, cb, w_down, g_final)


def kernel(x_prompt, x_sample, cache_win_k, cache_win_v, state_ssm_re, state_ssm_im, state_ffn_conv,
           g_mix, w_in, g_out_a, g_out_b, g_out_c, gmlp_gv, gmlp_ws, gmlp_bs, ssm_a_re, ssm_a_im,
           ssm_log_dt, ssm_b_re, ssm_b_im, ssm_c_re, ssm_c_im, ssm_d, ssm_w_glu, w_out, g_ffn, w_up,
           conv_w, conv_b, w_down, g_final):
    depth = w_in.shape[0]
    bsz, seq, _ = x_prompt.shape
    db, ds, _ = x_sample.shape
    assert bsz == 1 and ds == DEC_SEQ and seq % (SSM_R * SSM_NC) == 0 and seq >= A_WINDOW_MAX
    n_states = C_GROUPS * SSM_P
    row = lambda v: v.reshape(1, -1)

    g_fin = row(g_final)
    w_in_b, w_glu_b, w_out_b = w_in.astype(BF16), ssm_w_glu.astype(BF16), w_out.astype(BF16)
    w_up_b, w_down_b = w_up.astype(BF16), w_down.astype(BF16)
    xp = x_prompt.reshape(seq, D_MODEL)
    xs = jnp.transpose(x_sample, (1, 0, 2)).reshape(ds * db, D_MODEL)
    outs = {k: [] for k in ("kp", "vp", "ks", "vs", "gv", "rp", "ip", "rs", "is", "cp", "cs")}

    for l in range(depth):
        pw, bdb = _ssm_prep(ssm_a_re[l], ssm_a_im[l], ssm_log_dt[l], ssm_b_re[l], ssm_b_im[l])
        c_re, c_im = _ssm_out_matrix(ssm_c_re[l]), _ssm_out_matrix(ssm_c_im[l])
        d_row = row(ssm_d[l])
        gains = (row(g_out_a[l]), row(g_out_b[l]), row(g_out_c[l]))
        last = l == depth - 1

        p = _proj(xp, row(g_mix[l]), w_in_b, l, tm=min(1024, seq))
        a = _attn_prompt(p)
        b = _gmlp_prompt(p, row(gmlp_gv[l]), gmlp_ws[l], gmlp_bs[l])
        yg, h_last = _s5_prompt(p, bdb, c_re, c_im, d_row, pw)
        xm = _mix_out(a, b, yg, xp, w_glu_b, *gains, w_out_b, l)
        xp, tail = _ffn_prompt(xm, row(g_ffn[l]), w_up_b, conv_w, conv_b, w_down_b, g_fin, l,
                               final_norm=last, tm=min(1024, seq))
        keep = min(A_WINDOW_MAX, seq)
        outs["kp"].append(p[seq - keep:, A_WIDTH:2 * A_WIDTH].reshape(1, keep, A_HEADS, HEAD_DIM))
        outs["vp"].append(p[seq - keep:, 2 * A_WIDTH:3 * A_WIDTH].reshape(1, keep, A_HEADS, HEAD_DIM))
        st = SSM_LG_STATES
        outs["rp"].append(h_last[:, 0, :st].reshape(1, C_GROUPS, SSM_P))
        outs["ip"].append(h_last[:, 0, st:].reshape(1, C_GROUPS, SSM_P))
        outs["cp"].append(jnp.transpose(tail[-1, :, 6:8, :], (1, 0, 2)).reshape(1, 2, 2 * D_FF))

        ps = _proj(xs, row(g_mix[l]), w_in_b, l, tm=ds * db)

        def heads(cols):
            return jnp.transpose(cols.reshape(ds, db, A_HEADS, HEAD_DIM), (1, 0, 2, 3)).reshape(
                db, ds * A_HEADS, HEAD_DIM)

        q3, k3, v3 = (heads(ps[:, i * A_WIDTH:(i + 1) * A_WIDTH]) for i in range(3))
        a3 = _attn_sample(q3, k3, v3, cache_win_k, cache_win_v, l)
        a_s = jnp.transpose(a3.reshape(db, ds, A_WIDTH), (1, 0, 2)).reshape(ds * db, A_WIDTH)
        b_s, gv_s = _gmlp_sample(ps, row(gmlp_gv[l]), gmlp_ws[l], gmlp_bs[l])
        yg_s, hre, him = _s5_sample(ps, state_ssm_re[l].reshape(db, n_states), state_ssm_im[l].reshape(db, n_states),
                                    bdb, c_re, c_im, d_row, pw)
        xm_s = _mix_out(a_s, b_s, yg_s, xs, w_glu_b, *gains, w_out_b, l, tm=ds * db)
        xs, tail_g, tail_v = _ffn_sample(xm_s, row(g_ffn[l]), state_ffn_conv, w_up_b, conv_w, conv_b, w_down_b,
                                         g_fin, l, final_norm=last)
        outs["ks"].append(k3.reshape(db, ds, A_HEADS, HEAD_DIM))
        outs["vs"].append(v3.reshape(db, ds, A_HEADS, HEAD_DIM))
        outs["gv"].append(jnp.transpose(gv_s.reshape(ds, db, B_WIDTH), (1, 0, 2)))
        outs["rs"].append(hre.reshape(db, C_GROUPS, SSM_P))
        outs["is"].append(him.reshape(db, C_GROUPS, SSM_P))
        outs["cs"].append(jnp.concatenate([tail_g, tail_v], axis=-1))

    y_p = xp.reshape(1, seq, D_MODEL)
    y_s = jnp.transpose(xs.reshape(ds, db, D_MODEL), (1, 0, 2))
    st = lambda k: jnp.stack(outs[k])
    return (y_p, y_s, st("kp"), st("vp"), st("ks"), st("vs"), st("gv"),
            st("rp"), st("ip"), st("rs"), st("is"), st("cp"), st("cs"))
```

```python
import functools
import math

import numpy as np
import jax
import jax.numpy as jnp
from jax import lax
from jax.experimental import pallas as pl
from jax.experimental.pallas import tpu as pltpu

F32 = jnp.float32
BF16 = jnp.bfloat16

D_MODEL = 2048
A_WIDTH = 1024
A_HEADS = 8
HEAD_DIM = 128
A_DILATIONS = (1, 4, 16)
A_SPAN = 128
A_WINDOW_MAX = 2048
SAMPLE_TAIL_ROWS = 512
SAMPLE_FAR_QUARTERS = A_WINDOW_MAX // SAMPLE_TAIL_ROWS - 1
ATTN_UNROLL = {1: 16, 4: 16, 16: 8}
B_WIDTH = 512
B_GROUPS = 4
CHUNK = 128
C_WIDTH = 512
C_GROUPS = 32
C_GROUP_W = 16
SSM_P = 64
N_IN = 3 * A_WIDTH + 2 * B_WIDTH + C_WIDTH
D_FF = 5632
DEC_SEQ = 4
EPS = 1e-6
NEG = -1e30

LANE = 128
COL_Q, COL_K, COL_V = 0, A_WIDTH // LANE, 2 * A_WIDTH // LANE
COL_BU = 3 * A_WIDTH // LANE
COL_BV = COL_BU + B_WIDTH // LANE
COL_CU = COL_BV + B_WIDTH // LANE

SSM_LANE_GROUPS = 4
SSM_LG_STATES = C_GROUPS * SSM_P // SSM_LANE_GROUPS
SSM_R = 64
SSM_NC = 32

MIB = 1024 * 1024


def _params(sem, vmem_mib, flags=None):
    return pltpu.CompilerParams(dimension_semantics=sem, vmem_limit_bytes=vmem_mib * MIB, flags=flags)


def _gelu(x):
    c = math.sqrt(2.0 / math.pi)
    return x * (0.5 * (1.0 + jnp.tanh(c * (x + 0.044715 * (x * x * x)))))


def _rms(x):
    return x * lax.rsqrt(jnp.mean(x * x, axis=-1, keepdims=True) + EPS)


_NT = (((1,), (1,)), ((), ()))


def _proj_body(x_ref, g_ref, w_ref, o_ref, h_ref):
    @pl.when(pl.program_id(1) == 0)
    def _():
        h_ref[...] = (_rms(x_ref[...]) * g_ref[...]).astype(BF16)

    o_ref[...] = jnp.dot(h_ref[...], w_ref[...], preferred_element_type=F32)


def _proj(x, g, w, layer, tm, tn=1536):
    t = x.shape[0]
    return pl.pallas_call(
        _proj_body,
        grid=(t // tm, N_IN // tn),
        in_specs=[
            pl.BlockSpec((tm, D_MODEL), lambda i, j: (i, 0)),
            pl.BlockSpec((1, D_MODEL), lambda i, j: (0, 0)),
            pl.BlockSpec((None, D_MODEL, tn), lambda i, j: (layer, 0, j)),
        ],
        out_specs=pl.BlockSpec((tm, tn), lambda i, j: (i, j)),
        out_shape=jax.ShapeDtypeStruct((t, N_IN), F32),
        scratch_shapes=[pltpu.VMEM((tm, D_MODEL), BF16)],
        compiler_params=_params(("arbitrary", "arbitrary"), 56),
        name="proj_in",
    )(x, g, w)


def _attn_prompt_body(q_ref, k_ref, v_ref, o_ref, acc_ref, m_ref, l_ref, *, seq):
    scale = math.log2(math.e) / math.sqrt(HEAD_DIM)
    ii = lax.broadcasted_iota(jnp.int32, (A_SPAN, 2 * A_SPAN), 0)
    jj = lax.broadcasted_iota(jnp.int32, (A_SPAN, 2 * A_SPAN), 1)
    band = (jj >= ii) & (jj <= ii + A_SPAN)
    band_cur = band & (jj >= A_SPAN)
    blk = (A_SPAN, A_SPAN)

    for order, dil in enumerate(sorted(A_DILATIONS, reverse=True)):
        nb = seq // (A_SPAN * dil)
        step = A_SPAN * dil

        def rows(start, dil=dil):
            if dil == 1:
                return pl.ds(pl.multiple_of(start, A_SPAN), A_SPAN)
            return pl.ds(start, A_SPAN, stride=dil)

        def runs(run_list, rows=rows, step=step, first=order == 0):
            work = []
            for r, n0, count in run_list:
                static_start = isinstance(n0, int)
                if not (static_start and n0 == 0):
                    prev = r + jnp.maximum(n0 - 1, 0) * step
                    kp, vp = k_ref[rows(prev), :].astype(BF16), v_ref[rows(prev), :].astype(BF16)
                for u in range(count):
                    base = r + (n0 + u) * step
                    q = (q_ref[rows(base), :] * scale).astype(BF16)
                    kc, vc = k_ref[rows(base), :].astype(BF16), v_ref[rows(base), :].astype(BF16)
                    if u == 0 and static_start and n0 == 0:
                        kp, vp, mask = kc, vc, band_cur
                    elif u == 0 and not static_start:
                        mask = band & ((jj >= A_SPAN) | (n0 > 0))
                    else:
                        mask = band
                    s = lax.dot_general(q, jnp.concatenate([kp, kc], axis=0), _NT,
                                        preferred_element_type=F32)
                    s = jnp.where(mask, s, NEG)
                    v2 = jnp.concatenate([vp, vc], axis=0)
                    m_blk = jnp.max(s, axis=1, keepdims=True)
                    if first:
                        m_new = jnp.broadcast_to(m_blk, blk)
                        p = jnp.exp2(s - m_blk)
                        l_new = jnp.broadcast_to(jnp.sum(p, axis=1, keepdims=True), blk)
                        acc_new = jnp.dot(p.astype(BF16), v2, preferred_element_type=F32)
                    else:
                        m_old = m_ref[rows(base), :]
                        m_new = jnp.maximum(m_old, m_blk)
                        alpha = jnp.exp2(m_old - m_new)
                        p = jnp.exp2(s - jnp.concatenate([m_new, m_new], axis=1))
                        l_new = alpha * l_ref[rows(base), :] + jnp.sum(p, axis=1, keepdims=True)
                        acc_new = alpha * acc_ref[rows(base), :] + jnp.dot(p.astype(BF16), v2,
                                                                           preferred_element_type=F32)
                    work.append((base, m_new, l_new, acc_new))
                    kp, vp = kc, vc
            for base, m_new, l_new, acc_new in work:
                m_ref[rows(base), :] = m_new
                l_ref[rows(base), :] = l_new
                acc_ref[rows(base), :] = acc_new

        unroll = min(ATTN_UNROLL[dil], nb * dil)
        if unroll <= nb:
            per_res = nb // unroll

            def trip(i, c, runs=runs, unroll=unroll, per_res=per_res):
                runs([(i // per_res, (i % per_res) * unroll, unroll)])
                return c
        else:
            res_per_trip = unroll // nb

            def trip(i, c, runs=runs, nb=nb, res_per_trip=res_per_trip):
                runs([(i * res_per_trip + j, 0, nb) for j in range(res_per_trip)])
                return c

        lax.fori_loop(0, nb * dil // unroll, trip, 0)

    o_ref[...] = acc_ref[...] / l_ref[...]


def _attn_prompt(p):
    seq = p.shape[0]
    blk = (seq, HEAD_DIM)
    return pl.pallas_call(
        functools.partial(_attn_prompt_body, seq=seq),
        grid=(A_HEADS,),
        in_specs=[
            pl.BlockSpec(blk, lambda h: (0, COL_Q + h)),
            pl.BlockSpec(blk, lambda h: (0, COL_K + h)),
            pl.BlockSpec(blk, lambda h: (0, COL_V + h)),
        ],
        out_specs=pl.BlockSpec(blk, lambda h: (0, h)),
        out_shape=jax.ShapeDtypeStruct((seq, A_WIDTH), F32),
        scratch_shapes=[pltpu.VMEM(blk, F32)] * 3,
        compiler_params=_params(("arbitrary",), 56),
        name="attn_prompt",
    )(p, p, p)


def _sample_attn_masks():
    j = np.arange(DEC_SEQ)[:, None, None, None]
    h = np.arange(A_HEADS)[None, :, None, None]
    hk = np.arange(A_HEADS)[None, None, None, :]
    same = (h == hk)
    i = np.arange(512)[None, None, :, None]
    cnt_l = same * ((i >= 384 + j).astype(np.int32) + ((i - j) % 4 == 0).astype(np.int32))
    jn = np.arange(DEC_SEQ)[None, None, :, None]
    cnt_n = same * ((jn <= j).astype(np.int32) + 2 * (jn == j).astype(np.int32))
    g = np.arange(128)[None, None, :, None]
    head_match = np.broadcast_to(same[0:1], (1, A_HEADS, 128, A_HEADS))
    return (cnt_l.reshape(DEC_SEQ * A_HEADS, 512 * A_HEADS).astype(np.float32),
            cnt_n.reshape(DEC_SEQ * A_HEADS, DEC_SEQ * A_HEADS).astype(np.float32),
            (head_match + 0 * g).reshape(A_HEADS, 128 * A_HEADS).astype(np.float32))


def _attn_sample_body(q_ref, k_ref, v_ref, *refs):
    nh = A_HEADS
    nfar = SAMPLE_FAR_QUARTERS * DEC_SEQ
    k_far, kl_ref = refs[:nfar], refs[nfar]
    v_far, vl_ref = refs[nfar + 1:2 * nfar + 1], refs[2 * nfar + 1]
    cl_ref, cn_ref, hm_ref, o_ref = refs[2 * nfar + 2:]
    q = q_ref[...] * (1.0 / math.sqrt(HEAD_DIM))
    qs = q.astype(BF16)
    cl, cn, hm = cl_ref[...], cn_ref[...], hm_ref[...]

    def stride16(far, last, res):
        near = last.reshape(32, 16, nh, HEAD_DIM)[:, res].reshape(32 * nh, HEAD_DIM)
        parts = [far[qt * DEC_SEQ + res][...].reshape(32 * nh, HEAD_DIM).astype(BF16)
                 for qt in range(SAMPLE_FAR_QUARTERS)]
        return jnp.concatenate(parts + [near.astype(BF16)], axis=0)

    k_last = kl_ref[...]
    kl = k_last.reshape(512 * nh, HEAD_DIM).astype(BF16)
    s_l = lax.dot_general(qs, kl, _NT, preferred_element_type=F32)
    s_n = lax.dot_general(qs, k_ref[...].astype(BF16), _NT, preferred_element_type=F32)
    s_a = []
    for res in range(DEC_SEQ):
        kk = stride16(k_far, k_last, res)
        s_a.append(lax.dot_general(q[res * nh:(res + 1) * nh].astype(BF16), kk, _NT,
                                   preferred_element_type=F32))
    m = jnp.max(jnp.where(cl > 0, s_l, NEG), axis=1, keepdims=True)
    m = jnp.maximum(m, jnp.max(jnp.where(cn > 0, s_n, NEG), axis=1, keepdims=True))
    m_a = jnp.concatenate([jnp.max(jnp.where(hm > 0, s, NEG), axis=1, keepdims=True) for s in s_a], axis=0)
    m = jnp.maximum(m, m_a)

    p_l = cl * jnp.exp(jnp.minimum(s_l - m, 0.0))
    p_n = cn * jnp.exp(jnp.minimum(s_n - m, 0.0))
    den = jnp.sum(p_l, axis=1, keepdims=True) + jnp.sum(p_n, axis=1, keepdims=True)
    v_last = vl_ref[...]
    vl = v_last.reshape(512 * nh, HEAD_DIM).astype(BF16)
    out = jnp.dot(p_l.astype(BF16), vl, preferred_element_type=F32)
    out = out + jnp.dot(p_n.astype(BF16), v_ref[...].astype(BF16), preferred_element_type=F32)
    o_a, d_a = [], []
    for res in range(DEC_SEQ):
        p_a = hm * jnp.exp(jnp.minimum(s_a[res] - m[res * nh:(res + 1) * nh], 0.0))
        d_a.append(jnp.sum(p_a, axis=1, keepdims=True))
        vv = stride16(v_far, v_last, res)
        o_a.append(jnp.dot(p_a.astype(BF16), vv, preferred_element_type=F32))
    out = out + jnp.concatenate(o_a, axis=0)
    den = den + jnp.concatenate(d_a, axis=0)
    o_ref[...] = out / den


def _attn_sample(q3, k3, v3, cache_k, cache_v, layer):
    db = q3.shape[0]
    depth, _, w_buf, nh, hd = cache_k.shape
    assert (w_buf, nh, hd) == (A_WINDOW_MAX, A_HEADS, HEAD_DIM) and q3.shape[1] == DEC_SEQ * A_HEADS
    cnt_l, cnt_n, head_match = _sample_attn_masks()
    rows = DEC_SEQ * A_HEADS

    nq = SAMPLE_FAR_QUARTERS

    def strided_specs():
        return [pl.BlockSpec((None, None, None, 32, None, nh, hd), functools.partial(
            lambda b, qt, res: (layer, b, qt, 0, res, 0, 0), qt=qt, res=res))
            for qt in range(nq) for res in range(DEC_SEQ)]

    last_spec = pl.BlockSpec((None, None, None, SAMPLE_TAIL_ROWS, nh, hd), lambda b: (layer, b, nq, 0, 0, 0))
    tok_spec = pl.BlockSpec((None, rows, hd), lambda b: (b, 0, 0))

    def whole(a):
        return pl.BlockSpec(a.shape, lambda b: (0, 0))

    k16 = cache_k.reshape(depth, db, nq + 1, 32, 16, nh, hd)
    v16 = cache_v.reshape(depth, db, nq + 1, 32, 16, nh, hd)
    k512 = cache_k.reshape(depth, db, nq + 1, SAMPLE_TAIL_ROWS, nh, hd)
    v512 = cache_v.reshape(depth, db, nq + 1, SAMPLE_TAIL_ROWS, nh, hd)
    nfar = nq * DEC_SEQ
    return pl.pallas_call(
        _attn_sample_body,
        grid=(db,),
        in_specs=[tok_spec, tok_spec, tok_spec] + strided_specs() + [last_spec] + strided_specs() + [last_spec]
        + [whole(cnt_l), whole(cnt_n), whole(head_match)],
        out_specs=tok_spec,
        out_shape=jax.ShapeDtypeStruct((db, rows, hd), F32),
        compiler_params=_params(("arbitrary",), 48),
        name="attn_sample",
    )(q3, k3, v3, *([k16] * nfar), k512, *([v16] * nfar), v512,
      jnp.asarray(cnt_l), jnp.asarray(cnt_n), jnp.asarray(head_match))


def _gmlp_prompt_body(bu_ref, bv_ref, gg_ref, ws_ref, bs_ref, o_ref, *, chunks):
    ri = lax.broadcasted_iota(jnp.int32, (CHUNK, CHUNK), 0)
    ci = lax.broadcasted_iota(jnp.int32, (CHUNK, CHUNK), 1)
    wm = jnp.where(ci <= ri, ws_ref[0], 0.0).astype(BF16)
    bias = bs_ref[0]
    gain = gg_ref[...]
    for c in range(chunks):
        sl = slice(c * CHUNK, (c + 1) * CHUNK)
        gv = _rms(_gelu(bv_ref[sl, :])) * gain
        mix = jnp.dot(wm, gv.astype(BF16), preferred_element_type=F32) + bias
        o_ref[sl, :] = _gelu(bu_ref[sl, :]) * mix


def _gmlp_prompt(p, gain, ws, bs, chunks=8):
    seq = p.shape[0]
    tm = chunks * CHUNK
    return pl.pallas_call(
        functools.partial(_gmlp_prompt_body, chunks=chunks),
        grid=(seq // tm, B_GROUPS),
        in_specs=[
            pl.BlockSpec((tm, LANE), lambda i, g: (i, COL_BU + g)),
            pl.BlockSpec((tm, LANE), lambda i, g: (i, COL_BV + g)),
            pl.BlockSpec((1, LANE), lambda i, g: (0, g)),
            pl.BlockSpec((1, CHUNK, CHUNK), lambda i, g: (g, 0, 0)),
            pl.BlockSpec((1, CHUNK, 1), lambda i, g: (g, 0, 0)),
        ],
        out_specs=pl.BlockSpec((tm, LANE), lambda i, g: (i, g)),
        out_shape=jax.ShapeDtypeStruct((seq, B_WIDTH), F32),
        compiler_params=_params(("arbitrary", "arbitrary"), 32),
        name="gmlp_prompt",
    )(p, p, gain, ws, bs.reshape(B_GROUPS, CHUNK, 1))


def _gmlp_sample_body(ws_ref, bs_ref, bu_ref, bv_ref, gg_ref, o_ref, gv_ref, *, db):
    g = pl.program_id(0)
    gv = _rms(_gelu(bv_ref[...])) * gg_ref[...]
    gv_ref[...] = gv
    gu = _gelu(bu_ref[...])
    for i in range(DEC_SEQ):
        mix = jnp.full((db, LANE), bs_ref[g, i], F32)
        for j in range(i + 1):
            mix = mix + ws_ref[g, i * DEC_SEQ + j] * gv[j * db:(j + 1) * db]
        o_ref[i * db:(i + 1) * db, :] = gu[i * db:(i + 1) * db] * mix


def _gmlp_sample(p, gain, ws, bs):
    t = p.shape[0]
    db = t // DEC_SEQ
    ws4 = ws[:, :DEC_SEQ, :DEC_SEQ].reshape(B_GROUPS, DEC_SEQ * DEC_SEQ)
    bs4 = bs[:, :DEC_SEQ]
    smem = pl.BlockSpec(memory_space=pltpu.SMEM)
    return pl.pallas_call(
        functools.partial(_gmlp_sample_body, db=db),
        grid=(B_GROUPS,),
        in_specs=[
            smem, smem,
            pl.BlockSpec((t, LANE), lambda g: (0, COL_BU + g)),
            pl.BlockSpec((t, LANE), lambda g: (0, COL_BV + g)),
            pl.BlockSpec((1, LANE), lambda g: (0, g)),
        ],
        out_specs=[pl.BlockSpec((t, LANE), lambda g: (0, g))] * 2,
        out_shape=[jax.ShapeDtypeStruct((t, B_WIDTH), F32)] * 2,
        compiler_params=_params(("arbitrary",), 32),
        name="gmlp_sample",
    )(ws4, bs4, p, p, gain)


def _ssm_prep_body(lre_ref, lim_ref, ldt_ref, bre_ref, bim_ref, pwre_ref, pwim_ref, bbre_ref, bbim_ref):
    lre, lim = lre_ref[...], lim_ref[...]
    dt = jnp.exp(ldt_ref[...])
    k = lax.broadcasted_iota(jnp.int32, pwre_ref.shape, 0).astype(F32) + 1.0
    mag = jnp.exp(k * (dt * lre))
    ang = k * (dt * lim)
    pwre = mag * jnp.cos(ang)
    pwim = mag * jnp.sin(ang)
    pwre_ref[...] = pwre
    pwim_ref[...] = pwim
    xr, xi = pwre[0:1] - 1.0, pwim[0:1]
    den = lre * lre + lim * lim
    cr = (xr * lre + xi * lim) / den
    ci = (xi * lre - xr * lim) / den
    bre, bim = bre_ref[...], bim_ref[...]
    bbre_ref[...] = cr * bre - ci * bim
    bbim_ref[...] = cr * bim + ci * bre


def _ssm_prep(a_re, a_im, log_dt, b_re, b_im):
    n = C_GROUPS * SSM_P
    row = lambda a: a.reshape(1, n)
    bt = lambda b: jnp.transpose(b, (2, 0, 1)).reshape(C_GROUP_W, n)
    shapes = [jax.ShapeDtypeStruct((SSM_R, n), F32)] * 2 + [jax.ShapeDtypeStruct((C_GROUP_W, n), F32)] * 2
    pwre, pwim, bbre, bbim = pl.pallas_call(
        _ssm_prep_body, out_shape=shapes, name="ssm_prep",
    )(row(a_re), row(a_im), row(jnp.repeat(log_dt, SSM_P)), bt(b_re), bt(b_im))
    lg, st = SSM_LANE_GROUPS, SSM_LG_STATES
    split = lambda t: jnp.transpose(t.reshape(t.shape[0], lg, st), (1, 0, 2))
    pw = jnp.concatenate([split(pwre), split(pwim)], axis=-1)
    eye = jnp.eye(C_GROUPS // lg, dtype=F32)

    def blockdiag(bb):
        r = bb.reshape(C_GROUP_W, lg, C_GROUPS // lg, SSM_P)
        return jnp.einsum('ab,mlbp->lambp', eye, r).reshape(lg, LANE, st)

    bdb = jnp.concatenate([blockdiag(bbre), blockdiag(bbim)], axis=-1).astype(BF16)
    return pw, bdb


def _ssm_out_matrix(c):
    lg = SSM_LANE_GROUPS
    eye = jnp.eye(C_GROUPS // lg, dtype=F32)
    r = c.reshape(lg, C_GROUPS // lg, C_GROUP_W, SSM_P)
    return jnp.einsum('ab,lanp->lapbn', eye, r).reshape(lg, SSM_LG_STATES, LANE).astype(BF16)


def _cmul_add(are, aim, hre, him, xre, xim):
    return are * hre - aim * him + xre, are * him + aim * hre + xim


def _s5_prompt_body(u_ref, bdb_ref, cre_ref, cim_ref, d_ref, pw_ref, yg_ref, hl_ref,
                    u3_ref, h_ref, car_ref, hc_ref):
    r_steps, nc, st = SSM_R, SSM_NC, SSM_LG_STATES
    re, im = slice(0, st), slice(st, 2 * st)

    @pl.when(pl.program_id(1) == 0)
    def _():
        hc_ref[...] = jnp.zeros(hc_ref.shape, F32)

    def rows(r):
        return pl.ds(pl.multiple_of(r * nc, nc), nc)

    def regroup(r, c):
        u3_ref[rows(r), :] = u_ref[pl.ds(r, nc, stride=r_steps), :]
        return c

    lax.fori_loop(0, r_steps, regroup, 0)
    h_ref[...] = jnp.dot(u3_ref[...].astype(BF16), bdb_ref[0], preferred_element_type=F32)

    half = st // 2
    for hf in range(2):
        cre_ = slice(hf * half, (hf + 1) * half)
        cim_ = slice(st + hf * half, st + (hf + 1) * half)
        are = jnp.broadcast_to(pw_ref[0, 0:1, cre_], (nc, half))
        aim = jnp.broadcast_to(pw_ref[0, 0:1, cim_], (nc, half))

        def step(r, carry, cre_=cre_, cim_=cim_, are=are, aim=aim):
            hre, him = _cmul_add(are, aim, carry[0], carry[1], h_ref[rows(r), cre_], h_ref[rows(r), cim_])
            h_ref[rows(r), cre_] = hre
            h_ref[rows(r), cim_] = him
            return hre, him

        lax.fori_loop(1, r_steps, step, (h_ref[0:nc, cre_], h_ref[0:nc, cim_]))

    ends = h_ref[(r_steps - 1) * nc:r_steps * nc, :]
    are, aim = pw_ref[0, r_steps - 1:r_steps, re], pw_ref[0, r_steps - 1:r_steps, im]
    cre, cim = hc_ref[0:1, re], hc_ref[0:1, im]
    for c in range(nc):
        car_ref[c:c + 1, re] = cre
        car_ref[c:c + 1, im] = cim
        cre, cim = _cmul_add(are, aim, cre, cim, ends[c:c + 1, re], ends[c:c + 1, im])
    hc_ref[:, re] = jnp.broadcast_to(cre, (8, st))
    hc_ref[:, im] = jnp.broadcast_to(cim, (8, st))
    hl_ref[0] = hc_ref[...]

    def fix(r, c):
        pre, pim = pw_ref[0, pl.ds(r, 1), re], pw_ref[0, pl.ds(r, 1), im]
        hre, him = _cmul_add(pre, pim, car_ref[:, re], car_ref[:, im], h_ref[rows(r), re], h_ref[rows(r), im])
        h_ref[rows(r), re] = hre
        h_ref[rows(r), im] = him
        return c

    lax.fori_loop(0, r_steps, fix, 0)

    y = (jnp.dot(h_ref[:, re].astype(BF16), cre_ref[0], preferred_element_type=F32)
         - jnp.dot(h_ref[:, im].astype(BF16), cim_ref[0], preferred_element_type=F32)
         + d_ref[...] * u3_ref[...])
    u3_ref[...] = _gelu(y)

    def ungroup(r, c):
        yg_ref[pl.ds(r, nc, stride=r_steps), :] = u3_ref[rows(r), :]
        return c

    lax.fori_loop(0, r_steps, ungroup, 0)


def _s5_prompt(p, bdb, cre, cim, d, pw):
    seq = p.shape[0]
    tseg = SSM_R * SSM_NC
    st2 = 2 * SSM_LG_STATES
    lgs = SSM_LANE_GROUPS
    return pl.pallas_call(
        _s5_prompt_body,
        grid=(lgs, seq // tseg),
        in_specs=[
            pl.BlockSpec((tseg, LANE), lambda g, t: (t, COL_CU + g)),
            pl.BlockSpec((1, LANE, st2), lambda g, t: (g, 0, 0)),
            pl.BlockSpec((1, SSM_LG_STATES, LANE), lambda g, t: (g, 0, 0)),
            pl.BlockSpec((1, SSM_LG_STATES, LANE), lambda g, t: (g, 0, 0)),
            pl.BlockSpec((1, LANE), lambda g, t: (0, g)),
            pl.BlockSpec((1, SSM_R, st2), lambda g, t: (g, 0, 0)),
        ],
        out_specs=[
            pl.BlockSpec((tseg, LANE), lambda g, t: (t, g)),
            pl.BlockSpec((1, 8, st2), lambda g, t: (g, 0, 0)),
        ],
        out_shape=[jax.ShapeDtypeStruct((seq, C_WIDTH), F32), jax.ShapeDtypeStruct((lgs, 8, st2), F32)],
        scratch_shapes=[
            pltpu.VMEM((tseg, LANE), F32),
            pltpu.VMEM((tseg, st2), F32),
            pltpu.VMEM((SSM_NC, st2), F32),
            pltpu.VMEM((8, st2), F32),
        ],
        compiler_params=_params(("arbitrary", "arbitrary"), 40),
        name="s5_prompt",
    )(p, bdb, cre, cim, d, pw)


def _s5_sample_body(u_ref, hre_ref, him_ref, bdb_ref, cre_ref, cim_ref, d_ref, pw_ref,
                    yg_ref, ore_ref, oim_ref, h_ref, *, db):
    st = SSM_LG_STATES
    re, im = slice(0, st), slice(st, 2 * st)
    u = u_ref[...]
    x = jnp.dot(u.astype(BF16), bdb_ref[0], preferred_element_type=F32)
    are, aim = pw_ref[0, 0:1, re], pw_ref[0, 0:1, im]
    hre, him = hre_ref[...], him_ref[...]
    for j in range(DEC_SEQ):
        rj = slice(j * db, (j + 1) * db)
        hre, him = _cmul_add(are, aim, hre, him, x[rj, re], x[rj, im])
        h_ref[rj, re] = hre
        h_ref[rj, im] = him
    ore_ref[...] = hre
    oim_ref[...] = him
    y = (jnp.dot(h_ref[:, re].astype(BF16), cre_ref[0], preferred_element_type=F32)
         - jnp.dot(h_ref[:, im].astype(BF16), cim_ref[0], preferred_element_type=F32)
         + d_ref[...] * u)
    yg_ref[...] = _gelu(y)


def _s5_sample(p, h0re, h0im, bdb, cre, cim, d, pw):
    t = p.shape[0]
    db = t // DEC_SEQ
    st = SSM_LG_STATES
    lgs = SSM_LANE_GROUPS
    return pl.pallas_call(
        functools.partial(_s5_sample_body, db=db),
        grid=(lgs,),
        in_specs=[
            pl.BlockSpec((t, LANE), lambda g: (0, COL_CU + g)),
            pl.BlockSpec((db, st), lambda g: (0, g)),
            pl.BlockSpec((db, st), lambda g: (0, g)),
            pl.BlockSpec((1, LANE, 2 * st), lambda g: (g, 0, 0)),
            pl.BlockSpec((1, st, LANE), lambda g: (g, 0, 0)),
            pl.BlockSpec((1, st, LANE), lambda g: (g, 0, 0)),
            pl.BlockSpec((1, LANE), lambda g: (0, g)),
            pl.BlockSpec((1, SSM_R, 2 * st), lambda g: (g, 0, 0)),
        ],
        out_specs=[
            pl.BlockSpec((t, LANE), lambda g: (0, g)),
            pl.BlockSpec((db, st), lambda g: (0, g)),
            pl.BlockSpec((db, st), lambda g: (0, g)),
        ],
        out_shape=[jax.ShapeDtypeStruct((t, C_WIDTH), F32),
                   jax.ShapeDtypeStruct((db, lgs * st), F32), jax.ShapeDtypeStruct((db, lgs * st), F32)],
        scratch_shapes=[pltpu.VMEM((t, 2 * st), F32)],
        compiler_params=_params(("arbitrary",), 32),
        name="s5_sample",
    )(p, h0re, h0im, bdb, cre, cim, d, pw)


def _mix_out_body(a_ref, b_ref, y_ref, x_ref, wg_ref, ga_ref, gb_ref, gc_ref, wo_ref, o_ref):
    glu = jnp.dot(y_ref[...].astype(BF16), wg_ref[...], preferred_element_type=F32)
    c = glu[:, :C_WIDTH] * jax.nn.sigmoid(glu[:, C_WIDTH:])
    mixed = jnp.concatenate([
        (_rms(a_ref[...]) * ga_ref[...]).astype(BF16),
        (_rms(b_ref[...]) * gb_ref[...]).astype(BF16),
        (_rms(c) * gc_ref[...]).astype(BF16)], axis=1)
    o_ref[...] = x_ref[...] + jnp.dot(mixed, wo_ref[...], preferred_element_type=F32)


def _mix_out(a, b, y, x, w_glu, g_a, g_b, g_c, w_out, layer, tm=512):
    t = x.shape[0]
    row = lambda w: pl.BlockSpec((tm, w), lambda i: (i, 0))
    whole = lambda arr: pl.BlockSpec(arr.shape, lambda i: (0, 0))
    of_layer = lambda arr: pl.BlockSpec((None,) + arr.shape[1:], lambda i: (layer, 0, 0))
    return pl.pallas_call(
        _mix_out_body,
        grid=(t // tm,),
        in_specs=[row(A_WIDTH), row(B_WIDTH), row(C_WIDTH), row(D_MODEL),
                  of_layer(w_glu), whole(g_a), whole(g_b), whole(g_c), of_layer(w_out)],
        out_specs=row(D_MODEL),
        out_shape=jax.ShapeDtypeStruct((t, D_MODEL), F32),
        compiler_params=_params(("arbitrary",), 56),
        name="mix_out",
    )(a, b, y, x, w_glu, g_a, g_b, g_c, w_out)


FFN_HALO = 16


def _conv_gate(ug, uv, cwg, cwv, cbg, cbv, taps):
    def conv(u, cw, cb):
        return ((cb + cw[0:1, :] * taps[0](u)) + cw[1:2, :] * taps[1](u)) + cw[2:3, :] * taps[2](u)
    gate = conv(ug, cwg, cbg)
    return (gate * jax.nn.sigmoid(gate)) * conv(uv, cwv, cbv)


def _finish(o_ref, gf_ref, is_last_tile, final_norm):
    if final_norm:
        @pl.when(is_last_tile)
        def _():
            o_ref[...] = _rms(o_ref[...]) * gf_ref[...]


def _ffn_prompt_body(x_ref, halo_ref, g_ref, wg_ref, wv_ref, cwg_ref, cwv_ref, cbg_ref, cbv_ref, wd_ref, gf_ref,
                     o_ref, tail_ref, h_ref, ug_ref, uv_ref, *, tm, final_norm):
    i, f = pl.program_id(0), pl.program_id(1)
    hl = FFN_HALO

    @pl.when(f == 0)
    def _():
        h_ref[hl:, :] = (_rms(x_ref[...]) * g_ref[...]).astype(BF16)
        prev = _rms(halo_ref[...]) * g_ref[...]
        h_ref[0:hl, :] = jnp.where(i > 0, prev, 0.0).astype(BF16)
        o_ref[...] = x_ref[...]

    h = h_ref[...]
    ug_ref[...] = jnp.dot(h, wg_ref[...], preferred_element_type=F32)
    uv_ref[...] = jnp.dot(h, wv_ref[...], preferred_element_type=F32)
    taps = tuple((lambda u_ref, off=off: u_ref[pl.ds(hl - 2 + off, tm), :]) for off in range(3))
    act = _conv_gate(ug_ref, uv_ref, cwg_ref[...], cwv_ref[...], cbg_ref[...], cbv_ref[...], taps)
    o_ref[...] += jnp.dot(act.astype(BF16), wd_ref[...], preferred_element_type=F32)
    tail_ref[0] = ug_ref[pl.ds(hl + tm - 8, 8), :]
    tail_ref[1] = uv_ref[pl.ds(hl + tm - 8, 8), :]
    _finish(o_ref, gf_ref, f == pl.num_programs(1) - 1, final_norm)


def _ffn_prompt(x, g, w_up, conv_w, conv_b, w_down, g_final, layer, final_norm, tm=512, tf=512):
    t = x.shape[0]
    nf = D_FF // tf
    hl = FFN_HALO
    cb = conv_b.reshape(conv_b.shape[0], 1, 2 * D_FF)
    return pl.pallas_call(
        functools.partial(_ffn_prompt_body, tm=tm, final_norm=final_norm),
        grid=(t // tm, nf),
        in_specs=[
            pl.BlockSpec((tm, D_MODEL), lambda i, f: (i, 0)),
            pl.BlockSpec((hl, D_MODEL), lambda i, f: (jnp.maximum(i * (tm // hl) - 1, 0), 0)),
            pl.BlockSpec((1, D_MODEL), lambda i, f: (0, 0)),
            pl.BlockSpec((None, D_MODEL, tf), lambda i, f: (layer, 0, f)),
            pl.BlockSpec((None, D_MODEL, tf), lambda i, f: (layer, 0, nf + f)),
            pl.BlockSpec((None, 3, tf), lambda i, f: (layer, 0, f)),
            pl.BlockSpec((None, 3, tf), lambda i, f: (layer, 0, nf + f)),
            pl.BlockSpec((None, 1, tf), lambda i, f: (layer, 0, f)),
            pl.BlockSpec((None, 1, tf), lambda i, f: (layer, 0, nf + f)),
            pl.BlockSpec((None, tf, D_MODEL), lambda i, f: (layer, f, 0)),
            pl.BlockSpec((1, D_MODEL), lambda i, f: (0, 0)),
        ],
        out_specs=[
            pl.BlockSpec((tm, D_MODEL), lambda i, f: (i, 0), pipeline_mode=pl.Buffered(1)),
            pl.BlockSpec((None, 2, 8, tf), lambda i, f: (i, 0, 0, f)),
        ],
        out_shape=[jax.ShapeDtypeStruct((t, D_MODEL), F32), jax.ShapeDtypeStruct((t // tm, 2, 8, D_FF), F32)],
        scratch_shapes=[pltpu.VMEM((hl + tm, D_MODEL), BF16)] + [pltpu.VMEM((hl + tm, tf), F32)] * 2,
        compiler_params=_params(("arbitrary", "arbitrary"), 60),
        name="ffn_prompt",
    )(x, x, g, w_up, w_up, conv_w, conv_w, cb, cb, w_down, g_final)


def _ffn_sample_body(x_ref, g_ref, sg_ref, sv_ref, wg_ref, wv_ref, cwg_ref, cwv_ref, cbg_ref, cbv_ref, wd_ref, gf_ref,
                     o_ref, tg_ref, tv_ref, h_ref, *, db, final_norm):
    f = pl.program_id(0)
    t = DEC_SEQ * db

    @pl.when(f == 0)
    def _():
        h_ref[...] = (_rms(x_ref[...]) * g_ref[...]).astype(BF16)
        o_ref[...] = x_ref[...]

    h = h_ref[...]
    ug = jnp.dot(h, wg_ref[...], preferred_element_type=F32)
    uv = jnp.dot(h, wv_ref[...], preferred_element_type=F32)
    pg = jnp.concatenate([sg_ref[:, 0, :], sg_ref[:, 1, :], ug], axis=0)
    pv = jnp.concatenate([sv_ref[:, 0, :], sv_ref[:, 1, :], uv], axis=0)
    taps = (lambda u: u[0:t], lambda u: u[db:db + t], lambda u: u[2 * db:2 * db + t])
    act = _conv_gate(pg, pv, cwg_ref[...], cwv_ref[...], cbg_ref[...], cbv_ref[...], taps)
    o_ref[...] += jnp.dot(act.astype(BF16), wd_ref[...], preferred_element_type=F32)

    for j in range(2):
        rows = slice((DEC_SEQ - 2 + j) * db, (DEC_SEQ - 1 + j) * db)
        tg_ref[:, j, :] = ug[rows]
        tv_ref[:, j, :] = uv[rows]
    _finish(o_ref, gf_ref, f == pl.num_programs(0) - 1, final_norm)


def _ffn_sample(x, g, state, w_up, conv_w, conv_b, w_down, g_final, layer, final_norm, tf=512):
    t = x.shape[0]
    db = t // DEC_SEQ
    nf = D_FF // tf
    cb = conv_b.reshape(conv_b.shape[0], 1, 2 * D_FF)
    return pl.pallas_call(
        functools.partial(_ffn_sample_body, db=db, final_norm=final_norm),
        grid=(nf,),
        in_specs=[
            pl.BlockSpec((t, D_MODEL), lambda f: (0, 0)),
            pl.BlockSpec((1, D_MODEL), lambda f: (0, 0)),
            pl.BlockSpec((None, db, 2, tf), lambda f: (layer, 0, 0, f)),
            pl.BlockSpec((None, db, 2, tf), lambda f: (layer, 0, 0, nf + f)),
            pl.BlockSpec((None, D_MODEL, tf), lambda f: (layer, 0, f)),
            pl.BlockSpec((None, D_MODEL, tf), lambda f: (layer, 0, nf + f)),
            pl.BlockSpec((None, 3, tf), lambda f: (layer, 0, f)),
            pl.BlockSpec((None, 3, tf), lambda f: (layer, 0, nf + f)),
            pl.BlockSpec((None, 1, tf), lambda f: (layer, 0, f)),
            pl.BlockSpec((None, 1, tf), lambda f: (layer, 0, nf + f)),
            pl.BlockSpec((None, tf, D_MODEL), lambda f: (layer, f, 0)),
            pl.BlockSpec((1, D_MODEL), lambda f: (0, 0)),
        ],
        out_specs=[
            pl.BlockSpec((t, D_MODEL), lambda f: (0, 0)),
            pl.BlockSpec((db, 2, tf), lambda f: (0, 0, f)),
            pl.BlockSpec((db, 2, tf), lambda f: (0, 0, f)),
        ],
        out_shape=[jax.ShapeDtypeStruct((t, D_MODEL), F32)] + [jax.ShapeDtypeStruct((db, 2, D_FF), F32)] * 2,
        scratch_shapes=[pltpu.VMEM((t, D_MODEL), BF16)],
        compiler_params=_params(("arbitrary",), 48),
        name="ffn_sample",
    )(x, g, state, state, w_up, w_up, conv_w, conv_w, cb, cb, w_down, g_final)


def kernel(x_prompt, x_sample, cache_win_k, cache_win_v, state_ssm_re, state_ssm_im, state_ffn_conv,
           g_mix, w_in, g_out_a, g_out_b, g_out_c, gmlp_gv, gmlp_ws, gmlp_bs, ssm_a_re, ssm_a_im,
           ssm_log_dt, ssm_b_re, ssm_b_im, ssm_c_re, ssm_c_im, ssm_d, ssm_w_glu, w_out, g_ffn, w_up,
           conv_w, conv_b, w_down, g_final):
    depth = w_in.shape[0]
    bsz, seq, _ = x_prompt.shape
    db, ds, _ = x_sample.shape
    assert bsz == 1 and ds == DEC_SEQ and seq % (SSM_R * SSM_NC) == 0 and seq >= A_WINDOW_MAX
    n_states = C_GROUPS * SSM_P
    row = lambda v: v.reshape(1, -1)

    g_fin = row(g_final)
    w_in_b, w_glu_b, w_out_b = w_in.astype(BF16), ssm_w_glu.astype(BF16), w_out.astype(BF16)
    w_up_b, w_down_b = w_up.astype(BF16), w_down.astype(BF16)
    xp = x_prompt.reshape(seq, D_MODEL)
    xs = jnp.transpose(x_sample, (1, 0, 2)).reshape(ds * db, D_MODEL)
    outs = {k: [] for k in ("kp", "vp", "ks", "vs", "gv", "rp", "ip", "rs", "is", "cp", "cs")}

    for l in range(depth):
        pw, bdb = _ssm_prep(ssm_a_re[l], ssm_a_im[l], ssm_log_dt[l], ssm_b_re[l], ssm_b_im[l])
        c_re, c_im = _ssm_out_matrix(ssm_c_re[l]), _ssm_out_matrix(ssm_c_im[l])
        d_row = row(ssm_d[l])
        gains = (row(g_out_a[l]), row(g_out_b[l]), row(g_out_c[l]))
        last = l == depth - 1

        p = _proj(xp, row(g_mix[l]), w_in_b, l, tm=min(1024, seq))
        a = _attn_prompt(p)
        b = _gmlp_prompt(p, row(gmlp_gv[l]), gmlp_ws[l], gmlp_bs[l])
        yg, h_last = _s5_prompt(p, bdb, c_re, c_im, d_row, pw)
        xm = _mix_out(a, b, yg, xp, w_glu_b, *gains, w_out_b, l)
        xp, tail = _ffn_prompt(xm, row(g_ffn[l]), w_up_b, conv_w, conv_b, w_down_b, g_fin, l,
                               final_norm=last, tm=min(1024, seq))
        keep = min(A_WINDOW_MAX, seq)
        outs["kp"].append(p[seq - keep:, A_WIDTH:2 * A_WIDTH].reshape(1, keep, A_HEADS, HEAD_DIM))
        outs["vp"].append(p[seq - keep:, 2 * A_WIDTH:3 * A_WIDTH].reshape(1, keep, A_HEADS, HEAD_DIM))
        st = SSM_LG_STATES
        outs["rp"].append(h_last[:, 0, :st].reshape(1, C_GROUPS, SSM_P))
        outs["ip"].append(h_last[:, 0, st:].reshape(1, C_GROUPS, SSM_P))
        outs["cp"].append(jnp.transpose(tail[-1, :, 6:8, :], (1, 0, 2)).reshape(1, 2, 2 * D_FF))

        ps = _proj(xs, row(g_mix[l]), w_in_b, l, tm=ds * db)

        def heads(cols):
            return jnp.transpose(cols.reshape(ds, db, A_HEADS, HEAD_DIM), (1, 0, 2, 3)).reshape(
                db, ds * A_HEADS, HEAD_DIM)

        q3, k3, v3 = (heads(ps[:, i * A_WIDTH:(i + 1) * A_WIDTH]) for i in range(3))
        a3 = _attn_sample(q3, k3, v3, cache_win_k, cache_win_v, l)
        a_s = jnp.transpose(a3.reshape(db, ds, A_WIDTH), (1, 0, 2)).reshape(ds * db, A_WIDTH)
        b_s, gv_s = _gmlp_sample(ps, row(gmlp_gv[l]), gmlp_ws[l], gmlp_bs[l])
        yg_s, hre, him = _s5_sample(ps, state_ssm_re[l].reshape(db, n_states), state_ssm_im[l].reshape(db, n_states),
                                    bdb, c_re, c_im, d_row, pw)
        xm_s = _mix_out(a_s, b_s, yg_s, xs, w_glu_b, *gains, w_out_b, l, tm=ds * db)
        xs, tail_g, tail_v = _ffn_sample(xm_s, row(g_ffn[l]), state_ffn_conv, w_up_b, conv_w, conv_b, w_down_b,
                                         g_fin, l, final_norm=last)
        outs["ks"].append(k3.reshape(db, ds, A_HEADS, HEAD_DIM))
        outs["vs"].append(v3.reshape(db, ds, A_HEADS, HEAD_DIM))
        outs["gv"].append(jnp.transpose(gv_s.reshape(ds, db, B_WIDTH), (1, 0, 2)))
        outs["rs"].append(hre.reshape(db, C_GROUPS, SSM_P))
        outs["is"].append(him.reshape(db, C_GROUPS, SSM_P))
        outs["cs"].append(jnp.concatenate([tail_g, tail_v], axis=-1))

    y_p = xp.reshape(1, seq, D_MODEL)
    y_s = jnp.transpose(xs.reshape(ds, db, D_MODEL), (1, 0, 2))
    st = lambda k: jnp.stack(outs[k])
    return (y_p, y_s, st("kp"), st("vp"), st("ks"), st("vs"), st("gv"),
            st("rp"), st("ip"), st("rs"), st("is"), st("cp"), st("cs"))
```

```python
import functools
import math

import numpy as np
import jax
import jax.numpy as jnp
from jax import lax
from jax.experimental import pallas as pl
from jax.experimental.pallas import tpu as pltpu

F32 = jnp.float32
BF16 = jnp.bfloat16

D_MODEL = 2048
A_WIDTH = 1024
A_HEADS = 8
HEAD_DIM = 128
A_DILATIONS = (1, 4, 16)
A_SPAN = 128
A_WINDOW_MAX = 2048
SAMPLE_TAIL_ROWS = 512
SAMPLE_FAR_QUARTERS = A_WINDOW_MAX // SAMPLE_TAIL_ROWS - 1
SAMPLE_STRIDE = max(A_DILATIONS)
SAMPLE_GROUPS = SAMPLE_TAIL_ROWS // SAMPLE_STRIDE
ATTN_UNROLL = {1: 32, 4: 16, 16: 16}
B_WIDTH = 512
B_GROUPS = 4
CHUNK = 128
C_WIDTH = 512
C_GROUPS = 32
C_GROUP_W = 16
SSM_P = 64
N_IN = 3 * A_WIDTH + 2 * B_WIDTH + C_WIDTH
D_FF = 5632
DEC_SEQ = 4
EPS = 1e-6
NEG = -1e30

LANE = 128
COL_Q, COL_K, COL_V = 0, A_WIDTH // LANE, 2 * A_WIDTH // LANE
COL_BU = 3 * A_WIDTH // LANE
COL_BV = COL_BU + B_WIDTH // LANE
COL_CU = COL_BV + B_WIDTH // LANE

SSM_LANE_GROUPS = 4
SSM_LG_STATES = C_GROUPS * SSM_P // SSM_LANE_GROUPS
SSM_R = 64
SSM_NC = 32

MIB = 1024 * 1024
SUBLANE = 8

PROJ_TM, PROJ_TN = 1024, 1536
MIX_TM = 512
FFN_TM, FFN_TF = 1024, 512
GMLP_CHUNKS = 8
VMEM_MIB = dict(proj_in=56, attn_prompt=56, attn_sample=48, gmlp=32, s5_prompt=40, s5_sample=32,
                mix_out=56, ffn_prompt=60, ffn_sample=48)


def _params(sem, vmem_mib, flags=None):
    return pltpu.CompilerParams(dimension_semantics=sem, vmem_limit_bytes=vmem_mib * MIB, flags=flags)


def _gelu(x):
    c = math.sqrt(2.0 / math.pi)
    return x * (0.5 * (1.0 + jnp.tanh(c * (x + 0.044715 * (x * x * x)))))


def _rms(x):
    return x * lax.rsqrt(jnp.mean(x * x, axis=-1, keepdims=True) + EPS)


_NT = (((1,), (1,)), ((), ()))


def _proj_body(x_ref, g_ref, w_ref, o_ref, h_ref):
    @pl.when(pl.program_id(1) == 0)
    def _():
        h_ref[...] = (_rms(x_ref[...]) * g_ref[...]).astype(BF16)

    o_ref[...] = jnp.dot(h_ref[...], w_ref[...], preferred_element_type=F32)


def _proj(x, g, w, layer):
    t = x.shape[0]
    tm, tn = min(PROJ_TM, t), PROJ_TN
    return pl.pallas_call(
        _proj_body,
        grid=(t // tm, N_IN // tn),
        in_specs=[
            pl.BlockSpec((tm, D_MODEL), lambda i, j: (i, 0)),
            pl.BlockSpec((1, D_MODEL), lambda i, j: (0, 0)),
            pl.BlockSpec((None, D_MODEL, tn), lambda i, j: (layer, 0, j)),
        ],
        out_specs=pl.BlockSpec((tm, tn), lambda i, j: (i, j)),
        out_shape=jax.ShapeDtypeStruct((t, N_IN), F32),
        scratch_shapes=[pltpu.VMEM((tm, D_MODEL), BF16)],
        compiler_params=_params(("arbitrary", "arbitrary"), VMEM_MIB["proj_in"]),
        name="proj_in",
    )(x, g, w)


def _attn_prompt_body(q_ref, k_ref, v_ref, o_ref, acc_ref, m_ref, l_ref, *, seq):
    scale = math.log2(math.e) / math.sqrt(HEAD_DIM)
    ii = lax.broadcasted_iota(jnp.int32, (A_SPAN, 2 * A_SPAN), 0)
    jj = lax.broadcasted_iota(jnp.int32, (A_SPAN, 2 * A_SPAN), 1)
    band = (jj >= ii) & (jj <= ii + A_SPAN)
    band_cur = band & (jj >= A_SPAN)
    blk = (A_SPAN, A_SPAN)

    for order, dil in enumerate(sorted(A_DILATIONS, reverse=True)):
        nb = seq // (A_SPAN * dil)
        step = A_SPAN * dil

        def rows(start, dil=dil):
            if dil == 1:
                return pl.ds(pl.multiple_of(start, A_SPAN), A_SPAN)
            return pl.ds(start, A_SPAN, stride=dil)

        def runs(run_list, rows=rows, step=step, first=order == 0):
            work = []
            for r, n0, count in run_list:
                static_start = isinstance(n0, int)
                if not (static_start and n0 == 0):
                    prev = r + jnp.maximum(n0 - 1, 0) * step
                    kp, vp = k_ref[rows(prev), :].astype(BF16), v_ref[rows(prev), :].astype(BF16)
                for u in range(count):
                    base = r + (n0 + u) * step
                    q = (q_ref[rows(base), :] * scale).astype(BF16)
                    kc, vc = k_ref[rows(base), :].astype(BF16), v_ref[rows(base), :].astype(BF16)
                    if u == 0 and static_start and n0 == 0:
                        kp, vp, mask = kc, vc, band_cur
                    elif u == 0 and not static_start:
                        mask = band & ((jj >= A_SPAN) | (n0 > 0))
                    else:
                        mask = band
                    s = lax.dot_general(q, jnp.concatenate([kp, kc], axis=0), _NT,
                                        preferred_element_type=F32)
                    s = jnp.where(mask, s, NEG)
                    v2 = jnp.concatenate([vp, vc], axis=0)
                    m_blk = jnp.max(s, axis=1, keepdims=True)
                    if first:
                        m_new = jnp.broadcast_to(m_blk, blk)
                        p = jnp.exp2(s - m_blk)
                        l_new = jnp.broadcast_to(jnp.sum(p, axis=1, keepdims=True), blk)
                        acc_new = jnp.dot(p.astype(BF16), v2, preferred_element_type=F32)
                    else:
                        m_old = m_ref[rows(base), :]
                        m_new = jnp.maximum(m_old, m_blk)
                        alpha = jnp.exp2(m_old - m_new)
                        p = jnp.exp2(s - jnp.concatenate([m_new, m_new], axis=1))
                        l_new = alpha * l_ref[rows(base), :] + jnp.sum(p, axis=1, keepdims=True)
                        acc_new = alpha * acc_ref[rows(base), :] + jnp.dot(p.astype(BF16), v2,
                                                                           preferred_element_type=F32)
                    work.append((base, m_new, l_new, acc_new))
                    kp, vp = kc, vc
            for base, m_new, l_new, acc_new in work:
                m_ref[rows(base), :] = m_new
                l_ref[rows(base), :] = l_new
                acc_ref[rows(base), :] = acc_new

        unroll = min(ATTN_UNROLL[dil], nb * dil)
        if unroll <= nb:
            per_res = nb // unroll

            def trip(i, c, runs=runs, unroll=unroll, per_res=per_res):
                runs([(i // per_res, (i % per_res) * unroll, unroll)])
                return c
        else:
            res_per_trip = unroll // nb

            def trip(i, c, runs=runs, nb=nb, res_per_trip=res_per_trip):
                runs([(i * res_per_trip + j, 0, nb) for j in range(res_per_trip)])
                return c

        lax.fori_loop(0, nb * dil // unroll, trip, 0)

    o_ref[...] = acc_ref[...] / l_ref[...]


def _attn_prompt(p):
    seq = p.shape[0]
    blk = (seq, HEAD_DIM)
    return pl.pallas_call(
        functools.partial(_attn_prompt_body, seq=seq),
        grid=(A_HEADS,),
        in_specs=[
            pl.BlockSpec(blk, lambda h: (0, COL_Q + h)),
            pl.BlockSpec(blk, lambda h: (0, COL_K + h)),
            pl.BlockSpec(blk, lambda h: (0, COL_V + h)),
        ],
        out_specs=pl.BlockSpec(blk, lambda h: (0, h)),
        out_shape=jax.ShapeDtypeStruct((seq, A_WIDTH), F32),
        scratch_shapes=[pltpu.VMEM(blk, F32)] * 3,
        compiler_params=_params(("arbitrary",), VMEM_MIB["attn_prompt"]),
        name="attn_prompt",
    )(p, p, p)


def _sample_attn_masks():
    j = np.arange(DEC_SEQ)[:, None, None, None]
    h = np.arange(A_HEADS)[None, :, None, None]
    hk = np.arange(A_HEADS)[None, None, None, :]
    same = (h == hk)
    i = np.arange(512)[None, None, :, None]
    cnt_l = same * ((i >= 384 + j).astype(np.int32) + ((i - j) % 4 == 0).astype(np.int32))
    jn = np.arange(DEC_SEQ)[None, None, :, None]
    cnt_n = same * ((jn <= j).astype(np.int32) + 2 * (jn == j).astype(np.int32))
    g = np.arange(128)[None, None, :, None]
    head_match = np.broadcast_to(same[0:1], (1, A_HEADS, 128, A_HEADS))
    return (cnt_l.reshape(DEC_SEQ * A_HEADS, 512 * A_HEADS).astype(np.float32),
            cnt_n.reshape(DEC_SEQ * A_HEADS, DEC_SEQ * A_HEADS).astype(np.float32),
            (head_match + 0 * g).reshape(A_HEADS, 128 * A_HEADS).astype(np.float32))


def _attn_sample_body(q_ref, k_ref, v_ref, *refs):
    nh = A_HEADS
    nfar = SAMPLE_FAR_QUARTERS * DEC_SEQ
    k_far, kl_ref = refs[:nfar], refs[nfar]
    v_far, vl_ref = refs[nfar + 1:2 * nfar + 1], refs[2 * nfar + 1]
    cl_ref, cn_ref, hm_ref, o_ref = refs[2 * nfar + 2:]
    q = q_ref[...] * (1.0 / math.sqrt(HEAD_DIM))
    qs = q.astype(BF16)
    cl, cn, hm = cl_ref[...], cn_ref[...], hm_ref[...]

    def stride16(far, last, res):
        grp, stride = SAMPLE_GROUPS, SAMPLE_STRIDE
        near = last.reshape(grp, stride, nh, HEAD_DIM)[:, res].reshape(grp * nh, HEAD_DIM)
        parts = [far[qt * DEC_SEQ + res][...].reshape(grp * nh, HEAD_DIM).astype(BF16)
                 for qt in range(SAMPLE_FAR_QUARTERS)]
        return jnp.concatenate(parts + [near.astype(BF16)], axis=0)

    k_last = kl_ref[...]
    kl = k_last.reshape(SAMPLE_TAIL_ROWS * nh, HEAD_DIM).astype(BF16)
    s_l = lax.dot_general(qs, kl, _NT, preferred_element_type=F32)
    s_n = lax.dot_general(qs, k_ref[...].astype(BF16), _NT, preferred_element_type=F32)
    s_a = []
    for res in range(DEC_SEQ):
        kk = stride16(k_far, k_last, res)
        s_a.append(lax.dot_general(q[res * nh:(res + 1) * nh].astype(BF16), kk, _NT,
                                   preferred_element_type=F32))
    m = jnp.max(jnp.where(cl > 0, s_l, NEG), axis=1, keepdims=True)
    m = jnp.maximum(m, jnp.max(jnp.where(cn > 0, s_n, NEG), axis=1, keepdims=True))
    m_a = jnp.concatenate([jnp.max(jnp.where(hm > 0, s, NEG), axis=1, keepdims=True) for s in s_a], axis=0)
    m = jnp.maximum(m, m_a)

    p_l = cl * jnp.exp(jnp.minimum(s_l - m, 0.0))
    p_n = cn * jnp.exp(jnp.minimum(s_n - m, 0.0))
    den = jnp.sum(p_l, axis=1, keepdims=True) + jnp.sum(p_n, axis=1, keepdims=True)
    v_last = vl_ref[...]
    vl = v_last.reshape(SAMPLE_TAIL_ROWS * nh, HEAD_DIM).astype(BF16)
    out = jnp.dot(p_l.astype(BF16), vl, preferred_element_type=F32)
    out = out + jnp.dot(p_n.astype(BF16), v_ref[...].astype(BF16), preferred_element_type=F32)
    o_a, d_a = [], []
    for res in range(DEC_SEQ):
        p_a = hm * jnp.exp(jnp.minimum(s_a[res] - m[res * nh:(res + 1) * nh], 0.0))
        d_a.append(jnp.sum(p_a, axis=1, keepdims=True))
        vv = stride16(v_far, v_last, res)
        o_a.append(jnp.dot(p_a.astype(BF16), vv, preferred_element_type=F32))
    out = out + jnp.concatenate(o_a, axis=0)
    den = den + jnp.concatenate(d_a, axis=0)
    o_ref[...] = out / den


def _attn_sample(q3, k3, v3, cache_k, cache_v, layer):
    db = q3.shape[0]
    depth, _, w_buf, nh, hd = cache_k.shape
    assert (w_buf, nh, hd) == (A_WINDOW_MAX, A_HEADS, HEAD_DIM) and q3.shape[1] == DEC_SEQ * A_HEADS
    cnt_l, cnt_n, head_match = _sample_attn_masks()
    rows = DEC_SEQ * A_HEADS

    nq = SAMPLE_FAR_QUARTERS

    def strided_specs():
        return [pl.BlockSpec((None, None, None, SAMPLE_GROUPS, None, nh, hd), functools.partial(
            lambda b, qt, res: (layer, b, qt, 0, res, 0, 0), qt=qt, res=res))
            for qt in range(nq) for res in range(DEC_SEQ)]

    last_spec = pl.BlockSpec((None, None, None, SAMPLE_TAIL_ROWS, nh, hd), lambda b: (layer, b, nq, 0, 0, 0))
    tok_spec = pl.BlockSpec((None, rows, hd), lambda b: (b, 0, 0))

    def whole(a):
        return pl.BlockSpec(a.shape, lambda b: (0, 0))

    k16 = cache_k.reshape(depth, db, nq + 1, SAMPLE_GROUPS, SAMPLE_STRIDE, nh, hd)
    v16 = cache_v.reshape(depth, db, nq + 1, SAMPLE_GROUPS, SAMPLE_STRIDE, nh, hd)
    k512 = cache_k.reshape(depth, db, nq + 1, SAMPLE_TAIL_ROWS, nh, hd)
    v512 = cache_v.reshape(depth, db, nq + 1, SAMPLE_TAIL_ROWS, nh, hd)
    nfar = nq * DEC_SEQ
    return pl.pallas_call(
        _attn_sample_body,
        grid=(db,),
        in_specs=[tok_spec, tok_spec, tok_spec] + strided_specs() + [last_spec] + strided_specs() + [last_spec]
        + [whole(cnt_l), whole(cnt_n), whole(head_match)],
        out_specs=tok_spec,
        out_shape=jax.ShapeDtypeStruct((db, rows, hd), F32),
        compiler_params=_params(("arbitrary",), VMEM_MIB["attn_sample"]),
        name="attn_sample",
    )(q3, k3, v3, *([k16] * nfar), k512, *([v16] * nfar), v512,
      jnp.asarray(cnt_l), jnp.asarray(cnt_n), jnp.asarray(head_match))


def _gmlp_prompt_body(bu_ref, bv_ref, gg_ref, ws_ref, bs_ref, o_ref, *, chunks):
    ri = lax.broadcasted_iota(jnp.int32, (CHUNK, CHUNK), 0)
    ci = lax.broadcasted_iota(jnp.int32, (CHUNK, CHUNK), 1)
    wm = jnp.where(ci <= ri, ws_ref[0], 0.0).astype(BF16)
    bias = bs_ref[0]
    gain = gg_ref[...]
    for c in range(chunks):
        sl = slice(c * CHUNK, (c + 1) * CHUNK)
        gv = _rms(_gelu(bv_ref[sl, :])) * gain
        mix = jnp.dot(wm, gv.astype(BF16), preferred_element_type=F32) + bias
        o_ref[sl, :] = _gelu(bu_ref[sl, :]) * mix


def _gmlp_prompt(p, gain, ws, bs):
    seq = p.shape[0]
    chunks = GMLP_CHUNKS
    tm = chunks * CHUNK
    return pl.pallas_call(
        functools.partial(_gmlp_prompt_body, chunks=chunks),
        grid=(seq // tm, B_GROUPS),
        in_specs=[
            pl.BlockSpec((tm, LANE), lambda i, g: (i, COL_BU + g)),
            pl.BlockSpec((tm, LANE), lambda i, g: (i, COL_BV + g)),
            pl.BlockSpec((1, LANE), lambda i, g: (0, g)),
            pl.BlockSpec((1, CHUNK, CHUNK), lambda i, g: (g, 0, 0)),
            pl.BlockSpec((1, CHUNK, 1), lambda i, g: (g, 0, 0)),
        ],
        out_specs=pl.BlockSpec((tm, LANE), lambda i, g: (i, g)),
        out_shape=jax.ShapeDtypeStruct((seq, B_WIDTH), F32),
        compiler_params=_params(("arbitrary", "arbitrary"), VMEM_MIB["gmlp"]),
        name="gmlp_prompt",
    )(p, p, gain, ws, bs.reshape(B_GROUPS, CHUNK, 1))


def _gmlp_sample_body(ws_ref, bs_ref, bu_ref, bv_ref, gg_ref, o_ref, gv_ref, *, db):
    g = pl.program_id(0)
    gv = _rms(_gelu(bv_ref[...])) * gg_ref[...]
    gv_ref[...] = gv
    gu = _gelu(bu_ref[...])
    for i in range(DEC_SEQ):
        mix = jnp.full((db, LANE), bs_ref[g, i], F32)
        for j in range(i + 1):
            mix = mix + ws_ref[g, i * DEC_SEQ + j] * gv[j * db:(j + 1) * db]
        o_ref[i * db:(i + 1) * db, :] = gu[i * db:(i + 1) * db] * mix


def _gmlp_sample(p, gain, ws, bs):
    t = p.shape[0]
    db = t // DEC_SEQ
    ws4 = ws[:, :DEC_SEQ, :DEC_SEQ].reshape(B_GROUPS, DEC_SEQ * DEC_SEQ)
    bs4 = bs[:, :DEC_SEQ]
    smem = pl.BlockSpec(memory_space=pltpu.SMEM)
    return pl.pallas_call(
        functools.partial(_gmlp_sample_body, db=db),
        grid=(B_GROUPS,),
        in_specs=[
            smem, smem,
            pl.BlockSpec((t, LANE), lambda g: (0, COL_BU + g)),
            pl.BlockSpec((t, LANE), lambda g: (0, COL_BV + g)),
            pl.BlockSpec((1, LANE), lambda g: (0, g)),
        ],
        out_specs=[pl.BlockSpec((t, LANE), lambda g: (0, g))] * 2,
        out_shape=[jax.ShapeDtypeStruct((t, B_WIDTH), F32)] * 2,
        compiler_params=_params(("arbitrary",), VMEM_MIB["gmlp"]),
        name="gmlp_sample",
    )(ws4, bs4, p, p, gain)


def _ssm_prep_body(lre_ref, lim_ref, ldt_ref, bre_ref, bim_ref, pwre_ref, pwim_ref, bbre_ref, bbim_ref):
    lre, lim = lre_ref[...], lim_ref[...]
    dt = jnp.exp(ldt_ref[...])
    k = lax.broadcasted_iota(jnp.int32, pwre_ref.shape, 0).astype(F32) + 1.0
    mag = jnp.exp(k * (dt * lre))
    ang = k * (dt * lim)
    pwre = mag * jnp.cos(ang)
    pwim = mag * jnp.sin(ang)
    pwre_ref[...] = pwre
    pwim_ref[...] = pwim
    xr, xi = pwre[0:1] - 1.0, pwim[0:1]
    den = lre * lre + lim * lim
    cr = (xr * lre + xi * lim) / den
    ci = (xi * lre - xr * lim) / den
    bre, bim = bre_ref[...], bim_ref[...]
    bbre_ref[...] = cr * bre - ci * bim
    bbim_ref[...] = cr * bim + ci * bre


def _ssm_prep(a_re, a_im, log_dt, b_re, b_im):
    n = C_GROUPS * SSM_P
    row = lambda a: a.reshape(1, n)
    bt = lambda b: jnp.transpose(b, (2, 0, 1)).reshape(C_GROUP_W, n)
    shapes = [jax.ShapeDtypeStruct((SSM_R, n), F32)] * 2 + [jax.ShapeDtypeStruct((C_GROUP_W, n), F32)] * 2
    pwre, pwim, bbre, bbim = pl.pallas_call(
        _ssm_prep_body, out_shape=shapes, name="ssm_prep",
    )(row(a_re), row(a_im), row(jnp.repeat(log_dt, SSM_P)), bt(b_re), bt(b_im))
    lg, st = SSM_LANE_GROUPS, SSM_LG_STATES
    split = lambda t: jnp.transpose(t.reshape(t.shape[0], lg, st), (1, 0, 2))
    pw = jnp.concatenate([split(pwre), split(pwim)], axis=-1)
    eye = jnp.eye(C_GROUPS // lg, dtype=F32)

    def blockdiag(bb):
        r = bb.reshape(C_GROUP_W, lg, C_GROUPS // lg, SSM_P)
        return jnp.einsum('ab,mlbp->lambp', eye, r).reshape(lg, LANE, st)

    bdb = jnp.concatenate([blockdiag(bbre), blockdiag(bbim)], axis=-1).astype(BF16)
    return pw, bdb


def _ssm_out_matrix(c):
    lg = SSM_LANE_GROUPS
    eye = jnp.eye(C_GROUPS // lg, dtype=F32)
    r = c.reshape(lg, C_GROUPS // lg, C_GROUP_W, SSM_P)
    return jnp.einsum('ab,lanp->lapbn', eye, r).reshape(lg, SSM_LG_STATES, LANE).astype(BF16)


def _cmul_add(are, aim, hre, him, xre, xim):
    return are * hre - aim * him + xre, are * him + aim * hre + xim


def _s5_prompt_body(u_ref, bdb_ref, cre_ref, cim_ref, d_ref, pw_ref, yg_ref, hl_ref,
                    u3_ref, h_ref, car_ref, hc_ref):
    r_steps, nc, st = SSM_R, SSM_NC, SSM_LG_STATES
    re, im = slice(0, st), slice(st, 2 * st)

    @pl.when(pl.program_id(1) == 0)
    def _():
        hc_ref[...] = jnp.zeros(hc_ref.shape, F32)

    def rows(r):
        return pl.ds(pl.multiple_of(r * nc, nc), nc)

    def regroup(r, c):
        u3_ref[rows(r), :] = u_ref[pl.ds(r, nc, stride=r_steps), :]
        return c

    lax.fori_loop(0, r_steps, regroup, 0)
    h_ref[...] = jnp.dot(u3_ref[...].astype(BF16), bdb_ref[0], preferred_element_type=F32)

    half = st // 2
    for hf in range(2):
        cre_ = slice(hf * half, (hf + 1) * half)
        cim_ = slice(st + hf * half, st + (hf + 1) * half)
        are = jnp.broadcast_to(pw_ref[0, 0:1, cre_], (nc, half))
        aim = jnp.broadcast_to(pw_ref[0, 0:1, cim_], (nc, half))

        def step(r, carry, cre_=cre_, cim_=cim_, are=are, aim=aim):
            hre, him = _cmul_add(are, aim, carry[0], carry[1], h_ref[rows(r), cre_], h_ref[rows(r), cim_])
            h_ref[rows(r), cre_] = hre
            h_ref[rows(r), cim_] = him
            return hre, him

        lax.fori_loop(1, r_steps, step, (h_ref[0:nc, cre_], h_ref[0:nc, cim_]))

    ends = h_ref[(r_steps - 1) * nc:r_steps * nc, :]
    are, aim = pw_ref[0, r_steps - 1:r_steps, re], pw_ref[0, r_steps - 1:r_steps, im]
    cre, cim = hc_ref[0:1, re], hc_ref[0:1, im]
    for c in range(nc):
        car_ref[c:c + 1, re] = cre
        car_ref[c:c + 1, im] = cim
        cre, cim = _cmul_add(are, aim, cre, cim, ends[c:c + 1, re], ends[c:c + 1, im])
    hc_ref[:, re] = jnp.broadcast_to(cre, (SUBLANE, st))
    hc_ref[:, im] = jnp.broadcast_to(cim, (SUBLANE, st))
    hl_ref[0] = hc_ref[...]

    def fix(r, c):
        pre, pim = pw_ref[0, pl.ds(r, 1), re], pw_ref[0, pl.ds(r, 1), im]
        hre, him = _cmul_add(pre, pim, car_ref[:, re], car_ref[:, im], h_ref[rows(r), re], h_ref[rows(r), im])
        h_ref[rows(r), re] = hre
        h_ref[rows(r), im] = him
        return c

    lax.fori_loop(0, r_steps, fix, 0)

    y = (jnp.dot(h_ref[:, re].astype(BF16), cre_ref[0], preferred_element_type=F32)
         - jnp.dot(h_ref[:, im].astype(BF16), cim_ref[0], preferred_element_type=F32)
         + d_ref[...] * u3_ref[...])
    u3_ref[...] = _gelu(y)

    def ungroup(r, c):
        yg_ref[pl.ds(r, nc, stride=r_steps), :] = u3_ref[rows(r), :]
        return c

    lax.fori_loop(0, r_steps, ungroup, 0)


def _s5_prompt(p, bdb, cre, cim, d, pw):
    seq = p.shape[0]
    tseg = SSM_R * SSM_NC
    st2 = 2 * SSM_LG_STATES
    lgs = SSM_LANE_GROUPS
    return pl.pallas_call(
        _s5_prompt_body,
        grid=(lgs, seq // tseg),
        in_specs=[
            pl.BlockSpec((tseg, LANE), lambda g, t: (t, COL_CU + g)),
            pl.BlockSpec((1, LANE, st2), lambda g, t: (g, 0, 0)),
            pl.BlockSpec((1, SSM_LG_STATES, LANE), lambda g, t: (g, 0, 0)),
            pl.BlockSpec((1, SSM_LG_STATES, LANE), lambda g, t: (g, 0, 0)),
            pl.BlockSpec((1, LANE), lambda g, t: (0, g)),
            pl.BlockSpec((1, SSM_R, st2), lambda g, t: (g, 0, 0)),
        ],
        out_specs=[
            pl.BlockSpec((tseg, LANE), lambda g, t: (t, g)),
            pl.BlockSpec((1, SUBLANE, st2), lambda g, t: (g, 0, 0)),
        ],
        out_shape=[jax.ShapeDtypeStruct((seq, C_WIDTH), F32), jax.ShapeDtypeStruct((lgs, SUBLANE, st2), F32)],
        scratch_shapes=[
            pltpu.VMEM((tseg, LANE), F32),
            pltpu.VMEM((tseg, st2), F32),
            pltpu.VMEM((SSM_NC, st2), F32),
            pltpu.VMEM((SUBLANE, st2), F32),
        ],
        compiler_params=_params(("arbitrary", "arbitrary"), VMEM_MIB["s5_prompt"]),
        name="s5_prompt",
    )(p, bdb, cre, cim, d, pw)


def _s5_sample_body(u_ref, hre_ref, him_ref, bdb_ref, cre_ref, cim_ref, d_ref, pw_ref,
                    yg_ref, ore_ref, oim_ref, h_ref, *, db):
    st = SSM_LG_STATES
    re, im = slice(0, st), slice(st, 2 * st)
    u = u_ref[...]
    x = jnp.dot(u.astype(BF16), bdb_ref[0], preferred_element_type=F32)
    are, aim = pw_ref[0, 0:1, re], pw_ref[0, 0:1, im]
    hre, him = hre_ref[...], him_ref[...]
    for j in range(DEC_SEQ):
        rj = slice(j * db, (j + 1) * db)
        hre, him = _cmul_add(are, aim, hre, him, x[rj, re], x[rj, im])
        h_ref[rj, re] = hre
        h_ref[rj, im] = him
    ore_ref[...] = hre
    oim_ref[...] = him
    y = (jnp.dot(h_ref[:, re].astype(BF16), cre_ref[0], preferred_element_type=F32)
         - jnp.dot(h_ref[:, im].astype(BF16), cim_ref[0], preferred_element_type=F32)
         + d_ref[...] * u)
    yg_ref[...] = _gelu(y)


def _s5_sample(p, h0re, h0im, bdb, cre, cim, d, pw):
    t = p.shape[0]
    db = t // DEC_SEQ
    st = SSM_LG_STATES
    lgs = SSM_LANE_GROUPS
    return pl.pallas_call(
        functools.partial(_s5_sample_body, db=db),
        grid=(lgs,),
        in_specs=[
            pl.BlockSpec((t, LANE), lambda g: (0, COL_CU + g)),
            pl.BlockSpec((db, st), lambda g: (0, g)),
            pl.BlockSpec((db, st), lambda g: (0, g)),
            pl.BlockSpec((1, LANE, 2 * st), lambda g: (g, 0, 0)),
            pl.BlockSpec((1, st, LANE), lambda g: (g, 0, 0)),
            pl.BlockSpec((1, st, LANE), lambda g: (g, 0, 0)),
            pl.BlockSpec((1, LANE), lambda g: (0, g)),
            pl.BlockSpec((1, SSM_R, 2 * st), lambda g: (g, 0, 0)),
        ],
        out_specs=[
            pl.BlockSpec((t, LANE), lambda g: (0, g)),
            pl.BlockSpec((db, st), lambda g: (0, g)),
            pl.BlockSpec((db, st), lambda g: (0, g)),
        ],
        out_shape=[jax.ShapeDtypeStruct((t, C_WIDTH), F32),
                   jax.ShapeDtypeStruct((db, lgs * st), F32), jax.ShapeDtypeStruct((db, lgs * st), F32)],
        scratch_shapes=[pltpu.VMEM((t, 2 * st), F32)],
        compiler_params=_params(("arbitrary",), VMEM_MIB["s5_sample"]),
        name="s5_sample",
    )(p, h0re, h0im, bdb, cre, cim, d, pw)


def _mix_out_body(a_ref, b_ref, y_ref, x_ref, wg_ref, ga_ref, gb_ref, gc_ref, wo_ref, o_ref):
    glu = jnp.dot(y_ref[...].astype(BF16), wg_ref[...], preferred_element_type=F32)
    c = glu[:, :C_WIDTH] * jax.nn.sigmoid(glu[:, C_WIDTH:])
    mixed = jnp.concatenate([
        (_rms(a_ref[...]) * ga_ref[...]).astype(BF16),
        (_rms(b_ref[...]) * gb_ref[...]).astype(BF16),
        (_rms(c) * gc_ref[...]).astype(BF16)], axis=1)
    o_ref[...] = x_ref[...] + jnp.dot(mixed, wo_ref[...], preferred_element_type=F32)


def _mix_out(a, b, y, x, w_glu, g_a, g_b, g_c, w_out, layer):
    t = x.shape[0]
    tm = min(MIX_TM, t)
    row = lambda w: pl.BlockSpec((tm, w), lambda i: (i, 0))
    whole = lambda arr: pl.BlockSpec(arr.shape, lambda i: (0, 0))
    of_layer = lambda arr: pl.BlockSpec((None,) + arr.shape[1:], lambda i: (layer, 0, 0))
    return pl.pallas_call(
        _mix_out_body,
        grid=(t // tm,),
        in_specs=[row(A_WIDTH), row(B_WIDTH), row(C_WIDTH), row(D_MODEL),
                  of_layer(w_glu), whole(g_a), whole(g_b), whole(g_c), of_layer(w_out)],
        out_specs=row(D_MODEL),
        out_shape=jax.ShapeDtypeStruct((t, D_MODEL), F32),
        compiler_params=_params(("arbitrary",), VMEM_MIB["mix_out"]),
        name="mix_out",
    )(a, b, y, x, w_glu, g_a, g_b, g_c, w_out)


FFN_HALO = 16


def _conv_gate(ug, uv, cwg, cwv, cbg, cbv, taps):
    def conv(u, cw, cb):
        return ((cb + cw[0:1, :] * taps[0](u)) + cw[1:2, :] * taps[1](u)) + cw[2:3, :] * taps[2](u)
    gate = conv(ug, cwg, cbg)
    return (gate * jax.nn.sigmoid(gate)) * conv(uv, cwv, cbv)


def _finish(o_ref, gf_ref, is_last_tile, final_norm):
    if final_norm:
        @pl.when(is_last_tile)
        def _():
            o_ref[...] = _rms(o_ref[...]) * gf_ref[...]


def _ffn_prompt_body(x_ref, halo_ref, g_ref, wg_ref, wv_ref, cwg_ref, cwv_ref, cbg_ref, cbv_ref, wd_ref, gf_ref,
                     o_ref, tail_ref, h_ref, ug_ref, uv_ref, *, tm, final_norm):
    i, f = pl.program_id(0), pl.program_id(1)
    hl = FFN_HALO

    @pl.when(f == 0)
    def _():
        h_ref[hl:, :] = (_rms(x_ref[...]) * g_ref[...]).astype(BF16)
        prev = _rms(halo_ref[...]) * g_ref[...]
        h_ref[0:hl, :] = jnp.where(i > 0, prev, 0.0).astype(BF16)
        o_ref[...] = x_ref[...]

    h = h_ref[...]
    ug_ref[...] = jnp.dot(h, wg_ref[...], preferred_element_type=F32)
    uv_ref[...] = jnp.dot(h, wv_ref[...], preferred_element_type=F32)
    taps = tuple((lambda u_ref, off=off: u_ref[pl.ds(hl - 2 + off, tm), :]) for off in range(3))
    act = _conv_gate(ug_ref, uv_ref, cwg_ref[...], cwv_ref[...], cbg_ref[...], cbv_ref[...], taps)
    o_ref[...] += jnp.dot(act.astype(BF16), wd_ref[...], preferred_element_type=F32)
    tail_ref[0] = ug_ref[pl.ds(hl + tm - SUBLANE, SUBLANE), :]
    tail_ref[1] = uv_ref[pl.ds(hl + tm - SUBLANE, SUBLANE), :]
    _finish(o_ref, gf_ref, f == pl.num_programs(1) - 1, final_norm)


def _ffn_prompt(x, g, w_up, conv_w, conv_b, w_down, g_final, layer, final_norm):
    t = x.shape[0]
    tm, tf = min(FFN_TM, t), FFN_TF
    nf = D_FF // tf
    hl = FFN_HALO
    cb = conv_b.reshape(conv_b.shape[0], 1, 2 * D_FF)
    return pl.pallas_call(
        functools.partial(_ffn_prompt_body, tm=tm, final_norm=final_norm),
        grid=(t // tm, nf),
        in_specs=[
            pl.BlockSpec((tm, D_MODEL), lambda i, f: (i, 0)),
            pl.BlockSpec((hl, D_MODEL), lambda i, f: (jnp.maximum(i * (tm // hl) - 1, 0), 0)),
            pl.BlockSpec((1, D_MODEL), lambda i, f: (0, 0)),
            pl.BlockSpec((None, D_MODEL, tf), lambda i, f: (layer, 0, f)),
            pl.BlockSpec((None, D_MODEL, tf), lambda i, f: (layer, 0, nf + f)),
            pl.BlockSpec((None, 3, tf), lambda i, f: (layer, 0, f)),
            pl.BlockSpec((None, 3, tf), lambda i, f: (layer, 0, nf + f)),
            pl.BlockSpec((None, 1, tf), lambda i, f: (layer, 0, f)),
            pl.BlockSpec((None, 1, tf), lambda i, f: (layer, 0, nf + f)),
            pl.BlockSpec((None, tf, D_MODEL), lambda i, f: (layer, f, 0)),
            pl.BlockSpec((1, D_MODEL), lambda i, f: (0, 0)),
        ],
        out_specs=[
            pl.BlockSpec((tm, D_MODEL), lambda i, f: (i, 0), pipeline_mode=pl.Buffered(1)),
            pl.BlockSpec((None, 2, SUBLANE, tf), lambda i, f: (i, 0, 0, f)),
        ],
        out_shape=[jax.ShapeDtypeStruct((t, D_MODEL), F32), jax.ShapeDtypeStruct((t // tm, 2, SUBLANE, D_FF), F32)],
        scratch_shapes=[pltpu.VMEM((hl + tm, D_MODEL), BF16)] + [pltpu.VMEM((hl + tm, tf), F32)] * 2,
        compiler_params=_params(("arbitrary", "arbitrary"), VMEM_MIB["ffn_prompt"]),
        name="ffn_prompt",
    )(x, x, g, w_up, w_up, conv_w, conv_w, cb, cb, w_down, g_final)


def _ffn_sample_body(x_ref, g_ref, sg_ref, sv_ref, wg_ref, wv_ref, cwg_ref, cwv_ref, cbg_ref, cbv_ref, wd_ref, gf_ref,
                     o_ref, tg_ref, tv_ref, h_ref, *, db, final_norm):
    f = pl.program_id(0)
    t = DEC_SEQ * db

    @pl.when(f == 0)
    def _():
        h_ref[...] = (_rms(x_ref[...]) * g_ref[...]).astype(BF16)
        o_ref[...] = x_ref[...]

    h = h_ref[...]
    ug = jnp.dot(h, wg_ref[...], preferred_element_type=F32)
    uv = jnp.dot(h, wv_ref[...], preferred_element_type=F32)
    pg = jnp.concatenate([sg_ref[:, 0, :], sg_ref[:, 1, :], ug], axis=0)
    pv = jnp.concatenate([sv_ref[:, 0, :], sv_ref[:, 1, :], uv], axis=0)
    taps = (lambda u: u[0:t], lambda u: u[db:db + t], lambda u: u[2 * db:2 * db + t])
    act = _conv_gate(pg, pv, cwg_ref[...], cwv_ref[...], cbg_ref[...], cbv_ref[...], taps)
    o_ref[...] += jnp.dot(act.astype(BF16), wd_ref[...], preferred_element_type=F32)

    for j in range(2):
        rows = slice((DEC_SEQ - 2 + j) * db, (DEC_SEQ - 1 + j) * db)
        tg_ref[:, j, :] = ug[rows]
        tv_ref[:, j, :] = uv[rows]
    _finish(o_ref, gf_ref, f == pl.num_programs(0) - 1, final_norm)


def _ffn_sample(x, g, state, w_up, conv_w, conv_b, w_down, g_final, layer, final_norm, tf=FFN_TF):
    t = x.shape[0]
    db = t // DEC_SEQ
    nf = D_FF // tf
    cb = conv_b.reshape(conv_b.shape[0], 1, 2 * D_FF)
    return pl.pallas_call(
        functools.partial(_ffn_sample_body, db=db, final_norm=final_norm),
        grid=(nf,),
        in_specs=[
            pl.BlockSpec((t, D_MODEL), lambda f: (0, 0)),
            pl.BlockSpec((1, D_MODEL), lambda f: (0, 0)),
            pl.BlockSpec((None, db, 2, tf), lambda f: (layer, 0, 0, f)),
            pl.BlockSpec((None, db, 2, tf), lambda f: (layer, 0, 0, nf + f)),
            pl.BlockSpec((None, D_MODEL, tf), lambda f: (layer, 0, f)),
            pl.BlockSpec((None, D_MODEL, tf), lambda f: (layer, 0, nf + f)),
            pl.BlockSpec((None, 3, tf), lambda f: (layer, 0, f)),
            pl.BlockSpec((None, 3, tf), lambda f: (layer, 0, nf + f)),
            pl.BlockSpec((None, 1, tf), lambda f: (layer, 0, f)),
            pl.BlockSpec((None, 1, tf), lambda f: (layer, 0, nf + f)),
            pl.BlockSpec((None, tf, D_MODEL), lambda f: (layer, f, 0)),
            pl.BlockSpec((1, D_MODEL), lambda f: (0, 0)),
        ],
        out_specs=[
            pl.BlockSpec((t, D_MODEL), lambda f: (0, 0)),
            pl.BlockSpec((db, 2, tf), lambda f: (0, 0, f)),
            pl.BlockSpec((db, 2, tf), lambda f: (0, 0, f)),
        ],
        out_shape=[jax.ShapeDtypeStruct((t, D_MODEL), F32)] + [jax.ShapeDtypeStruct((db, 2, D_FF), F32)] * 2,
        scratch_shapes=[pltpu.VMEM((t, D_MODEL), BF16)],
        compiler_params=_params(("arbitrary",), VMEM_MIB["ffn_sample"]),
        name="ffn_sample",
    )(x, g, state, state, w_up, w_up, conv_w, conv_w, cb, cb, w_down, g_final)


def kernel(x_prompt, x_sample, cache_win_k, cache_win_v, state_ssm_re, state_ssm_im, state_ffn_conv,
           g_mix, w_in, g_out_a, g_out_b, g_out_c, gmlp_gv, gmlp_ws, gmlp_bs, ssm_a_re, ssm_a_im,
           ssm_log_dt, ssm_b_re, ssm_b_im, ssm_c_re, ssm_c_im, ssm_d, ssm_w_glu, w_out, g_ffn, w_up,
           conv_w, conv_b, w_down, g_final):
    depth = w_in.shape[0]
    bsz, seq, _ = x_prompt.shape
    db, ds, _ = x_sample.shape
    assert bsz == 1 and ds == DEC_SEQ and seq % (SSM_R * SSM_NC) == 0 and seq >= A_WINDOW_MAX
    n_states = C_GROUPS * SSM_P
    row = lambda v: v.reshape(1, -1)

    g_fin = row(g_final)
    w_in_b, w_glu_b, w_out_b = w_in.astype(BF16), ssm_w_glu.astype(BF16), w_out.astype(BF16)
    w_up_b, w_down_b = w_up.astype(BF16), w_down.astype(BF16)
    xp = x_prompt.reshape(seq, D_MODEL)
    xs = jnp.transpose(x_sample, (1, 0, 2)).reshape(ds * db, D_MODEL)
    outs = {k: [] for k in ("kp", "vp", "ks", "vs", "gv", "rp", "ip", "rs", "is", "cp", "cs")}

    for l in range(depth):
        pw, bdb = _ssm_prep(ssm_a_re[l], ssm_a_im[l], ssm_log_dt[l], ssm_b_re[l], ssm_b_im[l])
        c_re, c_im = _ssm_out_matrix(ssm_c_re[l]), _ssm_out_matrix(ssm_c_im[l])
        d_row = row(ssm_d[l])
        gains = (row(g_out_a[l]), row(g_out_b[l]), row(g_out_c[l]))
        last = l == depth - 1

        p = _proj(xp, row(g_mix[l]), w_in_b, l)
        a = _attn_prompt(p)
        b = _gmlp_prompt(p, row(gmlp_gv[l]), gmlp_ws[l], gmlp_bs[l])
        yg, h_last = _s5_prompt(p, bdb, c_re, c_im, d_row, pw)
        xm = _mix_out(a, b, yg, xp, w_glu_b, *gains, w_out_b, l)
        xp, tail = _ffn_prompt(xm, row(g_ffn[l]), w_up_b, conv_w, conv_b, w_down_b, g_fin, l, final_norm=last)
        keep = min(A_WINDOW_MAX, seq)
        outs["kp"].append(p[seq - keep:, A_WIDTH:2 * A_WIDTH].reshape(1, keep, A_HEADS, HEAD_DIM))
        outs["vp"].append(p[seq - keep:, 2 * A_WIDTH:3 * A_WIDTH].reshape(1, keep, A_HEADS, HEAD_DIM))
        st = SSM_LG_STATES
        outs["rp"].append(h_last[:, 0, :st].reshape(1, C_GROUPS, SSM_P))
        outs["ip"].append(h_last[:, 0, st:].reshape(1, C_GROUPS, SSM_P))
        outs["cp"].append(jnp.transpose(tail[-1, :, SUBLANE - 2:, :], (1, 0, 2)).reshape(1, 2, 2 * D_FF))

        ps = _proj(xs, row(g_mix[l]), w_in_b, l)

        def heads(cols):
            return jnp.transpose(cols.reshape(ds, db, A_HEADS, HEAD_DIM), (1, 0, 2, 3)).reshape(
                db, ds * A_HEADS, HEAD_DIM)

        q3, k3, v3 = (heads(ps[:, i * A_WIDTH:(i + 1) * A_WIDTH]) for i in range(3))
        a3 = _attn_sample(q3, k3, v3, cache_win_k, cache_win_v, l)
        a_s = jnp.transpose(a3.reshape(db, ds, A_WIDTH), (1, 0, 2)).reshape(ds * db, A_WIDTH)
        b_s, gv_s = _gmlp_sample(ps, row(gmlp_gv[l]), gmlp_ws[l], gmlp_bs[l])
        yg_s, hre, him = _s5_sample(ps, state_ssm_re[l].reshape(db, n_states), state_ssm_im[l].reshape(db, n_states),
                                    bdb, c_re, c_im, d_row, pw)
        xm_s = _mix_out(a_s, b_s, yg_s, xs, w_glu_b, *gains, w_out_b, l)
        xs, tail_g, tail_v = _ffn_sample(xm_s, row(g_ffn[l]), state_ffn_conv, w_up_b, conv_w, conv_b, w_down_b,
                                         g_fin, l, final_norm=last)
        outs["ks"].append(k3.reshape(db, ds, A_HEADS, HEAD_DIM))
        outs["vs"].append(v3.reshape(db, ds, A_HEADS, HEAD_DIM))
        outs["gv"].append(jnp.transpose(gv_s.reshape(ds, db, B_WIDTH), (1, 0, 2)))
        outs["rs"].append(hre.reshape(db, C_GROUPS, SSM_P))
        outs["is"].append(him.reshape(db, C_GROUPS, SSM_P))
        outs["cs"].append(jnp.concatenate([tail_g, tail_v], axis=-1))

    y_p = xp.reshape(1, seq, D_MODEL)
    y_s = jnp.transpose(xs.reshape(ds, db, D_MODEL), (1, 0, 2))
    st = lambda k: jnp.stack(outs[k])
    return (y_p, y_s, st("kp"), st("vp"), st("ks"), st("vs"), st("gv"),
            st("rp"), st("ip"), st("rs"), st("is"), st("cp"), st("cs"))
```

```python
import functools
import math

import numpy as np
import jax
import jax.numpy as jnp
from jax import lax
from jax.experimental import pallas as pl
from jax.experimental.pallas import tpu as pltpu

F32 = jnp.float32
BF16 = jnp.bfloat16

D_MODEL = 2048
A_WIDTH = 1024
A_HEADS = 8
HEAD_DIM = 128
A_DILATIONS = (1, 4, 16)
A_SPAN = 128
A_WINDOW_MAX = 2048
SAMPLE_TAIL_ROWS = 512
SAMPLE_FAR_QUARTERS = A_WINDOW_MAX // SAMPLE_TAIL_ROWS - 1
SAMPLE_BATCH_PER_STEP = 2
SAMPLE_STRIDE = max(A_DILATIONS)
SAMPLE_GROUPS = SAMPLE_TAIL_ROWS // SAMPLE_STRIDE
ATTN_UNROLL = {1: 32, 4: 16, 16: 16}
B_WIDTH = 512
B_GROUPS = 4
CHUNK = 128
C_WIDTH = 512
C_GROUPS = 32
C_GROUP_W = 16
SSM_P = 64
N_IN = 3 * A_WIDTH + 2 * B_WIDTH + C_WIDTH
D_FF = 5632
DEC_SEQ = 4
EPS = 1e-6
NEG = -1e30

LANE = 128
COL_Q, COL_K, COL_V = 0, A_WIDTH // LANE, 2 * A_WIDTH // LANE
COL_BU = 3 * A_WIDTH // LANE
COL_BV = COL_BU + B_WIDTH // LANE
COL_CU = COL_BV + B_WIDTH // LANE

SSM_LANE_GROUPS = 4
SSM_LG_STATES = C_GROUPS * SSM_P // SSM_LANE_GROUPS
SSM_R = 64
SSM_NC = 32

MIB = 1024 * 1024
SUBLANE = 8

PROJ_TM, PROJ_TN = 1024, 1536
MIX_TM = 512
FFN_TM, FFN_TF = 1024, 512
GMLP_CHUNKS = 8
VMEM_MIB = dict(proj_in=56, attn_prompt=56, attn_sample=48, gmlp=32, s5_prompt=40, s5_sample=32,
                mix_out=56, ffn_prompt=60, ffn_sample=48)


def _params(sem, vmem_mib, flags=None):
    return pltpu.CompilerParams(dimension_semantics=sem, vmem_limit_bytes=vmem_mib * MIB, flags=flags)


def _gelu(x):
    c = math.sqrt(2.0 / math.pi)
    return x * (0.5 * (1.0 + jnp.tanh(c * (x + 0.044715 * (x * x * x)))))


def _rms(x):
    return x * lax.rsqrt(jnp.mean(x * x, axis=-1, keepdims=True) + EPS)


_NT = (((1,), (1,)), ((), ()))


def _proj_body(x_ref, g_ref, w_ref, o_ref, h_ref):
    @pl.when(pl.program_id(1) == 0)
    def _():
        h_ref[...] = (_rms(x_ref[...]) * g_ref[...]).astype(BF16)

    o_ref[...] = jnp.dot(h_ref[...], w_ref[...], preferred_element_type=F32)


def _proj(x, g, w, layer):
    t = x.shape[0]
    tm, tn = min(PROJ_TM, t), PROJ_TN
    return pl.pallas_call(
        _proj_body,
        grid=(t // tm, N_IN // tn),
        in_specs=[
            pl.BlockSpec((tm, D_MODEL), lambda i, j: (i, 0)),
            pl.BlockSpec((1, D_MODEL), lambda i, j: (0, 0)),
            pl.BlockSpec((None, D_MODEL, tn), lambda i, j: (layer, 0, j)),
        ],
        out_specs=pl.BlockSpec((tm, tn), lambda i, j: (i, j)),
        out_shape=jax.ShapeDtypeStruct((t, N_IN), F32),
        scratch_shapes=[pltpu.VMEM((tm, D_MODEL), BF16)],
        compiler_params=_params(("arbitrary", "arbitrary"), VMEM_MIB["proj_in"]),
        name="proj_in",
    )(x, g, w)


def _attn_prompt_body(q_ref, k_ref, v_ref, o_ref, acc_ref, m_ref, l_ref, *, seq):
    scale = math.log2(math.e) / math.sqrt(HEAD_DIM)
    ii = lax.broadcasted_iota(jnp.int32, (A_SPAN, 2 * A_SPAN), 0)
    jj = lax.broadcasted_iota(jnp.int32, (A_SPAN, 2 * A_SPAN), 1)
    band = (jj >= ii) & (jj <= ii + A_SPAN)
    band_cur = band & (jj >= A_SPAN)
    blk = (A_SPAN, A_SPAN)

    for order, dil in enumerate(sorted(A_DILATIONS, reverse=True)):
        nb = seq // (A_SPAN * dil)
        step = A_SPAN * dil

        def rows(start, dil=dil):
            if dil == 1:
                return pl.ds(pl.multiple_of(start, A_SPAN), A_SPAN)
            return pl.ds(start, A_SPAN, stride=dil)

        def runs(run_list, rows=rows, step=step, first=order == 0):
            work = []
            for r, n0, count in run_list:
                static_start = isinstance(n0, int)
                if not (static_start and n0 == 0):
                    prev = r + jnp.maximum(n0 - 1, 0) * step
                    kp, vp = k_ref[rows(prev), :].astype(BF16), v_ref[rows(prev), :].astype(BF16)
                for u in range(count):
                    base = r + (n0 + u) * step
                    q = (q_ref[rows(base), :] * scale).astype(BF16)
                    kc, vc = k_ref[rows(base), :].astype(BF16), v_ref[rows(base), :].astype(BF16)
                    if u == 0 and static_start and n0 == 0:
                        kp, vp, mask = kc, vc, band_cur
                    elif u == 0 and not static_start:
                        mask = band & ((jj >= A_SPAN) | (n0 > 0))
                    else:
                        mask = band
                    s = lax.dot_general(q, jnp.concatenate([kp, kc], axis=0), _NT,
                                        preferred_element_type=F32)
                    s = jnp.where(mask, s, NEG)
                    v2 = jnp.concatenate([vp, vc], axis=0)
                    m_blk = jnp.max(s, axis=1, keepdims=True)
                    if first:
                        m_new = jnp.broadcast_to(m_blk, blk)
                        p = jnp.exp2(s - m_blk)
                        l_new = jnp.broadcast_to(jnp.sum(p, axis=1, keepdims=True), blk)
                        acc_new = jnp.dot(p.astype(BF16), v2, preferred_element_type=F32)
                    else:
                        m_old = m_ref[rows(base), :]
                        m_new = jnp.maximum(m_old, m_blk)
                        alpha = jnp.exp2(m_old - m_new)
                        p = jnp.exp2(s - jnp.concatenate([m_new, m_new], axis=1))
                        l_new = alpha * l_ref[rows(base), :] + jnp.sum(p, axis=1, keepdims=True)
                        acc_new = alpha * acc_ref[rows(base), :] + jnp.dot(p.astype(BF16), v2,
                                                                           preferred_element_type=F32)
                    work.append((base, m_new, l_new, acc_new))
                    kp, vp = kc, vc
            for base, m_new, l_new, acc_new in work:
                m_ref[rows(base), :] = m_new
                l_ref[rows(base), :] = l_new
                acc_ref[rows(base), :] = acc_new

        unroll = min(ATTN_UNROLL[dil], nb * dil)
        if unroll <= nb:
            per_res = nb // unroll

            def trip(i, c, runs=runs, unroll=unroll, per_res=per_res):
                runs([(i // per_res, (i % per_res) * unroll, unroll)])
                return c
        else:
            res_per_trip = unroll // nb

            def trip(i, c, runs=runs, nb=nb, res_per_trip=res_per_trip):
                runs([(i * res_per_trip + j, 0, nb) for j in range(res_per_trip)])
                return c

        lax.fori_loop(0, nb * dil // unroll, trip, 0)

    o_ref[...] = acc_ref[...] / l_ref[...]


def _attn_prompt(p):
    seq = p.shape[0]
    blk = (seq, HEAD_DIM)
    return pl.pallas_call(
        functools.partial(_attn_prompt_body, seq=seq),
        grid=(A_HEADS,),
        in_specs=[
            pl.BlockSpec(blk, lambda h: (0, COL_Q + h)),
            pl.BlockSpec(blk, lambda h: (0, COL_K + h)),
            pl.BlockSpec(blk, lambda h: (0, COL_V + h)),
        ],
        out_specs=pl.BlockSpec(blk, lambda h: (0, h)),
        out_shape=jax.ShapeDtypeStruct((seq, A_WIDTH), F32),
        scratch_shapes=[pltpu.VMEM(blk, F32)] * 3,
        compiler_params=_params(("arbitrary",), VMEM_MIB["attn_prompt"]),
        name="attn_prompt",
    )(p, p, p)


def _sample_attn_masks():
    j = np.arange(DEC_SEQ)[:, None, None, None]
    h = np.arange(A_HEADS)[None, :, None, None]
    hk = np.arange(A_HEADS)[None, None, None, :]
    same = (h == hk)
    i = np.arange(512)[None, None, :, None]
    cnt_l = same * ((i >= 384 + j).astype(np.int32) + ((i - j) % 4 == 0).astype(np.int32))
    jn = np.arange(DEC_SEQ)[None, None, :, None]
    cnt_n = same * ((jn <= j).astype(np.int32) + 2 * (jn == j).astype(np.int32))
    g = np.arange(128)[None, None, :, None]
    head_match = np.broadcast_to(same[0:1], (1, A_HEADS, 128, A_HEADS))
    return (cnt_l.reshape(DEC_SEQ * A_HEADS, 512 * A_HEADS).astype(np.float32),
            cnt_n.reshape(DEC_SEQ * A_HEADS, DEC_SEQ * A_HEADS).astype(np.float32),
            (head_match + 0 * g).reshape(A_HEADS, 128 * A_HEADS).astype(np.float32))


def _attn_sample_body(q_ref, k_ref, v_ref, *refs):
    nh = A_HEADS
    nq = SAMPLE_FAR_QUARTERS
    k_far, kl_ref = refs[:nq], refs[nq]
    v_far, vl_ref = refs[nq + 1:2 * nq + 1], refs[2 * nq + 1]
    cl_ref, cn_ref, hm_ref, o_ref = refs[2 * nq + 2:]
    cl, cn, hm = cl_ref[...], cn_ref[...], hm_ref[...]

    def stride16(far, last, bi, res):
        grp, stride = SAMPLE_GROUPS, SAMPLE_STRIDE
        near = last.reshape(grp, stride, nh, HEAD_DIM)[:, res].reshape(grp * nh, HEAD_DIM)
        parts = [far[qt][bi, :, res].reshape(grp * nh, HEAD_DIM).astype(BF16) for qt in range(nq)]
        return jnp.concatenate(parts + [near.astype(BF16)], axis=0)

    for bi in range(SAMPLE_BATCH_PER_STEP):
        q = q_ref[bi] * (1.0 / math.sqrt(HEAD_DIM))
        qs = q.astype(BF16)
        k_last = kl_ref[bi]
        kl = k_last.reshape(SAMPLE_TAIL_ROWS * nh, HEAD_DIM).astype(BF16)
        s_l = lax.dot_general(qs, kl, _NT, preferred_element_type=F32)
        s_n = lax.dot_general(qs, k_ref[bi].astype(BF16), _NT, preferred_element_type=F32)
        s_a = []
        for res in range(DEC_SEQ):
            kk = stride16(k_far, k_last, bi, res)
            s_a.append(lax.dot_general(q[res * nh:(res + 1) * nh].astype(BF16), kk, _NT,
                                       preferred_element_type=F32))
        m = jnp.max(jnp.where(cl > 0, s_l, NEG), axis=1, keepdims=True)
        m = jnp.maximum(m, jnp.max(jnp.where(cn > 0, s_n, NEG), axis=1, keepdims=True))
        m_a = jnp.concatenate([jnp.max(jnp.where(hm > 0, s, NEG), axis=1, keepdims=True) for s in s_a], axis=0)
        m = jnp.maximum(m, m_a)

        p_l = cl * jnp.exp(jnp.minimum(s_l - m, 0.0))
        p_n = cn * jnp.exp(jnp.minimum(s_n - m, 0.0))
        den = jnp.sum(p_l, axis=1, keepdims=True) + jnp.sum(p_n, axis=1, keepdims=True)
        v_last = vl_ref[bi]
        vl = v_last.reshape(SAMPLE_TAIL_ROWS * nh, HEAD_DIM).astype(BF16)
        out = jnp.dot(p_l.astype(BF16), vl, preferred_element_type=F32)
        out = out + jnp.dot(p_n.astype(BF16), v_ref[bi].astype(BF16), preferred_element_type=F32)
        o_a, d_a = [], []
        for res in range(DEC_SEQ):
            p_a = hm * jnp.exp(jnp.minimum(s_a[res] - m[res * nh:(res + 1) * nh], 0.0))
            d_a.append(jnp.sum(p_a, axis=1, keepdims=True))
            vv = stride16(v_far, v_last, bi, res)
            o_a.append(jnp.dot(p_a.astype(BF16), vv, preferred_element_type=F32))
        out = out + jnp.concatenate(o_a, axis=0)
        den = den + jnp.concatenate(d_a, axis=0)
        o_ref[bi] = out / den


def _attn_sample(q3, k3, v3, cache_k, cache_v, layer):
    db = q3.shape[0]
    depth, _, w_buf, nh, hd = cache_k.shape
    assert (w_buf, nh, hd) == (A_WINDOW_MAX, A_HEADS, HEAD_DIM) and q3.shape[1] == DEC_SEQ * A_HEADS
    cnt_l, cnt_n, head_match = _sample_attn_masks()
    rows = DEC_SEQ * A_HEADS

    nq = SAMPLE_FAR_QUARTERS
    nb = min(SAMPLE_BATCH_PER_STEP, db)
    assert nb == SAMPLE_BATCH_PER_STEP and db % nb == 0

    def strided_specs():
        return [pl.BlockSpec((None, nb, None, SAMPLE_GROUPS, DEC_SEQ, nh, hd), functools.partial(
            lambda s, qt: (layer, s, qt, 0, 0, 0, 0), qt=qt)) for qt in range(nq)]

    last_spec = pl.BlockSpec((None, nb, None, SAMPLE_TAIL_ROWS, nh, hd), lambda s: (layer, s, nq, 0, 0, 0))
    tok_spec = pl.BlockSpec((nb, rows, hd), lambda s: (s, 0, 0))

    def whole(a):
        return pl.BlockSpec(a.shape, lambda s: (0, 0))

    k16 = cache_k.reshape(depth, db, nq + 1, SAMPLE_GROUPS, SAMPLE_STRIDE, nh, hd)
    v16 = cache_v.reshape(depth, db, nq + 1, SAMPLE_GROUPS, SAMPLE_STRIDE, nh, hd)
    k512 = cache_k.reshape(depth, db, nq + 1, SAMPLE_TAIL_ROWS, nh, hd)
    v512 = cache_v.reshape(depth, db, nq + 1, SAMPLE_TAIL_ROWS, nh, hd)
    return pl.pallas_call(
        _attn_sample_body,
        grid=(db // nb,),
        in_specs=[tok_spec, tok_spec, tok_spec] + strided_specs() + [last_spec] + strided_specs() + [last_spec]
        + [whole(cnt_l), whole(cnt_n), whole(head_match)],
        out_specs=tok_spec,
        out_shape=jax.ShapeDtypeStruct((db, rows, hd), F32),
        compiler_params=_params(("arbitrary",), VMEM_MIB["attn_sample"]),
        name="attn_sample",
    )(q3, k3, v3, *([k16] * nq), k512, *([v16] * nq), v512,
      jnp.asarray(cnt_l), jnp.asarray(cnt_n), jnp.asarray(head_match))


def _gmlp_prompt_body(bu_ref, bv_ref, gg_ref, ws_ref, bs_ref, o_ref, *, chunks):
    ri = lax.broadcasted_iota(jnp.int32, (CHUNK, CHUNK), 0)
    ci = lax.broadcasted_iota(jnp.int32, (CHUNK, CHUNK), 1)
    wm = jnp.where(ci <= ri, ws_ref[0], 0.0).astype(BF16)
    bias = bs_ref[0]
    gain = gg_ref[...]
    for c in range(chunks):
        sl = slice(c * CHUNK, (c + 1) * CHUNK)
        gv = _rms(_gelu(bv_ref[sl, :])) * gain
        mix = jnp.dot(wm, gv.astype(BF16), preferred_element_type=F32) + bias
        o_ref[sl, :] = _gelu(bu_ref[sl, :]) * mix


def _gmlp_prompt(p, gain, ws, bs):
    seq = p.shape[0]
    chunks = GMLP_CHUNKS
    tm = chunks * CHUNK
    return pl.pallas_call(
        functools.partial(_gmlp_prompt_body, chunks=chunks),
        grid=(seq // tm, B_GROUPS),
        in_specs=[
            pl.BlockSpec((tm, LANE), lambda i, g: (i, COL_BU + g)),
            pl.BlockSpec((tm, LANE), lambda i, g: (i, COL_BV + g)),
            pl.BlockSpec((1, LANE), lambda i, g: (0, g)),
            pl.BlockSpec((1, CHUNK, CHUNK), lambda i, g: (g, 0, 0)),
            pl.BlockSpec((1, CHUNK, 1), lambda i, g: (g, 0, 0)),
        ],
        out_specs=pl.BlockSpec((tm, LANE), lambda i, g: (i, g)),
        out_shape=jax.ShapeDtypeStruct((seq, B_WIDTH), F32),
        compiler_params=_params(("arbitrary", "arbitrary"), VMEM_MIB["gmlp"]),
        name="gmlp_prompt",
    )(p, p, gain, ws, bs.reshape(B_GROUPS, CHUNK, 1))


def _gmlp_sample_body(ws_ref, bs_ref, bu_ref, bv_ref, gg_ref, o_ref, gv_ref, *, db):
    g = pl.program_id(0)
    gv = _rms(_gelu(bv_ref[...])) * gg_ref[...]
    gv_ref[...] = gv
    gu = _gelu(bu_ref[...])
    for i in range(DEC_SEQ):
        mix = jnp.full((db, LANE), bs_ref[g, i], F32)
        for j in range(i + 1):
            mix = mix + ws_ref[g, i * DEC_SEQ + j] * gv[j * db:(j + 1) * db]
        o_ref[i * db:(i + 1) * db, :] = gu[i * db:(i + 1) * db] * mix


def _gmlp_sample(p, gain, ws, bs):
    t = p.shape[0]
    db = t // DEC_SEQ
    ws4 = ws[:, :DEC_SEQ, :DEC_SEQ].reshape(B_GROUPS, DEC_SEQ * DEC_SEQ)
    bs4 = bs[:, :DEC_SEQ]
    smem = pl.BlockSpec(memory_space=pltpu.SMEM)
    return pl.pallas_call(
        functools.partial(_gmlp_sample_body, db=db),
        grid=(B_GROUPS,),
        in_specs=[
            smem, smem,
            pl.BlockSpec((t, LANE), lambda g: (0, COL_BU + g)),
            pl.BlockSpec((t, LANE), lambda g: (0, COL_BV + g)),
            pl.BlockSpec((1, LANE), lambda g: (0, g)),
        ],
        out_specs=[pl.BlockSpec((t, LANE), lambda g: (0, g))] * 2,
        out_shape=[jax.ShapeDtypeStruct((t, B_WIDTH), F32)] * 2,
        compiler_params=_params(("arbitrary",), VMEM_MIB["gmlp"]),
        name="gmlp_sample",
    )(ws4, bs4, p, p, gain)


def _ssm_prep_body(lre_ref, lim_ref, ldt_ref, bre_ref, bim_ref, pwre_ref, pwim_ref, bbre_ref, bbim_ref):
    lre, lim = lre_ref[...], lim_ref[...]
    dt = jnp.exp(ldt_ref[...])
    k = lax.broadcasted_iota(jnp.int32, pwre_ref.shape, 0).astype(F32) + 1.0
    mag = jnp.exp(k * (dt * lre))
    ang = k * (dt * lim)
    pwre = mag * jnp.cos(ang)
    pwim = mag * jnp.sin(ang)
    pwre_ref[...] = pwre
    pwim_ref[...] = pwim
    xr, xi = pwre[0:1] - 1.0, pwim[0:1]
    den = lre * lre + lim * lim
    cr = (xr * lre + xi * lim) / den
    ci = (xi * lre - xr * lim) / den
    bre, bim = bre_ref[...], bim_ref[...]
    bbre_ref[...] = cr * bre - ci * bim
    bbim_ref[...] = cr * bim + ci * bre


def _ssm_prep(a_re, a_im, log_dt, b_re, b_im):
    n = C_GROUPS * SSM_P
    row = lambda a: a.reshape(1, n)
    bt = lambda b: jnp.transpose(b, (2, 0, 1)).reshape(C_GROUP_W, n)
    shapes = [jax.ShapeDtypeStruct((SSM_R, n), F32)] * 2 + [jax.ShapeDtypeStruct((C_GROUP_W, n), F32)] * 2
    pwre, pwim, bbre, bbim = pl.pallas_call(
        _ssm_prep_body, out_shape=shapes, name="ssm_prep",
    )(row(a_re), row(a_im), row(jnp.repeat(log_dt, SSM_P)), bt(b_re), bt(b_im))
    lg, st = SSM_LANE_GROUPS, SSM_LG_STATES
    split = lambda t: jnp.transpose(t.reshape(t.shape[0], lg, st), (1, 0, 2))
    pw = jnp.concatenate([split(pwre), split(pwim)], axis=-1)
    eye = jnp.eye(C_GROUPS // lg, dtype=F32)

    def blockdiag(bb):
        r = bb.reshape(C_GROUP_W, lg, C_GROUPS // lg, SSM_P)
        return jnp.einsum('ab,mlbp->lambp', eye, r).reshape(lg, LANE, st)

    bdb = jnp.concatenate([blockdiag(bbre), blockdiag(bbim)], axis=-1).astype(BF16)
    return pw, bdb


def _ssm_out_matrix(c):
    lg = SSM_LANE_GROUPS
    eye = jnp.eye(C_GROUPS // lg, dtype=F32)
    r = c.reshape(lg, C_GROUPS // lg, C_GROUP_W, SSM_P)
    return jnp.einsum('ab,lanp->lapbn', eye, r).reshape(lg, SSM_LG_STATES, LANE).astype(BF16)


def _cmul_add(are, aim, hre, him, xre, xim):
    return are * hre - aim * him + xre, are * him + aim * hre + xim


def _s5_prompt_body(u_ref, bdb_ref, cre_ref, cim_ref, d_ref, pw_ref, yg_ref, hl_ref,
                    u3_ref, h_ref, car_ref, hc_ref):
    r_steps, nc, st = SSM_R, SSM_NC, SSM_LG_STATES
    re, im = slice(0, st), slice(st, 2 * st)

    @pl.when(pl.program_id(1) == 0)
    def _():
        hc_ref[...] = jnp.zeros(hc_ref.shape, F32)

    def rows(r):
        return pl.ds(pl.multiple_of(r * nc, nc), nc)

    def regroup(r, c):
        u3_ref[rows(r), :] = u_ref[pl.ds(r, nc, stride=r_steps), :]
        return c

    lax.fori_loop(0, r_steps, regroup, 0)
    h_ref[...] = jnp.dot(u3_ref[...].astype(BF16), bdb_ref[0], preferred_element_type=F32)

    half = st // 2
    for hf in range(2):
        cre_ = slice(hf * half, (hf + 1) * half)
        cim_ = slice(st + hf * half, st + (hf + 1) * half)
        are = jnp.broadcast_to(pw_ref[0, 0:1, cre_], (nc, half))
        aim = jnp.broadcast_to(pw_ref[0, 0:1, cim_], (nc, half))

        def step(r, carry, cre_=cre_, cim_=cim_, are=are, aim=aim):
            hre, him = _cmul_add(are, aim, carry[0], carry[1], h_ref[rows(r), cre_], h_ref[rows(r), cim_])
            h_ref[rows(r), cre_] = hre
            h_ref[rows(r), cim_] = him
            return hre, him

        lax.fori_loop(1, r_steps, step, (h_ref[0:nc, cre_], h_ref[0:nc, cim_]))

    ends = h_ref[(r_steps - 1) * nc:r_steps * nc, :]
    are, aim = pw_ref[0, r_steps - 1:r_steps, re], pw_ref[0, r_steps - 1:r_steps, im]
    cre, cim = hc_ref[0:1, re], hc_ref[0:1, im]
    for c in range(nc):
        car_ref[c:c + 1, re] = cre
        car_ref[c:c + 1, im] = cim
        cre, cim = _cmul_add(are, aim, cre, cim, ends[c:c + 1, re], ends[c:c + 1, im])
    hc_ref[:, re] = jnp.broadcast_to(cre, (SUBLANE, st))
    hc_ref[:, im] = jnp.broadcast_to(cim, (SUBLANE, st))
    hl_ref[0] = hc_ref[...]

    def fix(r, c):
        pre, pim = pw_ref[0, pl.ds(r, 1), re], pw_ref[0, pl.ds(r, 1), im]
        hre, him = _cmul_add(pre, pim, car_ref[:, re], car_ref[:, im], h_ref[rows(r), re], h_ref[rows(r), im])
        h_ref[rows(r), re] = hre
        h_ref[rows(r), im] = him
        return c

    lax.fori_loop(0, r_steps, fix, 0)

    y = (jnp.dot(h_ref[:, re].astype(BF16), cre_ref[0], preferred_element_type=F32)
         - jnp.dot(h_ref[:, im].astype(BF16), cim_ref[0], preferred_element_type=F32)
         + d_ref[...] * u3_ref[...])
    u3_ref[...] = _gelu(y)

    def ungroup(r, c):
        yg_ref[pl.ds(r, nc, stride=r_steps), :] = u3_ref[rows(r), :]
        return c

    lax.fori_loop(0, r_steps, ungroup, 0)


def _s5_prompt(p, bdb, cre, cim, d, pw):
    seq = p.shape[0]
    tseg = SSM_R * SSM_NC
    st2 = 2 * SSM_LG_STATES
    lgs = SSM_LANE_GROUPS
    return pl.pallas_call(
        _s5_prompt_body,
        grid=(lgs, seq // tseg),
        in_specs=[
            pl.BlockSpec((tseg, LANE), lambda g, t: (t, COL_CU + g)),
            pl.BlockSpec((1, LANE, st2), lambda g, t: (g, 0, 0)),
            pl.BlockSpec((1, SSM_LG_STATES, LANE), lambda g, t: (g, 0, 0)),
            pl.BlockSpec((1, SSM_LG_STATES, LANE), lambda g, t: (g, 0, 0)),
            pl.BlockSpec((1, LANE), lambda g, t: (0, g)),
            pl.BlockSpec((1, SSM_R, st2), lambda g, t: (g, 0, 0)),
        ],
        out_specs=[
            pl.BlockSpec((tseg, LANE), lambda g, t: (t, g)),
            pl.BlockSpec((1, SUBLANE, st2), lambda g, t: (g, 0, 0)),
        ],
        out_shape=[jax.ShapeDtypeStruct((seq, C_WIDTH), F32), jax.ShapeDtypeStruct((lgs, SUBLANE, st2), F32)],
        scratch_shapes=[
            pltpu.VMEM((tseg, LANE), F32),
            pltpu.VMEM((tseg, st2), F32),
            pltpu.VMEM((SSM_NC, st2), F32),
            pltpu.VMEM((SUBLANE, st2), F32),
        ],
        compiler_params=_params(("arbitrary", "arbitrary"), VMEM_MIB["s5_prompt"]),
        name="s5_prompt",
    )(p, bdb, cre, cim, d, pw)


def _s5_sample_body(u_ref, hre_ref, him_ref, bdb_ref, cre_ref, cim_ref, d_ref, pw_ref,
                    yg_ref, ore_ref, oim_ref, h_ref, *, db):
    st = SSM_LG_STATES
    re, im = slice(0, st), slice(st, 2 * st)
    u = u_ref[...]
    x = jnp.dot(u.astype(BF16), bdb_ref[0], preferred_element_type=F32)
    are, aim = pw_ref[0, 0:1, re], pw_ref[0, 0:1, im]
    hre, him = hre_ref[...], him_ref[...]
    for j in range(DEC_SEQ):
        rj = slice(j * db, (j + 1) * db)
        hre, him = _cmul_add(are, aim, hre, him, x[rj, re], x[rj, im])
        h_ref[rj, re] = hre
        h_ref[rj, im] = him
    ore_ref[...] = hre
    oim_ref[...] = him
    y = (jnp.dot(h_ref[:, re].astype(BF16), cre_ref[0], preferred_element_type=F32)
         - jnp.dot(h_ref[:, im].astype(BF16), cim_ref[0], preferred_element_type=F32)
         + d_ref[...] * u)
    yg_ref[...] = _gelu(y)


def _s5_sample(p, h0re, h0im, bdb, cre, cim, d, pw):
    t = p.shape[0]
    db = t // DEC_SEQ
    st = SSM_LG_STATES
    lgs = SSM_LANE_GROUPS
    return pl.pallas_call(
        functools.partial(_s5_sample_body, db=db),
        grid=(lgs,),
        in_specs=[
            pl.BlockSpec((t, LANE), lambda g: (0, COL_CU + g)),
            pl.BlockSpec((db, st), lambda g: (0, g)),
            pl.BlockSpec((db, st), lambda g: (0, g)),
            pl.BlockSpec((1, LANE, 2 * st), lambda g: (g, 0, 0)),
            pl.BlockSpec((1, st, LANE), lambda g: (g, 0, 0)),
            pl.BlockSpec((1, st, LANE), lambda g: (g, 0, 0)),
            pl.BlockSpec((1, LANE), lambda g: (0, g)),
            pl.BlockSpec((1, SSM_R, 2 * st), lambda g: (g, 0, 0)),
        ],
        out_specs=[
            pl.BlockSpec((t, LANE), lambda g: (0, g)),
            pl.BlockSpec((db, st), lambda g: (0, g)),
            pl.BlockSpec((db, st), lambda g: (0, g)),
        ],
        out_shape=[jax.ShapeDtypeStruct((t, C_WIDTH), F32),
                   jax.ShapeDtypeStruct((db, lgs * st), F32), jax.ShapeDtypeStruct((db, lgs * st), F32)],
        scratch_shapes=[pltpu.VMEM((t, 2 * st), F32)],
        compiler_params=_params(("arbitrary",), VMEM_MIB["s5_sample"]),
        name="s5_sample",
    )(p, h0re, h0im, bdb, cre, cim, d, pw)


def _mix_out_body(a_ref, b_ref, y_ref, x_ref, wg_ref, ga_ref, gb_ref, gc_ref, wo_ref, o_ref):
    glu = jnp.dot(y_ref[...].astype(BF16), wg_ref[...], preferred_element_type=F32)
    c = glu[:, :C_WIDTH] * jax.nn.sigmoid(glu[:, C_WIDTH:])
    mixed = jnp.concatenate([
        (_rms(a_ref[...]) * ga_ref[...]).astype(BF16),
        (_rms(b_ref[...]) * gb_ref[...]).astype(BF16),
        (_rms(c) * gc_ref[...]).astype(BF16)], axis=1)
    o_ref[...] = x_ref[...] + jnp.dot(mixed, wo_ref[...], preferred_element_type=F32)


def _mix_out(a, b, y, x, w_glu, g_a, g_b, g_c, w_out, layer):
    t = x.shape[0]
    tm = min(MIX_TM, t)
    row = lambda w: pl.BlockSpec((tm, w), lambda i: (i, 0))
    whole = lambda arr: pl.BlockSpec(arr.shape, lambda i: (0, 0))
    of_layer = lambda arr: pl.BlockSpec((None,) + arr.shape[1:], lambda i: (layer, 0, 0))
    return pl.pallas_call(
        _mix_out_body,
        grid=(t // tm,),
        in_specs=[row(A_WIDTH), row(B_WIDTH), row(C_WIDTH), row(D_MODEL),
                  of_layer(w_glu), whole(g_a), whole(g_b), whole(g_c), of_layer(w_out)],
        out_specs=row(D_MODEL),
        out_shape=jax.ShapeDtypeStruct((t, D_MODEL), F32),
        compiler_params=_params(("arbitrary",), VMEM_MIB["mix_out"]),
        name="mix_out",
    )(a, b, y, x, w_glu, g_a, g_b, g_c, w_out)


FFN_HALO = 16


def _conv_gate(ug, uv, cwg, cwv, cbg, cbv, taps):
    def conv(u, cw, cb):
        return ((cb + cw[0:1, :] * taps[0](u)) + cw[1:2, :] * taps[1](u)) + cw[2:3, :] * taps[2](u)
    gate = conv(ug, cwg, cbg)
    return (gate * jax.nn.sigmoid(gate)) * conv(uv, cwv, cbv)


def _finish(o_ref, gf_ref, is_last_tile, final_norm):
    if final_norm:
        @pl.when(is_last_tile)
        def _():
            o_ref[...] = _rms(o_ref[...]) * gf_ref[...]


def _ffn_prompt_body(x_ref, halo_ref, g_ref, wg_ref, wv_ref, cwg_ref, cwv_ref, cbg_ref, cbv_ref, wd_ref, gf_ref,
                     o_ref, tail_ref, h_ref, ug_ref, uv_ref, *, tm, final_norm):
    i, f = pl.program_id(0), pl.program_id(1)
    hl = FFN_HALO

    @pl.when(f == 0)
    def _():
        h_ref[hl:, :] = (_rms(x_ref[...]) * g_ref[...]).astype(BF16)
        prev = _rms(halo_ref[...]) * g_ref[...]
        h_ref[0:hl, :] = jnp.where(i > 0, prev, 0.0).astype(BF16)
        o_ref[...] = x_ref[...]

    h = h_ref[...]
    ug_ref[...] = jnp.dot(h, wg_ref[...], preferred_element_type=F32)
    uv_ref[...] = jnp.dot(h, wv_ref[...], preferred_element_type=F32)
    taps = tuple((lambda u_ref, off=off: u_ref[pl.ds(hl - 2 + off, tm), :]) for off in range(3))
    act = _conv_gate(ug_ref, uv_ref, cwg_ref[...], cwv_ref[...], cbg_ref[...], cbv_ref[...], taps)
    o_ref[...] += jnp.dot(act.astype(BF16), wd_ref[...], preferred_element_type=F32)
    tail_ref[0] = ug_ref[pl.ds(hl + tm - SUBLANE, SUBLANE), :]
    tail_ref[1] = uv_ref[pl.ds(hl + tm - SUBLANE, SUBLANE), :]
    _finish(o_ref, gf_ref, f == pl.num_programs(1) - 1, final_norm)


def _ffn_prompt(x, g, w_up, conv_w, conv_b, w_down, g_final, layer, final_norm):
    t = x.shape[0]
    tm, tf = min(FFN_TM, t), FFN_TF
    nf = D_FF // tf
    hl = FFN_HALO
    cb = conv_b.reshape(conv_b.shape[0], 1, 2 * D_FF)
    return pl.pallas_call(
        functools.partial(_ffn_prompt_body, tm=tm, final_norm=final_norm),
        grid=(t // tm, nf),
        in_specs=[
            pl.BlockSpec((tm, D_MODEL), lambda i, f: (i, 0)),
            pl.BlockSpec((hl, D_MODEL), lambda i, f: (jnp.maximum(i * (tm // hl) - 1, 0), 0)),
            pl.BlockSpec((1, D_MODEL), lambda i, f: (0, 0)),
            pl.BlockSpec((None, D_MODEL, tf), lambda i, f: (layer, 0, f)),
            pl.BlockSpec((None, D_MODEL, tf), lambda i, f: (layer, 0, nf + f)),
            pl.BlockSpec((None, 3, tf), lambda i, f: (layer, 0, f)),
            pl.BlockSpec((None, 3, tf), lambda i, f: (layer, 0, nf + f)),
            pl.BlockSpec((None, 1, tf), lambda i, f: (layer, 0, f)),
            pl.BlockSpec((None, 1, tf), lambda i, f: (layer, 0, nf + f)),
            pl.BlockSpec((None, tf, D_MODEL), lambda i, f: (layer, f, 0)),
            pl.BlockSpec((1, D_MODEL), lambda i, f: (0, 0)),
        ],
        out_specs=[
            pl.BlockSpec((tm, D_MODEL), lambda i, f: (i, 0), pipeline_mode=pl.Buffered(1)),
            pl.BlockSpec((None, 2, SUBLANE, tf), lambda i, f: (i, 0, 0, f)),
        ],
        out_shape=[jax.ShapeDtypeStruct((t, D_MODEL), F32), jax.ShapeDtypeStruct((t // tm, 2, SUBLANE, D_FF), F32)],
        scratch_shapes=[pltpu.VMEM((hl + tm, D_MODEL), BF16)] + [pltpu.VMEM((hl + tm, tf), F32)] * 2,
        compiler_params=_params(("arbitrary", "arbitrary"), VMEM_MIB["ffn_prompt"]),
        name="ffn_prompt",
    )(x, x, g, w_up, w_up, conv_w, conv_w, cb, cb, w_down, g_final)


def _ffn_sample_body(x_ref, g_ref, sg_ref, sv_ref, wg_ref, wv_ref, cwg_ref, cwv_ref, cbg_ref, cbv_ref, wd_ref, gf_ref,
                     o_ref, tg_ref, tv_ref, h_ref, *, db, final_norm):
    f = pl.program_id(0)
    t = DEC_SEQ * db

    @pl.when(f == 0)
    def _():
        h_ref[...] = (_rms(x_ref[...]) * g_ref[...]).astype(BF16)
        o_ref[...] = x_ref[...]

    h = h_ref[...]
    ug = jnp.dot(h, wg_ref[...], preferred_element_type=F32)
    uv = jnp.dot(h, wv_ref[...], preferred_element_type=F32)
    pg = jnp.concatenate([sg_ref[:, 0, :], sg_ref[:, 1, :], ug], axis=0)
    pv = jnp.concatenate([sv_ref[:, 0, :], sv_ref[:, 1, :], uv], axis=0)
    taps = (lambda u: u[0:t], lambda u: u[db:db + t], lambda u: u[2 * db:2 * db + t])
    act = _conv_gate(pg, pv, cwg_ref[...], cwv_ref[...], cbg_ref[...], cbv_ref[...], taps)
    o_ref[...] += jnp.dot(act.astype(BF16), wd_ref[...], preferred_element_type=F32)

    for j in range(2):
        rows = slice((DEC_SEQ - 2 + j) * db, (DEC_SEQ - 1 + j) * db)
        tg_ref[:, j, :] = ug[rows]
        tv_ref[:, j, :] = uv[rows]
    _finish(o_ref, gf_ref, f == pl.num_programs(0) - 1, final_norm)


def _ffn_sample(x, g, state, w_up, conv_w, conv_b, w_down, g_final, layer, final_norm, tf=FFN_TF):
    t = x.shape[0]
    db = t // DEC_SEQ
    nf = D_FF // tf
    cb = conv_b.reshape(conv_b.shape[0], 1, 2 * D_FF)
    return pl.pallas_call(
        functools.partial(_ffn_sample_body, db=db, final_norm=final_norm),
        grid=(nf,),
        in_specs=[
            pl.BlockSpec((t, D_MODEL), lambda f: (0, 0)),
            pl.BlockSpec((1, D_MODEL), lambda f: (0, 0)),
            pl.BlockSpec((None, db, 2, tf), lambda f: (layer, 0, 0, f)),
            pl.BlockSpec((None, db, 2, tf), lambda f: (layer, 0, 0, nf + f)),
            pl.BlockSpec((None, D_MODEL, tf), lambda f: (layer, 0, f)),
            pl.BlockSpec((None, D_MODEL, tf), lambda f: (layer, 0, nf + f)),
            pl.BlockSpec((None, 3, tf), lambda f: (layer, 0, f)),
            pl.BlockSpec((None, 3, tf), lambda f: (layer, 0, nf + f)),
            pl.BlockSpec((None, 1, tf), lambda f: (layer, 0, f)),
            pl.BlockSpec((None, 1, tf), lambda f: (layer, 0, nf + f)),
            pl.BlockSpec((None, tf, D_MODEL), lambda f: (layer, f, 0)),
            pl.BlockSpec((1, D_MODEL), lambda f: (0, 0)),
        ],
        out_specs=[
            pl.BlockSpec((t, D_MODEL), lambda f: (0, 0)),
            pl.BlockSpec((db, 2, tf), lambda f: (0, 0, f)),
            pl.BlockSpec((db, 2, tf), lambda f: (0, 0, f)),
        ],
        out_shape=[jax.ShapeDtypeStruct((t, D_MODEL), F32)] + [jax.ShapeDtypeStruct((db, 2, D_FF), F32)] * 2,
        scratch_shapes=[pltpu.VMEM((t, D_MODEL), BF16)],
        compiler_params=_params(("arbitrary",), VMEM_MIB["ffn_sample"]),
        name="ffn_sample",
    )(x, g, state, state, w_up, w_up, conv_w, conv_w, cb, cb, w_down, g_final)


def kernel(x_prompt, x_sample, cache_win_k, cache_win_v, state_ssm_re, state_ssm_im, state_ffn_conv,
           g_mix, w_in, g_out_a, g_out_b, g_out_c, gmlp_gv, gmlp_ws, gmlp_bs, ssm_a_re, ssm_a_im,
           ssm_log_dt, ssm_b_re, ssm_b_im, ssm_c_re, ssm_c_im, ssm_d, ssm_w_glu, w_out, g_ffn, w_up,
           conv_w, conv_b, w_down, g_final):
    depth = w_in.shape[0]
    bsz, seq, _ = x_prompt.shape
    db, ds, _ = x_sample.shape
    assert bsz == 1 and ds == DEC_SEQ and seq % (SSM_R * SSM_NC) == 0 and seq >= A_WINDOW_MAX
    n_states = C_GROUPS * SSM_P
    row = lambda v: v.reshape(1, -1)

    g_fin = row(g_final)
    w_in_b, w_glu_b, w_out_b = w_in.astype(BF16), ssm_w_glu.astype(BF16), w_out.astype(BF16)
    w_up_b, w_down_b = w_up.astype(BF16), w_down.astype(BF16)
    xp = x_prompt.reshape(seq, D_MODEL)
    xs = jnp.transpose(x_sample, (1, 0, 2)).reshape(ds * db, D_MODEL)
    outs = {k: [] for k in ("kp", "vp", "ks", "vs", "gv", "rp", "ip", "rs", "is", "cp", "cs")}

    for l in range(depth):
        pw, bdb = _ssm_prep(ssm_a_re[l], ssm_a_im[l], ssm_log_dt[l], ssm_b_re[l], ssm_b_im[l])
        c_re, c_im = _ssm_out_matrix(ssm_c_re[l]), _ssm_out_matrix(ssm_c_im[l])
        d_row = row(ssm_d[l])
        gains = (row(g_out_a[l]), row(g_out_b[l]), row(g_out_c[l]))
        last = l == depth - 1

        p = _proj(xp, row(g_mix[l]), w_in_b, l)
        a = _attn_prompt(p)
        b = _gmlp_prompt(p, row(gmlp_gv[l]), gmlp_ws[l], gmlp_bs[l])
        yg, h_last = _s5_prompt(p, bdb, c_re, c_im, d_row, pw)
        xm = _mix_out(a, b, yg, xp, w_glu_b, *gains, w_out_b, l)
        xp, tail = _ffn_prompt(xm, row(g_ffn[l]), w_up_b, conv_w, conv_b, w_down_b, g_fin, l, final_norm=last)
        keep = min(A_WINDOW_MAX, seq)
        outs["kp"].append(p[seq - keep:, A_WIDTH:2 * A_WIDTH].reshape(1, keep, A_HEADS, HEAD_DIM))
        outs["vp"].append(p[seq - keep:, 2 * A_WIDTH:3 * A_WIDTH].reshape(1, keep, A_HEADS, HEAD_DIM))
        st = SSM_LG_STATES
        outs["rp"].append(h_last[:, 0, :st].reshape(1, C_GROUPS, SSM_P))
        outs["ip"].append(h_last[:, 0, st:].reshape(1, C_GROUPS, SSM_P))
        outs["cp"].append(jnp.transpose(tail[-1, :, SUBLANE - 2:, :], (1, 0, 2)).reshape(1, 2, 2 * D_FF))

        ps = _proj(xs, row(g_mix[l]), w_in_b, l)

        def heads(cols):
            return jnp.transpose(cols.reshape(ds, db, A_HEADS, HEAD_DIM), (1, 0, 2, 3)).reshape(
                db, ds * A_HEADS, HEAD_DIM)

        q3, k3, v3 = (heads(ps[:, i * A_WIDTH:(i + 1) * A_WIDTH]) for i in range(3))
        a3 = _attn_sample(q3, k3, v3, cache_win_k, cache_win_v, l)
        a_s = jnp.transpose(a3.reshape(db, ds, A_WIDTH), (1, 0, 2)).reshape(ds * db, A_WIDTH)
        b_s, gv_s = _gmlp_sample(ps, row(gmlp_gv[l]), gmlp_ws[l], gmlp_bs[l])
        yg_s, hre, him = _s5_sample(ps, state_ssm_re[l].reshape(db, n_states), state_ssm_im[l].reshape(db, n_states),
                                    bdb, c_re, c_im, d_row, pw)
        xm_s = _mix_out(a_s, b_s, yg_s, xs, w_glu_b, *gains, w_out_b, l)
        xs, tail_g, tail_v = _ffn_sample(xm_s, row(g_ffn[l]), state_ffn_conv, w_up_b, conv_w, conv_b, w_down_b,
                                         g_fin, l, final_norm=last)
        outs["ks"].append(k3.reshape(db, ds, A_HEADS, HEAD_DIM))
        outs["vs"].append(v3.reshape(db, ds, A_HEADS, HEAD_DIM))
        outs["gv"].append(jnp.transpose(gv_s.reshape(ds, db, B_WIDTH), (1, 0, 2)))
        outs["rs"].append(hre.reshape(db, C_GROUPS, SSM_P))
        outs["is"].append(him.reshape(db, C_GROUPS, SSM_P))
        outs["cs"].append(jnp.concatenate([tail_g, tail_v], axis=-1))

    y_p = xp.reshape(1, seq, D_MODEL)
    y_s = jnp.transpose(xs.reshape(ds, db, D_MODEL), (1, 0, 2))
    st = lambda k: jnp.stack(outs[k])
    return (y_p, y_s, st("kp"), st("vp"), st("ks"), st("vs"), st("gv"),
            st("rp"), st("ip"), st("rs"), st("is"), st("cp"), st("cs"))
```

```python
import functools
import math

import numpy as np
import jax
import jax.numpy as jnp
from jax import lax
from jax.experimental import pallas as pl
from jax.experimental.pallas import tpu as pltpu

F32 = jnp.float32
BF16 = jnp.bfloat16

D_MODEL = 2048
A_WIDTH = 1024
A_HEADS = 8
HEAD_DIM = 128
A_DILATIONS = (1, 4, 16)
A_SPAN = 128
A_WINDOW_MAX = 2048
SAMPLE_TAIL_ROWS = 512
SAMPLE_FAR_QUARTERS = A_WINDOW_MAX // SAMPLE_TAIL_ROWS - 1
SAMPLE_BATCH_PER_STEP = 2
SAMPLE_STRIDE = max(A_DILATIONS)
SAMPLE_GROUPS = SAMPLE_TAIL_ROWS // SAMPLE_STRIDE
ATTN_UNROLL = {1: 32, 4: 32, 16: 16}
B_WIDTH = 512
B_GROUPS = 4
CHUNK = 128
C_WIDTH = 512
C_GROUPS = 32
C_GROUP_W = 16
SSM_P = 64
N_IN = 3 * A_WIDTH + 2 * B_WIDTH + C_WIDTH
D_FF = 5632
DEC_SEQ = 4
EPS = 1e-6
NEG = -1e30

LANE = 128
COL_Q, COL_K, COL_V = 0, A_WIDTH // LANE, 2 * A_WIDTH // LANE
COL_BU = 3 * A_WIDTH // LANE
COL_BV = COL_BU + B_WIDTH // LANE
COL_CU = COL_BV + B_WIDTH // LANE

SSM_LANE_GROUPS = 4
SSM_LG_STATES = C_GROUPS * SSM_P // SSM_LANE_GROUPS
SSM_R = 64
SSM_NC = 32

MIB = 1024 * 1024
SUBLANE = 8

PROJ_TM, PROJ_TN = 1024, 1536
MIX_TM = 512
FFN_TM, FFN_TF = 1024, 512
GMLP_CHUNKS = 16
VMEM_MIB = dict(proj_in=56, attn_prompt=56, attn_sample=48, gmlp=32, s5_prompt=40, s5_sample=32,
                mix_out=56, ffn_prompt=60, ffn_sample=48)


def _params(sem, vmem_mib, flags=None):
    return pltpu.CompilerParams(dimension_semantics=sem, vmem_limit_bytes=vmem_mib * MIB, flags=flags)


def _gelu(x):
    c = math.sqrt(2.0 / math.pi)
    return x * (0.5 * (1.0 + jnp.tanh(c * (x + 0.044715 * (x * x * x)))))


def _rms(x):
    return x * lax.rsqrt(jnp.mean(x * x, axis=-1, keepdims=True) + EPS)


_NT = (((1,), (1,)), ((), ()))


def _proj_body(x_ref, g_ref, w_ref, o_ref, h_ref):
    @pl.when(pl.program_id(1) == 0)
    def _():
        h_ref[...] = (_rms(x_ref[...]) * g_ref[...]).astype(BF16)

    o_ref[...] = jnp.dot(h_ref[...], w_ref[...], preferred_element_type=F32)


def _proj(x, g, w, layer):
    t = x.shape[0]
    tm, tn = min(PROJ_TM, t), PROJ_TN
    return pl.pallas_call(
        _proj_body,
        grid=(t // tm, N_IN // tn),
        in_specs=[
            pl.BlockSpec((tm, D_MODEL), lambda i, j: (i, 0)),
            pl.BlockSpec((1, D_MODEL), lambda i, j: (0, 0)),
            pl.BlockSpec((None, D_MODEL, tn), lambda i, j: (layer, 0, j)),
        ],
        out_specs=pl.BlockSpec((tm, tn), lambda i, j: (i, j)),
        out_shape=jax.ShapeDtypeStruct((t, N_IN), F32),
        scratch_shapes=[pltpu.VMEM((tm, D_MODEL), BF16)],
        compiler_params=_params(("arbitrary", "arbitrary"), VMEM_MIB["proj_in"]),
        name="proj_in",
    )(x, g, w)


def _attn_prompt_body(q_ref, k_ref, v_ref, o_ref, acc_ref, m_ref, l_ref, *, seq):
    scale = math.log2(math.e) / math.sqrt(HEAD_DIM)
    ii = lax.broadcasted_iota(jnp.int32, (A_SPAN, 2 * A_SPAN), 0)
    jj = lax.broadcasted_iota(jnp.int32, (A_SPAN, 2 * A_SPAN), 1)
    band = (jj >= ii) & (jj <= ii + A_SPAN)
    band_cur = band & (jj >= A_SPAN)
    blk = (A_SPAN, A_SPAN)

    for order, dil in enumerate(sorted(A_DILATIONS, reverse=True)):
        nb = seq // (A_SPAN * dil)
        step = A_SPAN * dil

        def rows(start, dil=dil):
            if dil == 1:
                return pl.ds(pl.multiple_of(start, A_SPAN), A_SPAN)
            return pl.ds(start, A_SPAN, stride=dil)

        def runs(run_list, rows=rows, step=step, first=order == 0):
            work = []
            for r, n0, count in run_list:
                static_start = isinstance(n0, int)
                if not (static_start and n0 == 0):
                    prev = r + jnp.maximum(n0 - 1, 0) * step
                    kp, vp = k_ref[rows(prev), :].astype(BF16), v_ref[rows(prev), :].astype(BF16)
                for u in range(count):
                    base = r + (n0 + u) * step
                    q = (q_ref[rows(base), :] * scale).astype(BF16)
                    kc, vc = k_ref[rows(base), :].astype(BF16), v_ref[rows(base), :].astype(BF16)
                    if u == 0 and static_start and n0 == 0:
                        kp, vp, mask = kc, vc, band_cur
                    elif u == 0 and not static_start:
                        mask = band & ((jj >= A_SPAN) | (n0 > 0))
                    else:
                        mask = band
                    s = lax.dot_general(q, jnp.concatenate([kp, kc], axis=0), _NT,
                                        preferred_element_type=F32)
                    s = jnp.where(mask, s, NEG)
                    v2 = jnp.concatenate([vp, vc], axis=0)
                    m_blk = jnp.max(s, axis=1, keepdims=True)
                    if first:
                        m_new = jnp.broadcast_to(m_blk, blk)
                        p = jnp.exp2(s - m_blk)
                        l_new = jnp.broadcast_to(jnp.sum(p, axis=1, keepdims=True), blk)
                        acc_new = jnp.dot(p.astype(BF16), v2, preferred_element_type=F32)
                    else:
                        m_old = m_ref[rows(base), :]
                        m_new = jnp.maximum(m_old, m_blk)
                        alpha = jnp.exp2(m_old - m_new)
                        p = jnp.exp2(s - jnp.concatenate([m_new, m_new], axis=1))
                        l_new = alpha * l_ref[rows(base), :] + jnp.sum(p, axis=1, keepdims=True)
                        acc_new = alpha * acc_ref[rows(base), :] + jnp.dot(p.astype(BF16), v2,
                                                                           preferred_element_type=F32)
                    work.append((base, m_new, l_new, acc_new))
                    kp, vp = kc, vc
            for base, m_new, l_new, acc_new in work:
                m_ref[rows(base), :] = m_new
                l_ref[rows(base), :] = l_new
                acc_ref[rows(base), :] = acc_new

        unroll = min(ATTN_UNROLL[dil], nb * dil)
        if unroll <= nb:
            per_res = nb // unroll

            def trip(i, c, runs=runs, unroll=unroll, per_res=per_res):
                runs([(i // per_res, (i % per_res) * unroll, unroll)])
                return c
        else:
            res_per_trip = unroll // nb

            def trip(i, c, runs=runs, nb=nb, res_per_trip=res_per_trip):
                runs([(i * res_per_trip + j, 0, nb) for j in range(res_per_trip)])
                return c

        lax.fori_loop(0, nb * dil // unroll, trip, 0)

    o_ref[...] = acc_ref[...] / l_ref[...]


def _attn_prompt(p):
    seq = p.shape[0]
    blk = (seq, HEAD_DIM)
    return pl.pallas_call(
        functools.partial(_attn_prompt_body, seq=seq),
        grid=(A_HEADS,),
        in_specs=[
            pl.BlockSpec(blk, lambda h: (0, COL_Q + h)),
            pl.BlockSpec(blk, lambda h: (0, COL_K + h)),
            pl.BlockSpec(blk, lambda h: (0, COL_V + h)),
        ],
        out_specs=pl.BlockSpec(blk, lambda h: (0, h)),
        out_shape=jax.ShapeDtypeStruct((seq, A_WIDTH), F32),
        scratch_shapes=[pltpu.VMEM(blk, F32)] * 3,
        compiler_params=_params(("arbitrary",), VMEM_MIB["attn_prompt"]),
        name="attn_prompt",
    )(p, p, p)


def _sample_attn_masks():
    j = np.arange(DEC_SEQ)[:, None, None, None]
    h = np.arange(A_HEADS)[None, :, None, None]
    hk = np.arange(A_HEADS)[None, None, None, :]
    same = (h == hk)
    i = np.arange(512)[None, None, :, None]
    cnt_l = same * ((i >= 384 + j).astype(np.int32) + ((i - j) % 4 == 0).astype(np.int32))
    jn = np.arange(DEC_SEQ)[None, None, :, None]
    cnt_n = same * ((jn <= j).astype(np.int32) + 2 * (jn == j).astype(np.int32))
    g = np.arange(128)[None, None, :, None]
    head_match = np.broadcast_to(same[0:1], (1, A_HEADS, 128, A_HEADS))
    return (cnt_l.reshape(DEC_SEQ * A_HEADS, 512 * A_HEADS).astype(np.float32),
            cnt_n.reshape(DEC_SEQ * A_HEADS, DEC_SEQ * A_HEADS).astype(np.float32),
            (head_match + 0 * g).reshape(A_HEADS, 128 * A_HEADS).astype(np.float32))


def _attn_sample_body(q_ref, k_ref, v_ref, *refs):
    nh = A_HEADS
    nq = SAMPLE_FAR_QUARTERS
    k_far, kl_ref = refs[:nq], refs[nq]
    v_far, vl_ref = refs[nq + 1:2 * nq + 1], refs[2 * nq + 1]
    cl_ref, cn_ref, hm_ref, o_ref = refs[2 * nq + 2:]
    cl, cn, hm = cl_ref[...], cn_ref[...], hm_ref[...]

    def stride16(far, last, bi, res):
        grp, stride = SAMPLE_GROUPS, SAMPLE_STRIDE
        near = last.reshape(grp, stride, nh, HEAD_DIM)[:, res].reshape(grp * nh, HEAD_DIM)
        parts = [far[qt][bi, :, res].reshape(grp * nh, HEAD_DIM).astype(BF16) for qt in range(nq)]
        return jnp.concatenate(parts + [near.astype(BF16)], axis=0)

    for bi in range(SAMPLE_BATCH_PER_STEP):
        q = q_ref[bi] * (1.0 / math.sqrt(HEAD_DIM))
        qs = q.astype(BF16)
        k_last = kl_ref[bi]
        kl = k_last.reshape(SAMPLE_TAIL_ROWS * nh, HEAD_DIM).astype(BF16)
        s_l = lax.dot_general(qs, kl, _NT, preferred_element_type=F32)
        s_n = lax.dot_general(qs, k_ref[bi].astype(BF16), _NT, preferred_element_type=F32)
        s_a = []
        for res in range(DEC_SEQ):
            kk = stride16(k_far, k_last, bi, res)
            s_a.append(lax.dot_general(q[res * nh:(res + 1) * nh].astype(BF16), kk, _NT,
                                       preferred_element_type=F32))
        m = jnp.max(jnp.where(cl > 0, s_l, NEG), axis=1, keepdims=True)
        m = jnp.maximum(m, jnp.max(jnp.where(cn > 0, s_n, NEG), axis=1, keepdims=True))
        m_a = jnp.concatenate([jnp.max(jnp.where(hm > 0, s, NEG), axis=1, keepdims=True) for s in s_a], axis=0)
        m = jnp.maximum(m, m_a)

        p_l = cl * jnp.exp(jnp.minimum(s_l - m, 0.0))
        p_n = cn * jnp.exp(jnp.minimum(s_n - m, 0.0))
        den = jnp.sum(p_l, axis=1, keepdims=True) + jnp.sum(p_n, axis=1, keepdims=True)
        v_last = vl_ref[bi]
        vl = v_last.reshape(SAMPLE_TAIL_ROWS * nh, HEAD_DIM).astype(BF16)
        out = jnp.dot(p_l.astype(BF16), vl, preferred_element_type=F32)
        out = out + jnp.dot(p_n.astype(BF16), v_ref[bi].astype(BF16), preferred_element_type=F32)
        o_a, d_a = [], []
        for res in range(DEC_SEQ):
            p_a = hm * jnp.exp(jnp.minimum(s_a[res] - m[res * nh:(res + 1) * nh], 0.0))
            d_a.append(jnp.sum(p_a, axis=1, keepdims=True))
            vv = stride16(v_far, v_last, bi, res)
            o_a.append(jnp.dot(p_a.astype(BF16), vv, preferred_element_type=F32))
        out = out + jnp.concatenate(o_a, axis=0)
        den = den + jnp.concatenate(d_a, axis=0)
        o_ref[bi] = out / den


def _attn_sample(q3, k3, v3, cache_k, cache_v, layer):
    db = q3.shape[0]
    depth, _, w_buf, nh, hd = cache_k.shape
    assert (w_buf, nh, hd) == (A_WINDOW_MAX, A_HEADS, HEAD_DIM) and q3.shape[1] == DEC_SEQ * A_HEADS
    cnt_l, cnt_n, head_match = _sample_attn_masks()
    rows = DEC_SEQ * A_HEADS

    nq = SAMPLE_FAR_QUARTERS
    nb = min(SAMPLE_BATCH_PER_STEP, db)
    assert nb == SAMPLE_BATCH_PER_STEP and db % nb == 0

    def strided_specs():
        return [pl.BlockSpec((None, nb, None, SAMPLE_GROUPS, DEC_SEQ, nh, hd), functools.partial(
            lambda s, qt: (layer, s, qt, 0, 0, 0, 0), qt=qt)) for qt in range(nq)]

    last_spec = pl.BlockSpec((None, nb, None, SAMPLE_TAIL_ROWS, nh, hd), lambda s: (layer, s, nq, 0, 0, 0))
    tok_spec = pl.BlockSpec((nb, rows, hd), lambda s: (s, 0, 0))

    def whole(a):
        return pl.BlockSpec(a.shape, lambda s: (0, 0))

    k16 = cache_k.reshape(depth, db, nq + 1, SAMPLE_GROUPS, SAMPLE_STRIDE, nh, hd)
    v16 = cache_v.reshape(depth, db, nq + 1, SAMPLE_GROUPS, SAMPLE_STRIDE, nh, hd)
    k512 = cache_k.reshape(depth, db, nq + 1, SAMPLE_TAIL_ROWS, nh, hd)
    v512 = cache_v.reshape(depth, db, nq + 1, SAMPLE_TAIL_ROWS, nh, hd)
    return pl.pallas_call(
        _attn_sample_body,
        grid=(db // nb,),
        in_specs=[tok_spec, tok_spec, tok_spec] + strided_specs() + [last_spec] + strided_specs() + [last_spec]
        + [whole(cnt_l), whole(cnt_n), whole(head_match)],
        out_specs=tok_spec,
        out_shape=jax.ShapeDtypeStruct((db, rows, hd), F32),
        compiler_params=_params(("arbitrary",), VMEM_MIB["attn_sample"]),
        name="attn_sample",
    )(q3, k3, v3, *([k16] * nq), k512, *([v16] * nq), v512,
      jnp.asarray(cnt_l), jnp.asarray(cnt_n), jnp.asarray(head_match))


def _gmlp_prompt_body(bu_ref, bv_ref, gg_ref, ws_ref, bs_ref, o_ref, *, chunks):
    ri = lax.broadcasted_iota(jnp.int32, (CHUNK, CHUNK), 0)
    ci = lax.broadcasted_iota(jnp.int32, (CHUNK, CHUNK), 1)
    wm = jnp.where(ci <= ri, ws_ref[0], 0.0).astype(BF16)
    bias = bs_ref[0]
    gain = gg_ref[...]
    for c in range(chunks):
        sl = slice(c * CHUNK, (c + 1) * CHUNK)
        gv = _rms(_gelu(bv_ref[sl, :])) * gain
        mix = jnp.dot(wm, gv.astype(BF16), preferred_element_type=F32) + bias
        o_ref[sl, :] = _gelu(bu_ref[sl, :]) * mix


def _gmlp_prompt(p, gain, ws, bs):
    seq = p.shape[0]
    chunks = GMLP_CHUNKS
    tm = chunks * CHUNK
    return pl.pallas_call(
        functools.partial(_gmlp_prompt_body, chunks=chunks),
        grid=(seq // tm, B_GROUPS),
        in_specs=[
            pl.BlockSpec((tm, LANE), lambda i, g: (i, COL_BU + g)),
            pl.BlockSpec((tm, LANE), lambda i, g: (i, COL_BV + g)),
            pl.BlockSpec((1, LANE), lambda i, g: (0, g)),
            pl.BlockSpec((1, CHUNK, CHUNK), lambda i, g: (g, 0, 0)),
            pl.BlockSpec((1, CHUNK, 1), lambda i, g: (g, 0, 0)),
        ],
        out_specs=pl.BlockSpec((tm, LANE), lambda i, g: (i, g)),
        out_shape=jax.ShapeDtypeStruct((seq, B_WIDTH), F32),
        compiler_params=_params(("arbitrary", "arbitrary"), VMEM_MIB["gmlp"]),
        name="gmlp_prompt",
    )(p, p, gain, ws, bs.reshape(B_GROUPS, CHUNK, 1))


def _gmlp_sample_body(ws_ref, bs_ref, bu_ref, bv_ref, gg_ref, o_ref, gv_ref, *, db):
    g = pl.program_id(0)
    gv = _rms(_gelu(bv_ref[...])) * gg_ref[...]
    gv_ref[...] = gv
    gu = _gelu(bu_ref[...])
    for i in range(DEC_SEQ):
        mix = jnp.full((db, LANE), bs_ref[g, i], F32)
        for j in range(i + 1):
            mix = mix + ws_ref[g, i * DEC_SEQ + j] * gv[j * db:(j + 1) * db]
        o_ref[i * db:(i + 1) * db, :] = gu[i * db:(i + 1) * db] * mix


def _gmlp_sample(p, gain, ws, bs):
    t = p.shape[0]
    db = t // DEC_SEQ
    ws4 = ws[:, :DEC_SEQ, :DEC_SEQ].reshape(B_GROUPS, DEC_SEQ * DEC_SEQ)
    bs4 = bs[:, :DEC_SEQ]
    smem = pl.BlockSpec(memory_space=pltpu.SMEM)
    return pl.pallas_call(
        functools.partial(_gmlp_sample_body, db=db),
        grid=(B_GROUPS,),
        in_specs=[
            smem, smem,
            pl.BlockSpec((t, LANE), lambda g: (0, COL_BU + g)),
            pl.BlockSpec((t, LANE), lambda g: (0, COL_BV + g)),
            pl.BlockSpec((1, LANE), lambda g: (0, g)),
        ],
        out_specs=[pl.BlockSpec((t, LANE), lambda g: (0, g))] * 2,
        out_shape=[jax.ShapeDtypeStruct((t, B_WIDTH), F32)] * 2,
        compiler_params=_params(("arbitrary",), VMEM_MIB["gmlp"]),
        name="gmlp_sample",
    )(ws4, bs4, p, p, gain)


def _ssm_prep_body(lre_ref, lim_ref, ldt_ref, bre_ref, bim_ref, pwre_ref, pwim_ref, bbre_ref, bbim_ref):
    lre, lim = lre_ref[...], lim_ref[...]
    dt = jnp.exp(ldt_ref[...])
    k = lax.broadcasted_iota(jnp.int32, pwre_ref.shape, 0).astype(F32) + 1.0
    mag = jnp.exp(k * (dt * lre))
    ang = k * (dt * lim)
    pwre = mag * jnp.cos(ang)
    pwim = mag * jnp.sin(ang)
    pwre_ref[...] = pwre
    pwim_ref[...] = pwim
    xr, xi = pwre[0:1] - 1.0, pwim[0:1]
    den = lre * lre + lim * lim
    cr = (xr * lre + xi * lim) / den
    ci = (xi * lre - xr * lim) / den
    bre, bim = bre_ref[...], bim_ref[...]
    bbre_ref[...] = cr * bre - ci * bim
    bbim_ref[...] = cr * bim + ci * bre


def _ssm_prep(a_re, a_im, log_dt, b_re, b_im):
    n = C_GROUPS * SSM_P
    row = lambda a: a.reshape(1, n)
    bt = lambda b: jnp.transpose(b, (2, 0, 1)).reshape(C_GROUP_W, n)
    shapes = [jax.ShapeDtypeStruct((SSM_R, n), F32)] * 2 + [jax.ShapeDtypeStruct((C_GROUP_W, n), F32)] * 2
    pwre, pwim, bbre, bbim = pl.pallas_call(
        _ssm_prep_body, out_shape=shapes, name="ssm_prep",
    )(row(a_re), row(a_im), row(jnp.repeat(log_dt, SSM_P)), bt(b_re), bt(b_im))
    lg, st = SSM_LANE_GROUPS, SSM_LG_STATES
    split = lambda t: jnp.transpose(t.reshape(t.shape[0], lg, st), (1, 0, 2))
    pw = jnp.concatenate([split(pwre), split(pwim)], axis=-1)
    eye = jnp.eye(C_GROUPS // lg, dtype=F32)

    def blockdiag(bb):
        r = bb.reshape(C_GROUP_W, lg, C_GROUPS // lg, SSM_P)
        return jnp.einsum('ab,mlbp->lambp', eye, r).reshape(lg, LANE, st)

    bdb = jnp.concatenate([blockdiag(bbre), blockdiag(bbim)], axis=-1).astype(BF16)
    return pw, bdb


def _ssm_out_matrix(c):
    lg = SSM_LANE_GROUPS
    eye = jnp.eye(C_GROUPS // lg, dtype=F32)
    r = c.reshape(lg, C_GROUPS // lg, C_GROUP_W, SSM_P)
    return jnp.einsum('ab,lanp->lapbn', eye, r).reshape(lg, SSM_LG_STATES, LANE).astype(BF16)


def _cmul_add(are, aim, hre, him, xre, xim):
    return are * hre - aim * him + xre, are * him + aim * hre + xim


def _s5_prompt_body(u_ref, bdb_ref, cre_ref, cim_ref, d_ref, pw_ref, yg_ref, hl_ref,
                    u3_ref, h_ref, car_ref, hc_ref):
    r_steps, nc, st = SSM_R, SSM_NC, SSM_LG_STATES
    re, im = slice(0, st), slice(st, 2 * st)

    @pl.when(pl.program_id(1) == 0)
    def _():
        hc_ref[...] = jnp.zeros(hc_ref.shape, F32)

    def rows(r):
        return pl.ds(pl.multiple_of(r * nc, nc), nc)

    def regroup(r, c):
        u3_ref[rows(r), :] = u_ref[pl.ds(r, nc, stride=r_steps), :]
        return c

    lax.fori_loop(0, r_steps, regroup, 0)
    h_ref[...] = jnp.dot(u3_ref[...].astype(BF16), bdb_ref[0], preferred_element_type=F32)

    half = st // 2
    for hf in range(2):
        cre_ = slice(hf * half, (hf + 1) * half)
        cim_ = slice(st + hf * half, st + (hf + 1) * half)
        are = jnp.broadcast_to(pw_ref[0, 0:1, cre_], (nc, half))
        aim = jnp.broadcast_to(pw_ref[0, 0:1, cim_], (nc, half))

        def step(r, carry, cre_=cre_, cim_=cim_, are=are, aim=aim):
            hre, him = _cmul_add(are, aim, carry[0], carry[1], h_ref[rows(r), cre_], h_ref[rows(r), cim_])
            h_ref[rows(r), cre_] = hre
            h_ref[rows(r), cim_] = him
            return hre, him

        lax.fori_loop(1, r_steps, step, (h_ref[0:nc, cre_], h_ref[0:nc, cim_]))

    ends = h_ref[(r_steps - 1) * nc:r_steps * nc, :]
    are, aim = pw_ref[0, r_steps - 1:r_steps, re], pw_ref[0, r_steps - 1:r_steps, im]
    cre, cim = hc_ref[0:1, re], hc_ref[0:1, im]
    for c in range(nc):
        car_ref[c:c + 1, re] = cre
        car_ref[c:c + 1, im] = cim
        cre, cim = _cmul_add(are, aim, cre, cim, ends[c:c + 1, re], ends[c:c + 1, im])
    hc_ref[:, re] = jnp.broadcast_to(cre, (SUBLANE, st))
    hc_ref[:, im] = jnp.broadcast_to(cim, (SUBLANE, st))
    hl_ref[0] = hc_ref[...]

    def fix(r, c):
        pre, pim = pw_ref[0, pl.ds(r, 1), re], pw_ref[0, pl.ds(r, 1), im]
        hre, him = _cmul_add(pre, pim, car_ref[:, re], car_ref[:, im], h_ref[rows(r), re], h_ref[rows(r), im])
        h_ref[rows(r), re] = hre
        h_ref[rows(r), im] = him
        return c

    lax.fori_loop(0, r_steps, fix, 0)

    y = (jnp.dot(h_ref[:, re].astype(BF16), cre_ref[0], preferred_element_type=F32)
         - jnp.dot(h_ref[:, im].astype(BF16), cim_ref[0], preferred_element_type=F32)
         + d_ref[...] * u3_ref[...])
    u3_ref[...] = _gelu(y)

    def ungroup(r, c):
        yg_ref[pl.ds(r, nc, stride=r_steps), :] = u3_ref[rows(r), :]
        return c

    lax.fori_loop(0, r_steps, ungroup, 0)


def _s5_prompt(p, bdb, cre, cim, d, pw):
    seq = p.shape[0]
    tseg = SSM_R * SSM_NC
    st2 = 2 * SSM_LG_STATES
    lgs = SSM_LANE_GROUPS
    return pl.pallas_call(
        _s5_prompt_body,
        grid=(lgs, seq // tseg),
        in_specs=[
            pl.BlockSpec((tseg, LANE), lambda g, t: (t, COL_CU + g)),
            pl.BlockSpec((1, LANE, st2), lambda g, t: (g, 0, 0)),
            pl.BlockSpec((1, SSM_LG_STATES, LANE), lambda g, t: (g, 0, 0)),
            pl.BlockSpec((1, SSM_LG_STATES, LANE), lambda g, t: (g, 0, 0)),
            pl.BlockSpec((1, LANE), lambda g, t: (0, g)),
            pl.BlockSpec((1, SSM_R, st2), lambda g, t: (g, 0, 0)),
        ],
        out_specs=[
            pl.BlockSpec((tseg, LANE), lambda g, t: (t, g)),
            pl.BlockSpec((1, SUBLANE, st2), lambda g, t: (g, 0, 0)),
        ],
        out_shape=[jax.ShapeDtypeStruct((seq, C_WIDTH), F32), jax.ShapeDtypeStruct((lgs, SUBLANE, st2), F32)],
        scratch_shapes=[
            pltpu.VMEM((tseg, LANE), F32),
            pltpu.VMEM((tseg, st2), F32),
            pltpu.VMEM((SSM_NC, st2), F32),
            pltpu.VMEM((SUBLANE, st2), F32),
        ],
        compiler_params=_params(("arbitrary", "arbitrary"), VMEM_MIB["s5_prompt"]),
        name="s5_prompt",
    )(p, bdb, cre, cim, d, pw)


def _s5_sample_body(u_ref, hre_ref, him_ref, bdb_ref, cre_ref, cim_ref, d_ref, pw_ref,
                    yg_ref, ore_ref, oim_ref, h_ref, *, db):
    st = SSM_LG_STATES
    re, im = slice(0, st), slice(st, 2 * st)
    u = u_ref[...]
    x = jnp.dot(u.astype(BF16), bdb_ref[0], preferred_element_type=F32)
    are, aim = pw_ref[0, 0:1, re], pw_ref[0, 0:1, im]
    hre, him = hre_ref[...], him_ref[...]
    for j in range(DEC_SEQ):
        rj = slice(j * db, (j + 1) * db)
        hre, him = _cmul_add(are, aim, hre, him, x[rj, re], x[rj, im])
        h_ref[rj, re] = hre
        h_ref[rj, im] = him
    ore_ref[...] = hre
    oim_ref[...] = him
    y = (jnp.dot(h_ref[:, re].astype(BF16), cre_ref[0], preferred_element_type=F32)
         - jnp.dot(h_ref[:, im].astype(BF16), cim_ref[0], preferred_element_type=F32)
         + d_ref[...] * u)
    yg_ref[...] = _gelu(y)


def _s5_sample(p, h0re, h0im, bdb, cre, cim, d, pw):
    t = p.shape[0]
    db = t // DEC_SEQ
    st = SSM_LG_STATES
    lgs = SSM_LANE_GROUPS
    return pl.pallas_call(
        functools.partial(_s5_sample_body, db=db),
        grid=(lgs,),
        in_specs=[
            pl.BlockSpec((t, LANE), lambda g: (0, COL_CU + g)),
            pl.BlockSpec((db, st), lambda g: (0, g)),
            pl.BlockSpec((db, st), lambda g: (0, g)),
            pl.BlockSpec((1, LANE, 2 * st), lambda g: (g, 0, 0)),
            pl.BlockSpec((1, st, LANE), lambda g: (g, 0, 0)),
            pl.BlockSpec((1, st, LANE), lambda g: (g, 0, 0)),
            pl.BlockSpec((1, LANE), lambda g: (0, g)),
            pl.BlockSpec((1, SSM_R, 2 * st), lambda g: (g, 0, 0)),
        ],
        out_specs=[
            pl.BlockSpec((t, LANE), lambda g: (0, g)),
            pl.BlockSpec((db, st), lambda g: (0, g)),
            pl.BlockSpec((db, st), lambda g: (0, g)),
        ],
        out_shape=[jax.ShapeDtypeStruct((t, C_WIDTH), F32),
                   jax.ShapeDtypeStruct((db, lgs * st), F32), jax.ShapeDtypeStruct((db, lgs * st), F32)],
        scratch_shapes=[pltpu.VMEM((t, 2 * st), F32)],
        compiler_params=_params(("arbitrary",), VMEM_MIB["s5_sample"]),
        name="s5_sample",
    )(p, h0re, h0im, bdb, cre, cim, d, pw)


def _mix_out_body(a_ref, b_ref, y_ref, x_ref, wg_ref, ga_ref, gb_ref, gc_ref, wo_ref, o_ref):
    glu = jnp.dot(y_ref[...].astype(BF16), wg_ref[...], preferred_element_type=F32)
    c = glu[:, :C_WIDTH] * jax.nn.sigmoid(glu[:, C_WIDTH:])
    mixed = jnp.concatenate([
        (_rms(a_ref[...]) * ga_ref[...]).astype(BF16),
        (_rms(b_ref[...]) * gb_ref[...]).astype(BF16),
        (_rms(c) * gc_ref[...]).astype(BF16)], axis=1)
    o_ref[...] = x_ref[...] + jnp.dot(mixed, wo_ref[...], preferred_element_type=F32)


def _mix_out(a, b, y, x, w_glu, g_a, g_b, g_c, w_out, layer):
    t = x.shape[0]
    tm = min(MIX_TM, t)
    row = lambda w: pl.BlockSpec((tm, w), lambda i: (i, 0))
    whole = lambda arr: pl.BlockSpec(arr.shape, lambda i: (0, 0))
    of_layer = lambda arr: pl.BlockSpec((None,) + arr.shape[1:], lambda i: (layer, 0, 0))
    return pl.pallas_call(
        _mix_out_body,
        grid=(t // tm,),
        in_specs=[row(A_WIDTH), row(B_WIDTH), row(C_WIDTH), row(D_MODEL),
                  of_layer(w_glu), whole(g_a), whole(g_b), whole(g_c), of_layer(w_out)],
        out_specs=row(D_MODEL),
        out_shape=jax.ShapeDtypeStruct((t, D_MODEL), F32),
        compiler_params=_params(("arbitrary",), VMEM_MIB["mix_out"]),
        name="mix_out",
    )(a, b, y, x, w_glu, g_a, g_b, g_c, w_out)


FFN_HALO = 16


def _conv_gate(ug, uv, cwg, cwv, cbg, cbv, taps):
    def conv(u, cw, cb):
        return ((cb + cw[0:1, :] * taps[0](u)) + cw[1:2, :] * taps[1](u)) + cw[2:3, :] * taps[2](u)
    gate = conv(ug, cwg, cbg)
    return (gate * jax.nn.sigmoid(gate)) * conv(uv, cwv, cbv)


def _finish(o_ref, gf_ref, is_last_tile, final_norm):
    if final_norm:
        @pl.when(is_last_tile)
        def _():
            o_ref[...] = _rms(o_ref[...]) * gf_ref[...]


def _ffn_prompt_body(x_ref, halo_ref, g_ref, wg_ref, wv_ref, cwg_ref, cwv_ref, cbg_ref, cbv_ref, wd_ref, gf_ref,
                     o_ref, tail_ref, h_ref, ug_ref, uv_ref, *, tm, final_norm):
    i, f = pl.program_id(0), pl.program_id(1)
    hl = FFN_HALO

    @pl.when(f == 0)
    def _():
        h_ref[hl:, :] = (_rms(x_ref[...]) * g_ref[...]).astype(BF16)
        prev = _rms(halo_ref[...]) * g_ref[...]
        h_ref[0:hl, :] = jnp.where(i > 0, prev, 0.0).astype(BF16)
        o_ref[...] = x_ref[...]

    h = h_ref[...]
    ug_ref[...] = jnp.dot(h, wg_ref[...], preferred_element_type=F32)
    uv_ref[...] = jnp.dot(h, wv_ref[...], preferred_element_type=F32)
    taps = tuple((lambda u_ref, off=off: u_ref[pl.ds(hl - 2 + off, tm), :]) for off in range(3))
    act = _conv_gate(ug_ref, uv_ref, cwg_ref[...], cwv_ref[...], cbg_ref[...], cbv_ref[...], taps)
    o_ref[...] += jnp.dot(act.astype(BF16), wd_ref[...], preferred_element_type=F32)
    tail_ref[0] = ug_ref[pl.ds(hl + tm - SUBLANE, SUBLANE), :]
    tail_ref[1] = uv_ref[pl.ds(hl + tm - SUBLANE, SUBLANE), :]
    _finish(o_ref, gf_ref, f == pl.num_programs(1) - 1, final_norm)


def _ffn_prompt(x, g, w_up, conv_w, conv_b, w_down, g_final, layer, final_norm):
    t = x.shape[0]
    tm, tf = min(FFN_TM, t), FFN_TF
    nf = D_FF // tf
    hl = FFN_HALO
    cb = conv_b.reshape(conv_b.shape[0], 1, 2 * D_FF)
    return pl.pallas_call(
        functools.partial(_ffn_prompt_body, tm=tm, final_norm=final_norm),
        grid=(t // tm, nf),
        in_specs=[
            pl.BlockSpec((tm, D_MODEL), lambda i, f: (i, 0)),
            pl.BlockSpec((hl, D_MODEL), lambda i, f: (jnp.maximum(i * (tm // hl) - 1, 0), 0)),
            pl.BlockSpec((1, D_MODEL), lambda i, f: (0, 0)),
            pl.BlockSpec((None, D_MODEL, tf), lambda i, f: (layer, 0, f)),
            pl.BlockSpec((None, D_MODEL, tf), lambda i, f: (layer, 0, nf + f)),
            pl.BlockSpec((None, 3, tf), lambda i, f: (layer, 0, f)),
            pl.BlockSpec((None, 3, tf), lambda i, f: (layer, 0, nf + f)),
            pl.BlockSpec((None, 1, tf), lambda i, f: (layer, 0, f)),
            pl.BlockSpec((None, 1, tf), lambda i, f: (layer, 0, nf + f)),
            pl.BlockSpec((None, tf, D_MODEL), lambda i, f: (layer, f, 0)),
            pl.BlockSpec((1, D_MODEL), lambda i, f: (0, 0)),
        ],
        out_specs=[
            pl.BlockSpec((tm, D_MODEL), lambda i, f: (i, 0), pipeline_mode=pl.Buffered(1)),
            pl.BlockSpec((None, 2, SUBLANE, tf), lambda i, f: (i, 0, 0, f)),
        ],
        out_shape=[jax.ShapeDtypeStruct((t, D_MODEL), F32), jax.ShapeDtypeStruct((t // tm, 2, SUBLANE, D_FF), F32)],
        scratch_shapes=[pltpu.VMEM((hl + tm, D_MODEL), BF16)] + [pltpu.VMEM((hl + tm, tf), F32)] * 2,
        compiler_params=_params(("arbitrary", "arbitrary"), VMEM_MIB["ffn_prompt"]),
        name="ffn_prompt",
    )(x, x, g, w_up, w_up, conv_w, conv_w, cb, cb, w_down, g_final)


def _ffn_sample_body(x_ref, g_ref, sg_ref, sv_ref, wg_ref, wv_ref, cwg_ref, cwv_ref, cbg_ref, cbv_ref, wd_ref, gf_ref,
                     o_ref, tg_ref, tv_ref, h_ref, *, db, final_norm):
    f = pl.program_id(0)
    t = DEC_SEQ * db

    @pl.when(f == 0)
    def _():
        h_ref[...] = (_rms(x_ref[...]) * g_ref[...]).astype(BF16)
        o_ref[...] = x_ref[...]

    h = h_ref[...]
    ug = jnp.dot(h, wg_ref[...], preferred_element_type=F32)
    uv = jnp.dot(h, wv_ref[...], preferred_element_type=F32)
    pg = jnp.concatenate([sg_ref[:, 0, :], sg_ref[:, 1, :], ug], axis=0)
    pv = jnp.concatenate([sv_ref[:, 0, :], sv_ref[:, 1, :], uv], axis=0)
    taps = (lambda u: u[0:t], lambda u: u[db:db + t], lambda u: u[2 * db:2 * db + t])
    act = _conv_gate(pg, pv, cwg_ref[...], cwv_ref[...], cbg_ref[...], cbv_ref[...], taps)
    o_ref[...] += jnp.dot(act.astype(BF16), wd_ref[...], preferred_element_type=F32)

    for j in range(2):
        rows = slice((DEC_SEQ - 2 + j) * db, (DEC_SEQ - 1 + j) * db)
        tg_ref[:, j, :] = ug[rows]
        tv_ref[:, j, :] = uv[rows]
    _finish(o_ref, gf_ref, f == pl.num_programs(0) - 1, final_norm)


def _ffn_sample(x, g, state, w_up, conv_w, conv_b, w_down, g_final, layer, final_norm, tf=FFN_TF):
    t = x.shape[0]
    db = t // DEC_SEQ
    nf = D_FF // tf
    cb = conv_b.reshape(conv_b.shape[0], 1, 2 * D_FF)
    return pl.pallas_call(
        functools.partial(_ffn_sample_body, db=db, final_norm=final_norm),
        grid=(nf,),
        in_specs=[
            pl.BlockSpec((t, D_MODEL), lambda f: (0, 0)),
            pl.BlockSpec((1, D_MODEL), lambda f: (0, 0)),
            pl.BlockSpec((None, db, 2, tf), lambda f: (layer, 0, 0, f)),
            pl.BlockSpec((None, db, 2, tf), lambda f: (layer, 0, 0, nf + f)),
            pl.BlockSpec((None, D_MODEL, tf), lambda f: (layer, 0, f)),
            pl.BlockSpec((None, D_MODEL, tf), lambda f: (layer, 0, nf + f)),
            pl.BlockSpec((None, 3, tf), lambda f: (layer, 0, f)),
            pl.BlockSpec((None, 3, tf), lambda f: (layer, 0, nf + f)),
            pl.BlockSpec((None, 1, tf), lambda f: (layer, 0, f)),
            pl.BlockSpec((None, 1, tf), lambda f: (layer, 0, nf + f)),
            pl.BlockSpec((None, tf, D_MODEL), lambda f: (layer, f, 0)),
            pl.BlockSpec((1, D_MODEL), lambda f: (0, 0)),
        ],
        out_specs=[
            pl.BlockSpec((t, D_MODEL), lambda f: (0, 0)),
            pl.BlockSpec((db, 2, tf), lambda f: (0, 0, f)),
            pl.BlockSpec((db, 2, tf), lambda f: (0, 0, f)),
        ],
        out_shape=[jax.ShapeDtypeStruct((t, D_MODEL), F32)] + [jax.ShapeDtypeStruct((db, 2, D_FF), F32)] * 2,
        scratch_shapes=[pltpu.VMEM((t, D_MODEL), BF16)],
        compiler_params=_params(("arbitrary",), VMEM_MIB["ffn_sample"]),
        name="ffn_sample",
    )(x, g, state, state, w_up, w_up, conv_w, conv_w, cb, cb, w_down, g_final)


def kernel(x_prompt, x_sample, cache_win_k, cache_win_v, state_ssm_re, state_ssm_im, state_ffn_conv,
           g_mix, w_in, g_out_a, g_out_b, g_out_c, gmlp_gv, gmlp_ws, gmlp_bs, ssm_a_re, ssm_a_im,
           ssm_log_dt, ssm_b_re, ssm_b_im, ssm_c_re, ssm_c_im, ssm_d, ssm_w_glu, w_out, g_ffn, w_up,
           conv_w, conv_b, w_down, g_final):
    depth = w_in.shape[0]
    bsz, seq, _ = x_prompt.shape
    db, ds, _ = x_sample.shape
    assert bsz == 1 and ds == DEC_SEQ and seq % (SSM_R * SSM_NC) == 0 and seq >= A_WINDOW_MAX
    n_states = C_GROUPS * SSM_P
    row = lambda v: v.reshape(1, -1)

    g_fin = row(g_final)
    w_in_b, w_glu_b, w_out_b = w_in.astype(BF16), ssm_w_glu.astype(BF16), w_out.astype(BF16)
    w_up_b, w_down_b = w_up.astype(BF16), w_down.astype(BF16)
    xp = x_prompt.reshape(seq, D_MODEL)
    xs = jnp.transpose(x_sample, (1, 0, 2)).reshape(ds * db, D_MODEL)
    outs = {k: [] for k in ("kp", "vp", "ks", "vs", "gv", "rp", "ip", "rs", "is", "cp", "cs")}

    for l in range(depth):
        pw, bdb = _ssm_prep(ssm_a_re[l], ssm_a_im[l], ssm_log_dt[l], ssm_b_re[l], ssm_b_im[l])
        c_re, c_im = _ssm_out_matrix(ssm_c_re[l]), _ssm_out_matrix(ssm_c_im[l])
        d_row = row(ssm_d[l])
        gains = (row(g_out_a[l]), row(g_out_b[l]), row(g_out_c[l]))
        last = l == depth - 1

        p = _proj(xp, row(g_mix[l]), w_in_b, l)
        a = _attn_prompt(p)
        b = _gmlp_prompt(p, row(gmlp_gv[l]), gmlp_ws[l], gmlp_bs[l])
        yg, h_last = _s5_prompt(p, bdb, c_re, c_im, d_row, pw)
        xm = _mix_out(a, b, yg, xp, w_glu_b, *gains, w_out_b, l)
        xp, tail = _ffn_prompt(xm, row(g_ffn[l]), w_up_b, conv_w, conv_b, w_down_b, g_fin, l, final_norm=last)
        keep = min(A_WINDOW_MAX, seq)
        outs["kp"].append(p[seq - keep:, A_WIDTH:2 * A_WIDTH].reshape(1, keep, A_HEADS, HEAD_DIM))
        outs["vp"].append(p[seq - keep:, 2 * A_WIDTH:3 * A_WIDTH].reshape(1, keep, A_HEADS, HEAD_DIM))
        st = SSM_LG_STATES
        outs["rp"].append(h_last[:, 0, :st].reshape(1, C_GROUPS, SSM_P))
        outs["ip"].append(h_last[:, 0, st:].reshape(1, C_GROUPS, SSM_P))
        outs["cp"].append(jnp.transpose(tail[-1, :, SUBLANE - 2:, :], (1, 0, 2)).reshape(1, 2, 2 * D_FF))

        ps = _proj(xs, row(g_mix[l]), w_in_b, l)

        def heads(cols):
            return jnp.transpose(cols.reshape(ds, db, A_HEADS, HEAD_DIM), (1, 0, 2, 3)).reshape(
                db, ds * A_HEADS, HEAD_DIM)

        q3, k3, v3 = (heads(ps[:, i * A_WIDTH:(i + 1) * A_WIDTH]) for i in range(3))
        a3 = _attn_sample(q3, k3, v3, cache_win_k, cache_win_v, l)
        a_s = jnp.transpose(a3.reshape(db, ds, A_WIDTH), (1, 0, 2)).reshape(ds * db, A_WIDTH)
        b_s, gv_s = _gmlp_sample(ps, row(gmlp_gv[l]), gmlp_ws[l], gmlp_bs[l])
        yg_s, hre, him = _s5_sample(ps, state_ssm_re[l].reshape(db, n_states), state_ssm_im[l].reshape(db, n_states),
                                    bdb, c_re, c_im, d_row, pw)
        xm_s = _mix_out(a_s, b_s, yg_s, xs, w_glu_b, *gains, w_out_b, l)
        xs, tail_g, tail_v = _ffn_sample(xm_s, row(g_ffn[l]), state_ffn_conv, w_up_b, conv_w, conv_b, w_down_b,
                                         g_fin, l, final_norm=last)
        outs["ks"].append(k3.reshape(db, ds, A_HEADS, HEAD_DIM))
        outs["vs"].append(v3.reshape(db, ds, A_HEADS, HEAD_DIM))
        outs["gv"].append(jnp.transpose(gv_s.reshape(ds, db, B_WIDTH), (1, 0, 2)))
        outs["rs"].append(hre.reshape(db, C_GROUPS, SSM_P))
        outs["is"].append(him.reshape(db, C_GROUPS, SSM_P))
        outs["cs"].append(jnp.concatenate([tail_g, tail_v], axis=-1))

    y_p = xp.reshape(1, seq, D_MODEL)
    y_s = jnp.transpose(xs.reshape(ds, db, D_MODEL), (1, 0, 2))
    st = lambda k: jnp.stack(outs[k])
    return (y_p, y_s, st("kp"), st("vp"), st("ks"), st("vs"), st("gv"),
            st("rp"), st("ip"), st("rs"), st("is"), st("cp"), st("cs"))
```

```python
import functools
import math

import numpy as np
import jax
import jax.numpy as jnp
from jax import lax
from jax.experimental import pallas as pl
from jax.experimental.pallas import tpu as pltpu

F32 = jnp.float32
BF16 = jnp.bfloat16

D_MODEL = 2048
A_WIDTH = 1024
A_HEADS = 8
HEAD_DIM = 128
A_DILATIONS = (1, 4, 16)
A_SPAN = 128
A_WINDOW_MAX = 2048
SAMPLE_TAIL_ROWS = 512
SAMPLE_FAR_QUARTERS = A_WINDOW_MAX // SAMPLE_TAIL_ROWS - 1
SAMPLE_BATCH_PER_STEP = 2
SAMPLE_STRIDE = max(A_DILATIONS)
SAMPLE_GROUPS = SAMPLE_TAIL_ROWS // SAMPLE_STRIDE
ATTN_UNROLL = {1: 32, 4: 32, 16: 32}
B_WIDTH = 512
B_GROUPS = 4
CHUNK = 128
C_WIDTH = 512
C_GROUPS = 32
C_GROUP_W = 16
SSM_P = 64
N_IN = 3 * A_WIDTH + 2 * B_WIDTH + C_WIDTH
D_FF = 5632
DEC_SEQ = 4
EPS = 1e-6
NEG = -1e30

LANE = 128
COL_Q, COL_K, COL_V = 0, A_WIDTH // LANE, 2 * A_WIDTH // LANE
COL_BU = 3 * A_WIDTH // LANE
COL_BV = COL_BU + B_WIDTH // LANE
COL_CU = COL_BV + B_WIDTH // LANE

SSM_LANE_GROUPS = 4
SSM_LG_STATES = C_GROUPS * SSM_P // SSM_LANE_GROUPS
SSM_R = 64
SSM_NC = 32

MIB = 1024 * 1024
SUBLANE = 8

PROJ_TM, PROJ_TN = 1024, 1536
MIX_TM = 512
FFN_TM, FFN_TF = 1024, 512
GMLP_CHUNKS = 32
VMEM_MIB = dict(proj_in=56, attn_prompt=56, attn_sample=48, gmlp=32, s5_prompt=40, s5_sample=32,
                mix_out=56, ffn_prompt=60, ffn_sample=48)


def _params(sem, vmem_mib, flags=None):
    return pltpu.CompilerParams(dimension_semantics=sem, vmem_limit_bytes=vmem_mib * MIB, flags=flags)


def _gelu(x):
    c = math.sqrt(2.0 / math.pi)
    return x * (0.5 * (1.0 + jnp.tanh(c * (x + 0.044715 * (x * x * x)))))


def _rms(x):
    return x * lax.rsqrt(jnp.mean(x * x, axis=-1, keepdims=True) + EPS)


_NT = (((1,), (1,)), ((), ()))


def _proj_body(x_ref, g_ref, w_ref, o_ref, h_ref):
    @pl.when(pl.program_id(1) == 0)
    def _():
        h_ref[...] = (_rms(x_ref[...]) * g_ref[...]).astype(BF16)

    o_ref[...] = jnp.dot(h_ref[...], w_ref[...], preferred_element_type=F32)


def _proj(x, g, w, layer):
    t = x.shape[0]
    tm, tn = min(PROJ_TM, t), PROJ_TN
    return pl.pallas_call(
        _proj_body,
        grid=(t // tm, N_IN // tn),
        in_specs=[
            pl.BlockSpec((tm, D_MODEL), lambda i, j: (i, 0)),
            pl.BlockSpec((1, D_MODEL), lambda i, j: (0, 0)),
            pl.BlockSpec((None, D_MODEL, tn), lambda i, j: (layer, 0, j)),
        ],
        out_specs=pl.BlockSpec((tm, tn), lambda i, j: (i, j)),
        out_shape=jax.ShapeDtypeStruct((t, N_IN), F32),
        scratch_shapes=[pltpu.VMEM((tm, D_MODEL), BF16)],
        compiler_params=_params(("arbitrary", "arbitrary"), VMEM_MIB["proj_in"]),
        name="proj_in",
    )(x, g, w)


def _attn_prompt_body(q_ref, k_ref, v_ref, o_ref, acc_ref, m_ref, l_ref, *, seq):
    scale = math.log2(math.e) / math.sqrt(HEAD_DIM)
    ii = lax.broadcasted_iota(jnp.int32, (A_SPAN, 2 * A_SPAN), 0)
    jj = lax.broadcasted_iota(jnp.int32, (A_SPAN, 2 * A_SPAN), 1)
    band = (jj >= ii) & (jj <= ii + A_SPAN)
    band_cur = band & (jj >= A_SPAN)
    blk = (A_SPAN, A_SPAN)

    for order, dil in enumerate(sorted(A_DILATIONS, reverse=True)):
        nb = seq // (A_SPAN * dil)
        step = A_SPAN * dil

        def rows(start, dil=dil):
            if dil == 1:
                return pl.ds(pl.multiple_of(start, A_SPAN), A_SPAN)
            return pl.ds(start, A_SPAN, stride=dil)

        def runs(run_list, rows=rows, step=step, first=order == 0):
            work = []
            for r, n0, count in run_list:
                static_start = isinstance(n0, int)
                if not (static_start and n0 == 0):
                    prev = r + jnp.maximum(n0 - 1, 0) * step
                    kp, vp = k_ref[rows(prev), :].astype(BF16), v_ref[rows(prev), :].astype(BF16)
                for u in range(count):
                    base = r + (n0 + u) * step
                    q = (q_ref[rows(base), :] * scale).astype(BF16)
                    kc, vc = k_ref[rows(base), :].astype(BF16), v_ref[rows(base), :].astype(BF16)
                    if u == 0 and static_start and n0 == 0:
                        kp, vp, mask = kc, vc, band_cur
                    elif u == 0 and not static_start:
                        mask = band & ((jj >= A_SPAN) | (n0 > 0))
                    else:
                        mask = band
                    s = lax.dot_general(q, jnp.concatenate([kp, kc], axis=0), _NT,
                                        preferred_element_type=F32)
                    s = jnp.where(mask, s, NEG)
                    v2 = jnp.concatenate([vp, vc], axis=0)
                    m_blk = jnp.max(s, axis=1, keepdims=True)
                    if first:
                        m_new = jnp.broadcast_to(m_blk, blk)
                        p = jnp.exp2(s - m_blk)
                        l_new = jnp.broadcast_to(jnp.sum(p, axis=1, keepdims=True), blk)
                        acc_new = jnp.dot(p.astype(BF16), v2, preferred_element_type=F32)
                    else:
                        m_old = m_ref[rows(base), :]
                        m_new = jnp.maximum(m_old, m_blk)
                        alpha = jnp.exp2(m_old - m_new)
                        p = jnp.exp2(s - jnp.concatenate([m_new, m_new], axis=1))
                        l_new = alpha * l_ref[rows(base), :] + jnp.sum(p, axis=1, keepdims=True)
                        acc_new = alpha * acc_ref[rows(base), :] + jnp.dot(p.astype(BF16), v2,
                                                                           preferred_element_type=F32)
                    work.append((base, m_new, l_new, acc_new))
                    kp, vp = kc, vc
            for base, m_new, l_new, acc_new in work:
                m_ref[rows(base), :] = m_new
                l_ref[rows(base), :] = l_new
                acc_ref[rows(base), :] = acc_new

        unroll = min(ATTN_UNROLL[dil], nb * dil)
        if unroll <= nb:
            per_res = nb // unroll

            def trip(i, c, runs=runs, unroll=unroll, per_res=per_res):
                runs([(i // per_res, (i % per_res) * unroll, unroll)])
                return c
        else:
            res_per_trip = unroll // nb

            def trip(i, c, runs=runs, nb=nb, res_per_trip=res_per_trip):
                runs([(i * res_per_trip + j, 0, nb) for j in range(res_per_trip)])
                return c

        lax.fori_loop(0, nb * dil // unroll, trip, 0)

    o_ref[...] = acc_ref[...] / l_ref[...]


def _attn_prompt(p):
    seq = p.shape[0]
    blk = (seq, HEAD_DIM)
    return pl.pallas_call(
        functools.partial(_attn_prompt_body, seq=seq),
        grid=(A_HEADS,),
        in_specs=[
            pl.BlockSpec(blk, lambda h: (0, COL_Q + h)),
            pl.BlockSpec(blk, lambda h: (0, COL_K + h)),
            pl.BlockSpec(blk, lambda h: (0, COL_V + h)),
        ],
        out_specs=pl.BlockSpec(blk, lambda h: (0, h)),
        out_shape=jax.ShapeDtypeStruct((seq, A_WIDTH), F32),
        scratch_shapes=[pltpu.VMEM(blk, F32)] * 3,
        compiler_params=_params(("arbitrary",), VMEM_MIB["attn_prompt"]),
        name="attn_prompt",
    )(p, p, p)


def _sample_attn_masks():
    j = np.arange(DEC_SEQ)[:, None, None, None]
    h = np.arange(A_HEADS)[None, :, None, None]
    hk = np.arange(A_HEADS)[None, None, None, :]
    same = (h == hk)
    i = np.arange(512)[None, None, :, None]
    cnt_l = same * ((i >= 384 + j).astype(np.int32) + ((i - j) % 4 == 0).astype(np.int32))
    jn = np.arange(DEC_SEQ)[None, None, :, None]
    cnt_n = same * ((jn <= j).astype(np.int32) + 2 * (jn == j).astype(np.int32))
    g = np.arange(128)[None, None, :, None]
    head_match = np.broadcast_to(same[0:1], (1, A_HEADS, 128, A_HEADS))
    return (cnt_l.reshape(DEC_SEQ * A_HEADS, 512 * A_HEADS).astype(np.float32),
            cnt_n.reshape(DEC_SEQ * A_HEADS, DEC_SEQ * A_HEADS).astype(np.float32),
            (head_match + 0 * g).reshape(A_HEADS, 128 * A_HEADS).astype(np.float32))


def _attn_sample_body(q_ref, k_ref, v_ref, *refs):
    nh = A_HEADS
    nq = SAMPLE_FAR_QUARTERS
    k_far, kl_ref = refs[:nq], refs[nq]
    v_far, vl_ref = refs[nq + 1:2 * nq + 1], refs[2 * nq + 1]
    cl_ref, cn_ref, hm_ref, o_ref = refs[2 * nq + 2:]
    cl, cn, hm = cl_ref[...], cn_ref[...], hm_ref[...]

    def stride16(far, last, bi, res):
        grp, stride = SAMPLE_GROUPS, SAMPLE_STRIDE
        near = last.reshape(grp, stride, nh, HEAD_DIM)[:, res].reshape(grp * nh, HEAD_DIM)
        parts = [far[qt][bi, :, res].reshape(grp * nh, HEAD_DIM).astype(BF16) for qt in range(nq)]
        return jnp.concatenate(parts + [near.astype(BF16)], axis=0)

    for bi in range(SAMPLE_BATCH_PER_STEP):
        q = q_ref[bi] * (1.0 / math.sqrt(HEAD_DIM))
        qs = q.astype(BF16)
        k_last = kl_ref[bi]
        kl = k_last.reshape(SAMPLE_TAIL_ROWS * nh, HEAD_DIM).astype(BF16)
        s_l = lax.dot_general(qs, kl, _NT, preferred_element_type=F32)
        s_n = lax.dot_general(qs, k_ref[bi].astype(BF16), _NT, preferred_element_type=F32)
        s_a = []
        for res in range(DEC_SEQ):
            kk = stride16(k_far, k_last, bi, res)
            s_a.append(lax.dot_general(q[res * nh:(res + 1) * nh].astype(BF16), kk, _NT,
                                       preferred_element_type=F32))
        m = jnp.max(jnp.where(cl > 0, s_l, NEG), axis=1, keepdims=True)
        m = jnp.maximum(m, jnp.max(jnp.where(cn > 0, s_n, NEG), axis=1, keepdims=True))
        m_a = jnp.concatenate([jnp.max(jnp.where(hm > 0, s, NEG), axis=1, keepdims=True) for s in s_a], axis=0)
        m = jnp.maximum(m, m_a)

        p_l = cl * jnp.exp(jnp.minimum(s_l - m, 0.0))
        p_n = cn * jnp.exp(jnp.minimum(s_n - m, 0.0))
        den = jnp.sum(p_l, axis=1, keepdims=True) + jnp.sum(p_n, axis=1, keepdims=True)
        v_last = vl_ref[bi]
        vl = v_last.reshape(SAMPLE_TAIL_ROWS * nh, HEAD_DIM).astype(BF16)
        out = jnp.dot(p_l.astype(BF16), vl, preferred_element_type=F32)
        out = out + jnp.dot(p_n.astype(BF16), v_ref[bi].astype(BF16), preferred_element_type=F32)
        o_a, d_a = [], []
        for res in range(DEC_SEQ):
            p_a = hm * jnp.exp(jnp.minimum(s_a[res] - m[res * nh:(res + 1) * nh], 0.0))
            d_a.append(jnp.sum(p_a, axis=1, keepdims=True))
            vv = stride16(v_far, v_last, bi, res)
            o_a.append(jnp.dot(p_a.astype(BF16), vv, preferred_element_type=F32))
        out = out + jnp.concatenate(o_a, axis=0)
        den = den + jnp.concatenate(d_a, axis=0)
        o_ref[bi] = out / den


def _attn_sample(q3, k3, v3, cache_k, cache_v, layer):
    db = q3.shape[0]
    depth, _, w_buf, nh, hd = cache_k.shape
    assert (w_buf, nh, hd) == (A_WINDOW_MAX, A_HEADS, HEAD_DIM) and q3.shape[1] == DEC_SEQ * A_HEADS
    cnt_l, cnt_n, head_match = _sample_attn_masks()
    rows = DEC_SEQ * A_HEADS

    nq = SAMPLE_FAR_QUARTERS
    nb = min(SAMPLE_BATCH_PER_STEP, db)
    assert nb == SAMPLE_BATCH_PER_STEP and db % nb == 0

    def strided_specs():
        return [pl.BlockSpec((None, nb, None, SAMPLE_GROUPS, DEC_SEQ, nh, hd), functools.partial(
            lambda s, qt: (layer, s, qt, 0, 0, 0, 0), qt=qt)) for qt in range(nq)]

    last_spec = pl.BlockSpec((None, nb, None, SAMPLE_TAIL_ROWS, nh, hd), lambda s: (layer, s, nq, 0, 0, 0))
    tok_spec = pl.BlockSpec((nb, rows, hd), lambda s: (s, 0, 0))

    def whole(a):
        return pl.BlockSpec(a.shape, lambda s: (0, 0))

    k16 = cache_k.reshape(depth, db, nq + 1, SAMPLE_GROUPS, SAMPLE_STRIDE, nh, hd)
    v16 = cache_v.reshape(depth, db, nq + 1, SAMPLE_GROUPS, SAMPLE_STRIDE, nh, hd)
    k512 = cache_k.reshape(depth, db, nq + 1, SAMPLE_TAIL_ROWS, nh, hd)
    v512 = cache_v.reshape(depth, db, nq + 1, SAMPLE_TAIL_ROWS, nh, hd)
    return pl.pallas_call(
        _attn_sample_body,
        grid=(db // nb,),
        in_specs=[tok_spec, tok_spec, tok_spec] + strided_specs() + [last_spec] + strided_specs() + [last_spec]
        + [whole(cnt_l), whole(cnt_n), whole(head_match)],
        out_specs=tok_spec,
        out_shape=jax.ShapeDtypeStruct((db, rows, hd), F32),
        compiler_params=_params(("arbitrary",), VMEM_MIB["attn_sample"]),
        name="attn_sample",
    )(q3, k3, v3, *([k16] * nq), k512, *([v16] * nq), v512,
      jnp.asarray(cnt_l), jnp.asarray(cnt_n), jnp.asarray(head_match))


def _gmlp_prompt_body(bu_ref, bv_ref, gg_ref, ws_ref, bs_ref, o_ref, *, chunks):
    ri = lax.broadcasted_iota(jnp.int32, (CHUNK, CHUNK), 0)
    ci = lax.broadcasted_iota(jnp.int32, (CHUNK, CHUNK), 1)
    wm = jnp.where(ci <= ri, ws_ref[0], 0.0).astype(BF16)
    bias = bs_ref[0]
    gain = gg_ref[...]
    for c in range(chunks):
        sl = slice(c * CHUNK, (c + 1) * CHUNK)
        gv = _rms(_gelu(bv_ref[sl, :])) * gain
        mix = jnp.dot(wm, gv.astype(BF16), preferred_element_type=F32) + bias
        o_ref[sl, :] = _gelu(bu_ref[sl, :]) * mix


def _gmlp_prompt(p, gain, ws, bs):
    seq = p.shape[0]
    chunks = GMLP_CHUNKS
    tm = chunks * CHUNK
    return pl.pallas_call(
        functools.partial(_gmlp_prompt_body, chunks=chunks),
        grid=(seq // tm, B_GROUPS),
        in_specs=[
            pl.BlockSpec((tm, LANE), lambda i, g: (i, COL_BU + g)),
            pl.BlockSpec((tm, LANE), lambda i, g: (i, COL_BV + g)),
            pl.BlockSpec((1, LANE), lambda i, g: (0, g)),
            pl.BlockSpec((1, CHUNK, CHUNK), lambda i, g: (g, 0, 0)),
            pl.BlockSpec((1, CHUNK, 1), lambda i, g: (g, 0, 0)),
        ],
        out_specs=pl.BlockSpec((tm, LANE), lambda i, g: (i, g)),
        out_shape=jax.ShapeDtypeStruct((seq, B_WIDTH), F32),
        compiler_params=_params(("arbitrary", "arbitrary"), VMEM_MIB["gmlp"]),
        name="gmlp_prompt",
    )(p, p, gain, ws, bs.reshape(B_GROUPS, CHUNK, 1))


def _gmlp_sample_body(ws_ref, bs_ref, bu_ref, bv_ref, gg_ref, o_ref, gv_ref, *, db):
    g = pl.program_id(0)
    gv = _rms(_gelu(bv_ref[...])) * gg_ref[...]
    gv_ref[...] = gv
    gu = _gelu(bu_ref[...])
    for i in range(DEC_SEQ):
        mix = jnp.full((db, LANE), bs_ref[g, i], F32)
        for j in range(i + 1):
            mix = mix + ws_ref[g, i * DEC_SEQ + j] * gv[j * db:(j + 1) * db]
        o_ref[i * db:(i + 1) * db, :] = gu[i * db:(i + 1) * db] * mix


def _gmlp_sample(p, gain, ws, bs):
    t = p.shape[0]
    db = t // DEC_SEQ
    ws4 = ws[:, :DEC_SEQ, :DEC_SEQ].reshape(B_GROUPS, DEC_SEQ * DEC_SEQ)
    bs4 = bs[:, :DEC_SEQ]
    smem = pl.BlockSpec(memory_space=pltpu.SMEM)
    return pl.pallas_call(
        functools.partial(_gmlp_sample_body, db=db),
        grid=(B_GROUPS,),
        in_specs=[
            smem, smem,
            pl.BlockSpec((t, LANE), lambda g: (0, COL_BU + g)),
            pl.BlockSpec((t, LANE), lambda g: (0, COL_BV + g)),
            pl.BlockSpec((1, LANE), lambda g: (0, g)),
        ],
        out_specs=[pl.BlockSpec((t, LANE), lambda g: (0, g))] * 2,
        out_shape=[jax.ShapeDtypeStruct((t, B_WIDTH), F32)] * 2,
        compiler_params=_params(("arbitrary",), VMEM_MIB["gmlp"]),
        name="gmlp_sample",
    )(ws4, bs4, p, p, gain)


def _ssm_prep_body(lre_ref, lim_ref, ldt_ref, bre_ref, bim_ref, pwre_ref, pwim_ref, bbre_ref, bbim_ref):
    lre, lim = lre_ref[...], lim_ref[...]
    dt = jnp.exp(ldt_ref[...])
    k = lax.broadcasted_iota(jnp.int32, pwre_ref.shape, 0).astype(F32) + 1.0
    mag = jnp.exp(k * (dt * lre))
    ang = k * (dt * lim)
    pwre = mag * jnp.cos(ang)
    pwim = mag * jnp.sin(ang)
    pwre_ref[...] = pwre
    pwim_ref[...] = pwim
    xr, xi = pwre[0:1] - 1.0, pwim[0:1]
    den = lre * lre + lim * lim
    cr = (xr * lre + xi * lim) / den
    ci = (xi * lre - xr * lim) / den
    bre, bim = bre_ref[...], bim_ref[...]
    bbre_ref[...] = cr * bre - ci * bim
    bbim_ref[...] = cr * bim + ci * bre


def _ssm_prep(a_re, a_im, log_dt, b_re, b_im):
    n = C_GROUPS * SSM_P
    row = lambda a: a.reshape(1, n)
    bt = lambda b: jnp.transpose(b, (2, 0, 1)).reshape(C_GROUP_W, n)
    shapes = [jax.ShapeDtypeStruct((SSM_R, n), F32)] * 2 + [jax.ShapeDtypeStruct((C_GROUP_W, n), F32)] * 2
    pwre, pwim, bbre, bbim = pl.pallas_call(
        _ssm_prep_body, out_shape=shapes, name="ssm_prep",
    )(row(a_re), row(a_im), row(jnp.repeat(log_dt, SSM_P)), bt(b_re), bt(b_im))
    lg, st = SSM_LANE_GROUPS, SSM_LG_STATES
    split = lambda t: jnp.transpose(t.reshape(t.shape[0], lg, st), (1, 0, 2))
    pw = jnp.concatenate([split(pwre), split(pwim)], axis=-1)
    eye = jnp.eye(C_GROUPS // lg, dtype=F32)

    def blockdiag(bb):
        r = bb.reshape(C_GROUP_W, lg, C_GROUPS // lg, SSM_P)
        return jnp.einsum('ab,mlbp->lambp', eye, r).reshape(lg, LANE, st)

    bdb = jnp.concatenate([blockdiag(bbre), blockdiag(bbim)], axis=-1).astype(BF16)
    return pw, bdb


def _ssm_out_matrix(c):
    lg = SSM_LANE_GROUPS
    eye = jnp.eye(C_GROUPS // lg, dtype=F32)
    r = c.reshape(lg, C_GROUPS // lg, C_GROUP_W, SSM_P)
    return jnp.einsum('ab,lanp->lapbn', eye, r).reshape(lg, SSM_LG_STATES, LANE).astype(BF16)


def _cmul_add(are, aim, hre, him, xre, xim):
    return are * hre - aim * him + xre, are * him + aim * hre + xim


def _s5_prompt_body(u_ref, bdb_ref, cre_ref, cim_ref, d_ref, pw_ref, yg_ref, hl_ref,
                    u3_ref, h_ref, car_ref, hc_ref):
    r_steps, nc, st = SSM_R, SSM_NC, SSM_LG_STATES
    re, im = slice(0, st), slice(st, 2 * st)

    @pl.when(pl.program_id(1) == 0)
    def _():
        hc_ref[...] = jnp.zeros(hc_ref.shape, F32)

    def rows(r):
        return pl.ds(pl.multiple_of(r * nc, nc), nc)

    def regroup(r, c):
        u3_ref[rows(r), :] = u_ref[pl.ds(r, nc, stride=r_steps), :]
        return c

    lax.fori_loop(0, r_steps, regroup, 0)
    h_ref[...] = jnp.dot(u3_ref[...].astype(BF16), bdb_ref[0], preferred_element_type=F32)

    half = st // 2
    for hf in range(2):
        cre_ = slice(hf * half, (hf + 1) * half)
        cim_ = slice(st + hf * half, st + (hf + 1) * half)
        are = jnp.broadcast_to(pw_ref[0, 0:1, cre_], (nc, half))
        aim = jnp.broadcast_to(pw_ref[0, 0:1, cim_], (nc, half))

        def step(r, carry, cre_=cre_, cim_=cim_, are=are, aim=aim):
            hre, him = _cmul_add(are, aim, carry[0], carry[1], h_ref[rows(r), cre_], h_ref[rows(r), cim_])
            h_ref[rows(r), cre_] = hre
            h_ref[rows(r), cim_] = him
            return hre, him

        lax.fori_loop(1, r_steps, step, (h_ref[0:nc, cre_], h_ref[0:nc, cim_]))

    ends = h_ref[(r_steps - 1) * nc:r_steps * nc, :]
    are, aim = pw_ref[0, r_steps - 1:r_steps, re], pw_ref[0, r_steps - 1:r_steps, im]
    cre, cim = hc_ref[0:1, re], hc_ref[0:1, im]
    for c in range(nc):
        car_ref[c:c + 1, re] = cre
        car_ref[c:c + 1, im] = cim
        cre, cim = _cmul_add(are, aim, cre, cim, ends[c:c + 1, re], ends[c:c + 1, im])
    hc_ref[:, re] = jnp.broadcast_to(cre, (SUBLANE, st))
    hc_ref[:, im] = jnp.broadcast_to(cim, (SUBLANE, st))
    hl_ref[0] = hc_ref[...]

    def fix(r, c):
        pre, pim = pw_ref[0, pl.ds(r, 1), re], pw_ref[0, pl.ds(r, 1), im]
        hre, him = _cmul_add(pre, pim, car_ref[:, re], car_ref[:, im], h_ref[rows(r), re], h_ref[rows(r), im])
        h_ref[rows(r), re] = hre
        h_ref[rows(r), im] = him
        return c

    lax.fori_loop(0, r_steps, fix, 0)

    y = (jnp.dot(h_ref[:, re].astype(BF16), cre_ref[0], preferred_element_type=F32)
         - jnp.dot(h_ref[:, im].astype(BF16), cim_ref[0], preferred_element_type=F32)
         + d_ref[...] * u3_ref[...])
    u3_ref[...] = _gelu(y)

    def ungroup(r, c):
        yg_ref[pl.ds(r, nc, stride=r_steps), :] = u3_ref[rows(r), :]
        return c

    lax.fori_loop(0, r_steps, ungroup, 0)


def _s5_prompt(p, bdb, cre, cim, d, pw):
    seq = p.shape[0]
    tseg = SSM_R * SSM_NC
    st2 = 2 * SSM_LG_STATES
    lgs = SSM_LANE_GROUPS
    return pl.pallas_call(
        _s5_prompt_body,
        grid=(lgs, seq // tseg),
        in_specs=[
            pl.BlockSpec((tseg, LANE), lambda g, t: (t, COL_CU + g)),
            pl.BlockSpec((1, LANE, st2), lambda g, t: (g, 0, 0)),
            pl.BlockSpec((1, SSM_LG_STATES, LANE), lambda g, t: (g, 0, 0)),
            pl.BlockSpec((1, SSM_LG_STATES, LANE), lambda g, t: (g, 0, 0)),
            pl.BlockSpec((1, LANE), lambda g, t: (0, g)),
            pl.BlockSpec((1, SSM_R, st2), lambda g, t: (g, 0, 0)),
        ],
        out_specs=[
            pl.BlockSpec((tseg, LANE), lambda g, t: (t, g)),
            pl.BlockSpec((1, SUBLANE, st2), lambda g, t: (g, 0, 0)),
        ],
        out_shape=[jax.ShapeDtypeStruct((seq, C_WIDTH), F32), jax.ShapeDtypeStruct((lgs, SUBLANE, st2), F32)],
        scratch_shapes=[
            pltpu.VMEM((tseg, LANE), F32),
            pltpu.VMEM((tseg, st2), F32),
            pltpu.VMEM((SSM_NC, st2), F32),
            pltpu.VMEM((SUBLANE, st2), F32),
        ],
        compiler_params=_params(("arbitrary", "arbitrary"), VMEM_MIB["s5_prompt"]),
        name="s5_prompt",
    )(p, bdb, cre, cim, d, pw)


def _s5_sample_body(u_ref, hre_ref, him_ref, bdb_ref, cre_ref, cim_ref, d_ref, pw_ref,
                    yg_ref, ore_ref, oim_ref, h_ref, *, db):
    st = SSM_LG_STATES
    re, im = slice(0, st), slice(st, 2 * st)
    u = u_ref[...]
    x = jnp.dot(u.astype(BF16), bdb_ref[0], preferred_element_type=F32)
    are, aim = pw_ref[0, 0:1, re], pw_ref[0, 0:1, im]
    hre, him = hre_ref[...], him_ref[...]
    for j in range(DEC_SEQ):
        rj = slice(j * db, (j + 1) * db)
        hre, him = _cmul_add(are, aim, hre, him, x[rj, re], x[rj, im])
        h_ref[rj, re] = hre
        h_ref[rj, im] = him
    ore_ref[...] = hre
    oim_ref[...] = him
    y = (jnp.dot(h_ref[:, re].astype(BF16), cre_ref[0], preferred_element_type=F32)
         - jnp.dot(h_ref[:, im].astype(BF16), cim_ref[0], preferred_element_type=F32)
         + d_ref[...] * u)
    yg_ref[...] = _gelu(y)


def _s5_sample(p, h0re, h0im, bdb, cre, cim, d, pw):
    t = p.shape[0]
    db = t // DEC_SEQ
    st = SSM_LG_STATES
    lgs = SSM_LANE_GROUPS
    return pl.pallas_call(
        functools.partial(_s5_sample_body, db=db),
        grid=(lgs,),
        in_specs=[
            pl.BlockSpec((t, LANE), lambda g: (0, COL_CU + g)),
            pl.BlockSpec((db, st), lambda g: (0, g)),
            pl.BlockSpec((db, st), lambda g: (0, g)),
            pl.BlockSpec((1, LANE, 2 * st), lambda g: (g, 0, 0)),
            pl.BlockSpec((1, st, LANE), lambda g: (g, 0, 0)),
            pl.BlockSpec((1, st, LANE), lambda g: (g, 0, 0)),
            pl.BlockSpec((1, LANE), lambda g: (0, g)),
            pl.BlockSpec((1, SSM_R, 2 * st), lambda g: (g, 0, 0)),
        ],
        out_specs=[
            pl.BlockSpec((t, LANE), lambda g: (0, g)),
            pl.BlockSpec((db, st), lambda g: (0, g)),
            pl.BlockSpec((db, st), lambda g: (0, g)),
        ],
        out_shape=[jax.ShapeDtypeStruct((t, C_WIDTH), F32),
                   jax.ShapeDtypeStruct((db, lgs * st), F32), jax.ShapeDtypeStruct((db, lgs * st), F32)],
        scratch_shapes=[pltpu.VMEM((t, 2 * st), F32)],
        compiler_params=_params(("arbitrary",), VMEM_MIB["s5_sample"]),
        name="s5_sample",
    )(p, h0re, h0im, bdb, cre, cim, d, pw)


def _mix_out_body(a_ref, b_ref, y_ref, x_ref, wg_ref, ga_ref, gb_ref, gc_ref, wo_ref, o_ref):
    glu = jnp.dot(y_ref[...].astype(BF16), wg_ref[...], preferred_element_type=F32)
    c = glu[:, :C_WIDTH] * jax.nn.sigmoid(glu[:, C_WIDTH:])
    mixed = jnp.concatenate([
        (_rms(a_ref[...]) * ga_ref[...]).astype(BF16),
        (_rms(b_ref[...]) * gb_ref[...]).astype(BF16),
        (_rms(c) * gc_ref[...]).astype(BF16)], axis=1)
    o_ref[...] = x_ref[...] + jnp.dot(mixed, wo_ref[...], preferred_element_type=F32)


def _mix_out(a, b, y, x, w_glu, g_a, g_b, g_c, w_out, layer):
    t = x.shape[0]
    tm = min(MIX_TM, t)
    row = lambda w: pl.BlockSpec((tm, w), lambda i: (i, 0))
    whole = lambda arr: pl.BlockSpec(arr.shape, lambda i: (0, 0))
    of_layer = lambda arr: pl.BlockSpec((None,) + arr.shape[1:], lambda i: (layer, 0, 0))
    return pl.pallas_call(
        _mix_out_body,
        grid=(t // tm,),
        in_specs=[row(A_WIDTH), row(B_WIDTH), row(C_WIDTH), row(D_MODEL),
                  of_layer(w_glu), whole(g_a), whole(g_b), whole(g_c), of_layer(w_out)],
        out_specs=row(D_MODEL),
        out_shape=jax.ShapeDtypeStruct((t, D_MODEL), F32),
        compiler_params=_params(("arbitrary",), VMEM_MIB["mix_out"]),
        name="mix_out",
    )(a, b, y, x, w_glu, g_a, g_b, g_c, w_out)


FFN_HALO = 16


def _conv_gate(ug, uv, cwg, cwv, cbg, cbv, taps):
    def conv(u, cw, cb):
        return ((cb + cw[0:1, :] * taps[0](u)) + cw[1:2, :] * taps[1](u)) + cw[2:3, :] * taps[2](u)
    gate = conv(ug, cwg, cbg)
    return (gate * jax.nn.sigmoid(gate)) * conv(uv, cwv, cbv)


def _finish(o_ref, gf_ref, is_last_tile, final_norm):
    if final_norm:
        @pl.when(is_last_tile)
        def _():
            o_ref[...] = _rms(o_ref[...]) * gf_ref[...]


def _ffn_prompt_body(x_ref, halo_ref, g_ref, wg_ref, wv_ref, cwg_ref, cwv_ref, cbg_ref, cbv_ref, wd_ref, gf_ref,
                     o_ref, tail_ref, h_ref, ug_ref, uv_ref, *, tm, final_norm):
    i, f = pl.program_id(0), pl.program_id(1)
    hl = FFN_HALO

    @pl.when(f == 0)
    def _():
        h_ref[hl:, :] = (_rms(x_ref[...]) * g_ref[...]).astype(BF16)
        prev = _rms(halo_ref[...]) * g_ref[...]
        h_ref[0:hl, :] = jnp.where(i > 0, prev, 0.0).astype(BF16)
        o_ref[...] = x_ref[...]

    h = h_ref[...]
    ug_ref[...] = jnp.dot(h, wg_ref[...], preferred_element_type=F32)
    uv_ref[...] = jnp.dot(h, wv_ref[...], preferred_element_type=F32)
    taps = tuple((lambda u_ref, off=off: u_ref[pl.ds(hl - 2 + off, tm), :]) for off in range(3))
    act = _conv_gate(ug_ref, uv_ref, cwg_ref[...], cwv_ref[...], cbg_ref[...], cbv_ref[...], taps)
    o_ref[...] += jnp.dot(act.astype(BF16), wd_ref[...], preferred_element_type=F32)
    tail_ref[0] = ug_ref[pl.ds(hl + tm - SUBLANE, SUBLANE), :]
    tail_ref[1] = uv_ref[pl.ds(hl + tm - SUBLANE, SUBLANE), :]
    _finish(o_ref, gf_ref, f == pl.num_programs(1) - 1, final_norm)


def _ffn_prompt(x, g, w_up, conv_w, conv_b, w_down, g_final, layer, final_norm):
    t = x.shape[0]
    tm, tf = min(FFN_TM, t), FFN_TF
    nf = D_FF // tf
    hl = FFN_HALO
    cb = conv_b.reshape(conv_b.shape[0], 1, 2 * D_FF)
    return pl.pallas_call(
        functools.partial(_ffn_prompt_body, tm=tm, final_norm=final_norm),
        grid=(t // tm, nf),
        in_specs=[
            pl.BlockSpec((tm, D_MODEL), lambda i, f: (i, 0)),
            pl.BlockSpec((hl, D_MODEL), lambda i, f: (jnp.maximum(i * (tm // hl) - 1, 0), 0)),
            pl.BlockSpec((1, D_MODEL), lambda i, f: (0, 0)),
            pl.BlockSpec((None, D_MODEL, tf), lambda i, f: (layer, 0, f)),
            pl.BlockSpec((None, D_MODEL, tf), lambda i, f: (layer, 0, nf + f)),
            pl.BlockSpec((None, 3, tf), lambda i, f: (layer, 0, f)),
            pl.BlockSpec((None, 3, tf), lambda i, f: (layer, 0, nf + f)),
            pl.BlockSpec((None, 1, tf), lambda i, f: (layer, 0, f)),
            pl.BlockSpec((None, 1, tf), lambda i, f: (layer, 0, nf + f)),
            pl.BlockSpec((None, tf, D_MODEL), lambda i, f: (layer, f, 0)),
            pl.BlockSpec((1, D_MODEL), lambda i, f: (0, 0)),
        ],
        out_specs=[
            pl.BlockSpec((tm, D_MODEL), lambda i, f: (i, 0), pipeline_mode=pl.Buffered(1)),
            pl.BlockSpec((None, 2, SUBLANE, tf), lambda i, f: (i, 0, 0, f)),
        ],
        out_shape=[jax.ShapeDtypeStruct((t, D_MODEL), F32), jax.ShapeDtypeStruct((t // tm, 2, SUBLANE, D_FF), F32)],
        scratch_shapes=[pltpu.VMEM((hl + tm, D_MODEL), BF16)] + [pltpu.VMEM((hl + tm, tf), F32)] * 2,
        compiler_params=_params(("arbitrary", "arbitrary"), VMEM_MIB["ffn_prompt"]),
        name="ffn_prompt",
    )(x, x, g, w_up, w_up, conv_w, conv_w, cb, cb, w_down, g_final)


def _ffn_sample_body(x_ref, g_ref, sg_ref, sv_ref, wg_ref, wv_ref, cwg_ref, cwv_ref, cbg_ref, cbv_ref, wd_ref, gf_ref,
                     o_ref, tg_ref, tv_ref, h_ref, *, db, final_norm):
    f = pl.program_id(0)
    t = DEC_SEQ * db

    @pl.when(f == 0)
    def _():
        h_ref[...] = (_rms(x_ref[...]) * g_ref[...]).astype(BF16)
        o_ref[...] = x_ref[...]

    h = h_ref[...]
    ug = jnp.dot(h, wg_ref[...], preferred_element_type=F32)
    uv = jnp.dot(h, wv_ref[...], preferred_element_type=F32)
    pg = jnp.concatenate([sg_ref[:, 0, :], sg_ref[:, 1, :], ug], axis=0)
    pv = jnp.concatenate([sv_ref[:, 0, :], sv_ref[:, 1, :], uv], axis=0)
    taps = (lambda u: u[0:t], lambda u: u[db:db + t], lambda u: u[2 * db:2 * db + t])
    act = _conv_gate(pg, pv, cwg_ref[...], cwv_ref[...], cbg_ref[...], cbv_ref[...], taps)
    o_ref[...] += jnp.dot(act.astype(BF16), wd_ref[...], preferred_element_type=F32)

    for j in range(2):
        rows = slice((DEC_SEQ - 2 + j) * db, (DEC_SEQ - 1 + j) * db)
        tg_ref[:, j, :] = ug[rows]
        tv_ref[:, j, :] = uv[rows]
    _finish(o_ref, gf_ref, f == pl.num_programs(0) - 1, final_norm)


def _ffn_sample(x, g, state, w_up, conv_w, conv_b, w_down, g_final, layer, final_norm, tf=FFN_TF):
    t = x.shape[0]
    db = t // DEC_SEQ
    nf = D_FF // tf
    cb = conv_b.reshape(conv_b.shape[0], 1, 2 * D_FF)
    return pl.pallas_call(
        functools.partial(_ffn_sample_body, db=db, final_norm=final_norm),
        grid=(nf,),
        in_specs=[
            pl.BlockSpec((t, D_MODEL), lambda f: (0, 0)),
            pl.BlockSpec((1, D_MODEL), lambda f: (0, 0)),
            pl.BlockSpec((None, db, 2, tf), lambda f: (layer, 0, 0, f)),
            pl.BlockSpec((None, db, 2, tf), lambda f: (layer, 0, 0, nf + f)),
            pl.BlockSpec((None, D_MODEL, tf), lambda f: (layer, 0, f)),
            pl.BlockSpec((None, D_MODEL, tf), lambda f: (layer, 0, nf + f)),
            pl.BlockSpec((None, 3, tf), lambda f: (layer, 0, f)),
            pl.BlockSpec((None, 3, tf), lambda f: (layer, 0, nf + f)),
            pl.BlockSpec((None, 1, tf), lambda f: (layer, 0, f)),
            pl.BlockSpec((None, 1, tf), lambda f: (layer, 0, nf + f)),
            pl.BlockSpec((None, tf, D_MODEL), lambda f: (layer, f, 0)),
            pl.BlockSpec((1, D_MODEL), lambda f: (0, 0)),
        ],
        out_specs=[
            pl.BlockSpec((t, D_MODEL), lambda f: (0, 0)),
            pl.BlockSpec((db, 2, tf), lambda f: (0, 0, f)),
            pl.BlockSpec((db, 2, tf), lambda f: (0, 0, f)),
        ],
        out_shape=[jax.ShapeDtypeStruct((t, D_MODEL), F32)] + [jax.ShapeDtypeStruct((db, 2, D_FF), F32)] * 2,
        scratch_shapes=[pltpu.VMEM((t, D_MODEL), BF16)],
        compiler_params=_params(("arbitrary",), VMEM_MIB["ffn_sample"]),
        name="ffn_sample",
    )(x, g, state, state, w_up, w_up, conv_w, conv_w, cb, cb, w_down, g_final)


def kernel(x_prompt, x_sample, cache_win_k, cache_win_v, state_ssm_re, state_ssm_im, state_ffn_conv,
           g_mix, w_in, g_out_a, g_out_b, g_out_c, gmlp_gv, gmlp_ws, gmlp_bs, ssm_a_re, ssm_a_im,
           ssm_log_dt, ssm_b_re, ssm_b_im, ssm_c_re, ssm_c_im, ssm_d, ssm_w_glu, w_out, g_ffn, w_up,
           conv_w, conv_b, w_down, g_final):
    depth = w_in.shape[0]
    bsz, seq, _ = x_prompt.shape
    db, ds, _ = x_sample.shape
    assert bsz == 1 and ds == DEC_SEQ and seq % (SSM_R * SSM_NC) == 0 and seq >= A_WINDOW_MAX
    n_states = C_GROUPS * SSM_P
    row = lambda v: v.reshape(1, -1)

    g_fin = row(g_final)
    w_in_b, w_glu_b, w_out_b = w_in.astype(BF16), ssm_w_glu.astype(BF16), w_out.astype(BF16)
    w_up_b, w_down_b = w_up.astype(BF16), w_down.astype(BF16)
    xp = x_prompt.reshape(seq, D_MODEL)
    xs = jnp.transpose(x_sample, (1, 0, 2)).reshape(ds * db, D_MODEL)
    outs = {k: [] for k in ("kp", "vp", "ks", "vs", "gv", "rp", "ip", "rs", "is", "cp", "cs")}

    for l in range(depth):
        pw, bdb = _ssm_prep(ssm_a_re[l], ssm_a_im[l], ssm_log_dt[l], ssm_b_re[l], ssm_b_im[l])
        c_re, c_im = _ssm_out_matrix(ssm_c_re[l]), _ssm_out_matrix(ssm_c_im[l])
        d_row = row(ssm_d[l])
        gains = (row(g_out_a[l]), row(g_out_b[l]), row(g_out_c[l]))
        last = l == depth - 1

        p = _proj(xp, row(g_mix[l]), w_in_b, l)
        a = _attn_prompt(p)
        b = _gmlp_prompt(p, row(gmlp_gv[l]), gmlp_ws[l], gmlp_bs[l])
        yg, h_last = _s5_prompt(p, bdb, c_re, c_im, d_row, pw)
        xm = _mix_out(a, b, yg, xp, w_glu_b, *gains, w_out_b, l)
        xp, tail = _ffn_prompt(xm, row(g_ffn[l]), w_up_b, conv_w, conv_b, w_down_b, g_fin, l, final_norm=last)
        keep = min(A_WINDOW_MAX, seq)
        outs["kp"].append(p[seq - keep:, A_WIDTH:2 * A_WIDTH].reshape(1, keep, A_HEADS, HEAD_DIM))
        outs["vp"].append(p[seq - keep:, 2 * A_WIDTH:3 * A_WIDTH].reshape(1, keep, A_HEADS, HEAD_DIM))
        st = SSM_LG_STATES
        outs["rp"].append(h_last[:, 0, :st].reshape(1, C_GROUPS, SSM_P))
        outs["ip"].append(h_last[:, 0, st:].reshape(1, C_GROUPS, SSM_P))
        outs["cp"].append(jnp.transpose(tail[-1, :, SUBLANE - 2:, :], (1, 0, 2)).reshape(1, 2, 2 * D_FF))

        ps = _proj(xs, row(g_mix[l]), w_in_b, l)

        def heads(cols):
            return jnp.transpose(cols.reshape(ds, db, A_HEADS, HEAD_DIM), (1, 0, 2, 3)).reshape(
                db, ds * A_HEADS, HEAD_DIM)

        q3, k3, v3 = (heads(ps[:, i * A_WIDTH:(i + 1) * A_WIDTH]) for i in range(3))
        a3 = _attn_sample(q3, k3, v3, cache_win_k, cache_win_v, l)
        a_s = jnp.transpose(a3.reshape(db, ds, A_WIDTH), (1, 0, 2)).reshape(ds * db, A_WIDTH)
        b_s, gv_s = _gmlp_sample(ps, row(gmlp_gv[l]), gmlp_ws[l], gmlp_bs[l])
        yg_s, hre, him = _s5_sample(ps, state_ssm_re[l].reshape(db, n_states), state_ssm_im[l].reshape(db, n_states),
                                    bdb, c_re, c_im, d_row, pw)
        xm_s = _mix_out(a_s, b_s, yg_s, xs, w_glu_b, *gains, w_out_b, l)
        xs, tail_g, tail_v = _ffn_sample(xm_s, row(g_ffn[l]), state_ffn_conv, w_up_b, conv_w, conv_b, w_down_b,
                                         g_fin, l, final_norm=last)
        outs["ks"].append(k3.reshape(db, ds, A_HEADS, HEAD_DIM))
        outs["vs"].append(v3.reshape(db, ds, A_HEADS, HEAD_DIM))
        outs["gv"].append(jnp.transpose(gv_s.reshape(ds, db, B_WIDTH), (1, 0, 2)))
        outs["rs"].append(hre.reshape(db, C_GROUPS, SSM_P))
        outs["is"].append(him.reshape(db, C_GROUPS, SSM_P))
        outs["cs"].append(jnp.concatenate([tail_g, tail_v], axis=-1))

    y_p = xp.reshape(1, seq, D_MODEL)
    y_s = jnp.transpose(xs.reshape(ds, db, D_MODEL), (1, 0, 2))
    st = lambda k: jnp.stack(outs[k])
    return (y_p, y_s, st("kp"), st("vp"), st("ks"), st("vs"), st("gv"),
            st("rp"), st("ip"), st("rs"), st("is"), st("cp"), st("cs"))
```
